```python
import jax, jax.numpy as jnp
from jax import lax
import numpy as np

D_MODEL = 1024
BATCH = 2
SEQ = 8192
DEPTH = 1

GRID_W = 64
CTX_LEN = 256
EPS = 1e-6

D_A = D_MODEL // 2
G_A = 4
CHUNK = 128

N_HEADS = 8
N_KV = 2
REP = N_HEADS // N_KV
HEAD_DIM = 64
D_Q = N_HEADS * HEAD_DIM
D_KV = N_KV * HEAD_DIM
WINDOW = 128
WBLK = 128
ROPE_THETA = 10000.0

D_IN = 2 * D_A + D_Q + 2 * D_KV + 2 * D_MODEL
SPLITS = (D_A, 2 * D_A, 2 * D_A + D_Q, 2 * D_A + D_Q + D_KV,
          2 * D_A + D_Q + 2 * D_KV, 2 * D_A + D_Q + 2 * D_KV + D_MODEL)
KV_LO = SPLITS[2]
KV_HI = SPLITS[4]

N_EXPERTS = 256
TOP_K = 8
D_EXPERT = D_MODEL // 4
D_SHARED = D_MODEL // 4
ROUTE_SCALE = 2.5
MOE_BLK = 128

kernel_name = 'hybrid_gmlp_swa_moe_dit_block'


def rmsnorm(x, g):
    x32 = x.astype(jnp.float32)
    y = x32 * lax.rsqrt(jnp.mean(x32 * x32, axis=-1, keepdims=True) + EPS)
    return y.astype(x.dtype) * g


def layernorm(x, g, b):
    x32 = x.astype(jnp.float32)
    mu = jnp.mean(x32, axis=-1, keepdims=True)
    var = jnp.mean(jnp.square(x32 - mu), axis=-1, keepdims=True)
    return ((x32 - mu) * lax.rsqrt(var + EPS)).astype(x.dtype) * g + b


def modulate(h, shift, scale):
    return h * (1 + scale) + shift


def ada_mod(cond, w, b):
    m = jax.nn.silu(cond) @ w + b
    return jnp.split(m[:, None, :], 6, axis=-1)


def axial_rope_tables(L, dtype):
    rows_n = L // GRID_W
    row, col = jnp.meshgrid(jnp.arange(rows_n), jnp.arange(GRID_W), indexing='ij')
    n_freq = HEAD_DIM // 4
    inv = ROPE_THETA ** (-jnp.arange(n_freq, dtype=jnp.float32) / n_freq)
    ang_r = row.reshape(-1)[:, None].astype(jnp.float32) * inv
    ang_c = col.reshape(-1)[:, None].astype(jnp.float32) * inv
    return (jnp.cos(ang_r).astype(dtype), jnp.sin(ang_r).astype(dtype),
            jnp.cos(ang_c).astype(dtype), jnp.sin(ang_c).astype(dtype))


def apply_rope(t, tables):
    cr, sr, cc, sc = tables

    def rot(xh, cos, sin):
        x1, x2 = jnp.split(xh, 2, axis=-1)
        cos = cos[None, :, None, :]
        sin = sin[None, :, None, :]
        return jnp.concatenate([x1 * cos - x2 * sin, x2 * cos + x1 * sin], axis=-1)

    t_row, t_col = jnp.split(t, 2, axis=-1)
    return jnp.concatenate([rot(t_row, cr, sr), rot(t_col, cc, sc)], axis=-1)


def heads(t, n):
    B, L, _ = t.shape
    return t.reshape(B, L, n, HEAD_DIM)


def split_proj(z):
    u, v, q, k, val, ga, gb = jnp.split(z, SPLITS, axis=-1)
    return u, v, heads(q, N_HEADS), heads(k, N_KV), heads(val, N_KV), ga, gb


def chunk_gmlp(u, v, ln_g, ln_b, w_s, b_s):
    B, L, _ = u.shape
    u = jax.nn.gelu(u)
    v = layernorm(jax.nn.gelu(v), ln_g, ln_b)
    vb = v.reshape(B, L // CHUNK, CHUNK, G_A, D_A // G_A)
    s = jnp.einsum('gpq,bnqgc->bnpgc', w_s, vb) + b_s.T[None, None, :, :, None]
    return u * s.reshape(B, L, D_A)


def window_attention(q, k, v, kc, vc, sink):
    B, L, _, _ = q.shape
    nb = L // WBLK
    scale = HEAD_DIM ** -0.5
    qb = q.reshape(B, nb, WBLK, N_KV, REP, HEAD_DIM)

    def band(t):
        tp = jnp.pad(t, ((0, 0), (WBLK, WBLK), (0, 0), (0, 0))).reshape(B, nb + 2, WBLK, N_KV, HEAD_DIM)
        return jnp.concatenate([tp[:, :-2], tp[:, 1:-1], tp[:, 2:]], axis=2)

    kw, vw = band(k), band(v)
    s_win = jnp.einsum('bnqhrd,bnkhd->bnhrqk', qb, kw).astype(jnp.float32) * scale
    q_pos = jnp.arange(nb)[:, None] * WBLK + jnp.arange(WBLK)[None, :]
    k_pos = jnp.arange(nb)[:, None] * WBLK - WBLK + jnp.arange(3 * WBLK)[None, :]
    kp = k_pos[:, None, :]
    valid = (kp >= 0) & (kp < L) & (jnp.abs(kp - q_pos[:, :, None]) <= WINDOW)
    s_win = jnp.where(valid[None, :, None, None], s_win, -jnp.inf)
    s_ctx = jnp.einsum('bnqhrd,bchd->bnhrqc', qb, kc).astype(jnp.float32) * scale
    s_sink = jnp.broadcast_to(sink.astype(jnp.float32).reshape(N_KV, REP)[None, None, :, :, None, None],
                              (B, nb, N_KV, REP, WBLK, 1))
    p = jax.nn.softmax(jnp.concatenate([s_win, s_ctx, s_sink], axis=-1), axis=-1).astype(v.dtype)
    n_win = 3 * WBLK
    n_ctx = kc.shape[1]
    o = (jnp.einsum('bnhrqk,bnkhd->bnqhrd', p[..., :n_win], vw)
         + jnp.einsum('bnhrqc,bchd->bnqhrd', p[..., n_win:n_win + n_ctx], vc))
    return o.reshape(B, L, D_Q)


def context_attention(qc, kc, vc, sink):
    B, C, _, _ = qc.shape
    qr = qc.reshape(B, C, N_KV, REP, HEAD_DIM)
    s = jnp.einsum('bqhrd,bkhd->bhrqk', qr, kc).astype(jnp.float32) * HEAD_DIM ** -0.5
    s_sink = jnp.broadcast_to(sink.astype(jnp.float32).reshape(N_KV, REP)[None, :, :, None, None],
                              (B, N_KV, REP, C, 1))
    p = jax.nn.softmax(jnp.concatenate([s, s_sink], axis=-1), axis=-1).astype(vc.dtype)
    o = jnp.einsum('bhrqk,bkhd->bqhrd', p[..., :C], vc)
    return o.reshape(B, C, D_Q)


def merge_branches(a, o, ga, gb, w_a, w_b, w_o):
    y = jax.nn.sigmoid(ga) * (a @ w_a) + jax.nn.sigmoid(gb) * (o @ w_b)
    return y @ w_o


def moe_ffn(h, router_w, router_b, w_gate, w_up, w_down, ws_gate, ws_up, ws_down):
    shape = h.shape
    hf = h.reshape(-1, D_MODEL)
    T = hf.shape[0]
    scores = jax.nn.sigmoid(hf.astype(jnp.float32) @ router_w.astype(jnp.float32))
    _, idx = lax.top_k(scores + router_b.astype(jnp.float32), TOP_K)
    sel = jnp.take_along_axis(scores, idx, axis=1)
    gate_w = ROUTE_SCALE * sel / jnp.sum(sel, axis=-1, keepdims=True)

    n_assign = T * TOP_K
    n_blocks = -(-n_assign // MOE_BLK) + N_EXPERTS
    flat_e = idx.reshape(-1)
    flat_t = jnp.repeat(jnp.arange(T, dtype=jnp.int32), TOP_K)
    order = jnp.argsort(flat_e)
    sorted_e = flat_e[order]
    sorted_t = flat_t[order]
    w_sorted = gate_w.reshape(-1)[order]
    counts = jnp.bincount(flat_e, length=N_EXPERTS)
    group_start = jnp.cumsum(counts) - counts
    padded_counts = (counts + MOE_BLK - 1) // MOE_BLK * MOE_BLK
    padded_end = jnp.cumsum(padded_counts)
    padded_start = padded_end - padded_counts
    dest = padded_start[sorted_e] + (jnp.arange(n_assign) - group_start[sorted_e])
    buf_tok = jnp.full((n_blocks * MOE_BLK,), T, jnp.int32).at[dest].set(sorted_t)
    block_expert = jnp.minimum(
        jnp.searchsorted(padded_end, jnp.arange(n_blocks) * MOE_BLK, side='right'), N_EXPERTS - 1)
    h_pad = jnp.concatenate([hf, jnp.zeros((1, D_MODEL), hf.dtype)], axis=0)

    def expert_block(args):
        tok, e = args
        xb = jnp.take(h_pad, tok, axis=0)
        act = jax.nn.silu(xb @ w_gate[e]) * (xb @ w_up[e])
        return act @ w_down[e]

    rows = lax.map(expert_block, (buf_tok.reshape(n_blocks, MOE_BLK), block_expert))
    rows = rows.reshape(n_blocks * MOE_BLK, D_MODEL)
    contrib = rows[dest] * w_sorted[:, None].astype(hf.dtype)
    routed = jax.ops.segment_sum(contrib, sorted_t, num_segments=T)
    shared = (jax.nn.silu(hf @ ws_gate) * (hf @ ws_up)) @ ws_down
    return (routed + shared).reshape(shape)


def setup_inputs(seed: int = 0) -> dict:
    key = jax.random.key(seed)
    ks = jax.random.split(key, 32)
    f32 = jnp.float32
    D = D_MODEL

    def nrm(k, shape, scale):
        return jax.random.normal(k, shape, f32) * scale

    return {
        'x': nrm(ks[0], (BATCH, SEQ, D), 1.0),
        'c': nrm(ks[1], (BATCH, D), 1.0),
        'ctx': nrm(ks[2], (BATCH, CTX_LEN, D), 1.0),
        'c_ctx': nrm(ks[3], (D,), 1.0),
        'ada_w': nrm(ks[4], (DEPTH, D, 6 * D), 0.5 * D ** -0.5),
        'ada_b': nrm(ks[5], (DEPTH, 6 * D), 0.02),
        'norm1_g': 1.0 + nrm(ks[6], (DEPTH, D), 0.05),
        'norm2_g': 1.0 + nrm(ks[7], (DEPTH, D), 0.05),
        'w_in': nrm(ks[8], (DEPTH, D, D_IN), D ** -0.5),
        'gmlp_ln_g': 1.0 + nrm(ks[9], (DEPTH, D_A), 0.05),
        'gmlp_ln_b': nrm(ks[10], (DEPTH, D_A), 0.02),
        'gmlp_ws': nrm(ks[11], (DEPTH, G_A, CHUNK, CHUNK), CHUNK ** -0.5),
        'gmlp_bs': 1.0 + nrm(ks[12], (DEPTH, G_A, CHUNK), 0.1),
        'attn_sink': nrm(ks[13], (DEPTH, N_HEADS), 0.5),
        'w_branch_a': nrm(ks[14], (DEPTH, D_A, D), D_A ** -0.5),
        'w_branch_b': nrm(ks[15], (DEPTH, D_Q, D), D_Q ** -0.5),
        'w_out': nrm(ks[16], (DEPTH, D, D), D ** -0.5),
        'router_w': nrm(ks[17], (DEPTH, D, N_EXPERTS), D ** -0.5),
        'router_b': nrm(ks[18], (DEPTH, N_EXPERTS), 0.01),
        'exp_w_gate': nrm(ks[19], (DEPTH, N_EXPERTS, D, D_EXPERT), D ** -0.5),
        'exp_w_up': nrm(ks[20], (DEPTH, N_EXPERTS, D, D_EXPERT), D ** -0.5),
        'exp_w_down': nrm(ks[21], (DEPTH, N_EXPERTS, D_EXPERT, D), D_EXPERT ** -0.5),
        'sh_w_gate': nrm(ks[22], (DEPTH, D, D_SHARED), D ** -0.5),
        'sh_w_up': nrm(ks[23], (DEPTH, D, D_SHARED), D ** -0.5),
        'sh_w_down': nrm(ks[24], (DEPTH, D_SHARED, D), D_SHARED ** -0.5),
        'final_g': 1.0 + nrm(ks[25], (D,), 0.05),
    }


def reference(x, c, ctx, c_ctx, ada_w, ada_b, norm1_g, norm2_g, w_in, gmlp_ln_g, gmlp_ln_b,
              gmlp_ws, gmlp_bs, attn_sink, w_branch_a, w_branch_b, w_out, router_w, router_b,
              exp_w_gate, exp_w_up, exp_w_down, sh_w_gate, sh_w_up, sh_w_down, final_g):
    rope = axial_rope_tables(x.shape[1], x.dtype)
    xc = ctx
    for l in range(DEPTH):
        last = l == DEPTH - 1
        sh1, sc1, g1, sh2, sc2, g2 = ada_mod(c, ada_w[l], ada_b[l])
        sh1c, sc1c, g1c, sh2c, sc2c, g2c = ada_mod(c_ctx[None], ada_w[l], ada_b[l])

        h = modulate(rmsnorm(x, norm1_g[l]), sh1, sc1)
        hc = modulate(rmsnorm(xc, norm1_g[l]), sh1c, sc1c)
        u, v, q, k, val, ga, gb = split_proj(h @ w_in[l])
        q = apply_rope(q, rope)
        k = apply_rope(k, rope)
        if last:
            k_c, v_c = jnp.split(hc @ w_in[l][:, KV_LO:KV_HI], 2, axis=-1)
            k_c, v_c = heads(k_c, N_KV), heads(v_c, N_KV)
        else:
            u_c, vv_c, q_c, k_c, v_c, ga_c, gb_c = split_proj(hc @ w_in[l])
        a = chunk_gmlp(u, v, gmlp_ln_g[l], gmlp_ln_b[l], gmlp_ws[l], gmlp_bs[l])
        o = window_attention(q, k, val, k_c, v_c, attn_sink[l])
        x = x + g1 * merge_branches(a, o, ga, gb, w_branch_a[l], w_branch_b[l], w_out[l])
        if not last:
            a_c = chunk_gmlp(u_c, vv_c, gmlp_ln_g[l], gmlp_ln_b[l], gmlp_ws[l], gmlp_bs[l])
            o_c = context_attention(q_c, k_c, v_c, attn_sink[l])
            xc = xc + g1c * merge_branches(a_c, o_c, ga_c, gb_c, w_branch_a[l], w_branch_b[l], w_out[l])

        moe_w = (router_w[l], router_b[l], exp_w_gate[l], exp_w_up[l], exp_w_down[l],
                 sh_w_gate[l], sh_w_up[l], sh_w_down[l])
        x = x + g2 * moe_ffn(modulate(rmsnorm(x, norm2_g[l]), sh2, sc2), *moe_w)
        if not last:
            xc = xc + g2c * moe_ffn(modulate(rmsnorm(xc, norm2_g[l]), sh2c, sc2c), *moe_w)
    return rmsnorm(x, final_g)
```

```python
import functools

import jax
import jax.numpy as jnp
from jax import lax
from jax.experimental import pallas as pl
from jax.experimental.pallas import tpu as pltpu

F32 = jnp.float32
BF16 = jnp.bfloat16

D_MODEL = 1024
EPS = 1e-6
GRID_W = 64
D_A = D_MODEL // 2
G_A = 4
CHUNK = 128
N_HEADS = 8
N_KV = 2
REP = N_HEADS // N_KV
HEAD_DIM = 64
D_Q = N_HEADS * HEAD_DIM
D_KV = N_KV * HEAD_DIM
WBLK = 128
ROPE_THETA = 10000.0
N_EXPERTS = 256
TOP_K = 8
D_EXPERT = D_MODEL // 4
D_SHARED = D_MODEL // 4
ROUTE_SCALE = 2.5

C_U = 0
C_V = D_A
C_Q = 2 * D_A
C_K = C_Q + D_Q
C_VAL = C_K + D_KV
C_GA = C_VAL + D_KV
C_GB = C_GA + D_MODEL
D_IN = C_GB + D_MODEL

LANES = 128
ROPE_HALF = HEAD_DIM // 4
NEG_BIG = -1e30

TM_PROJ = 512
TM_MERGE = 256
TM_DMA = 256
EXP_BLK = 256
VMEM_LIMIT = 56 * 1024 * 1024


def _gelu(x):
    return 0.5 * x * (1.0 + jnp.tanh(0.7978845608028654 * (x + 0.044715 * x * x * x)))


def _silu(x):
    return x * jax.nn.sigmoid(x)


def _dot(a, b):
    return jnp.dot(a, b, preferred_element_type=F32)


def _rms_mod(x, g, shift, scale):
    ms = jnp.mean(x * x, axis=-1, keepdims=True)
    return (x * lax.rsqrt(ms + EPS)) * g * (1.0 + scale) + shift


def _params(*sem):
    return pltpu.CompilerParams(dimension_semantics=sem, vmem_limit_bytes=VMEM_LIMIT)


def _ada_kernel(c_ref, w_ref, b_ref, o_ref):
    c = c_ref[...]
    s = _silu(c).astype(BF16)
    o_ref[...] = _dot(s, w_ref[...].astype(BF16)) + b_ref[...]


def _ada(cond8, ada_w, ada_b):
    n = ada_w.shape[1]
    tn = 1536
    return pl.pallas_call(
        _ada_kernel,
        grid=(n // tn,),
        in_specs=[pl.BlockSpec((8, D_MODEL), lambda j: (0, 0)),
                  pl.BlockSpec((D_MODEL, tn), lambda j: (0, j)),
                  pl.BlockSpec((1, tn), lambda j: (0, j))],
        out_specs=pl.BlockSpec((8, tn), lambda j: (0, j)),
        out_shape=jax.ShapeDtypeStruct((8, n), F32),
        compiler_params=_params("arbitrary"),
        name="ada",
    )(cond8, ada_w, ada_b.reshape(1, n))


def _ctx_kernel(x_ref, mod_ref, g_ref, w_ref, k_ref, v_ref):
    mod = mod_ref[0]
    h = _rms_mod(x_ref[...], g_ref[...], mod[0:1], mod[1:2]).astype(BF16)
    z = _dot(h, w_ref[...])
    k_ref[...] = z[:, :D_KV].astype(BF16)
    v_ref[...] = z[:, D_KV:].astype(BF16)


def _ctx_kv(ctx2d, mods, g1, w_kv):
    n = ctx2d.shape[0]
    nb = mods.shape[0] - 1
    return pl.pallas_call(
        _ctx_kernel,
        grid=(1,),
        in_specs=[pl.BlockSpec((n, D_MODEL), lambda i: (0, 0)),
                  pl.BlockSpec((1, 6, D_MODEL), lambda i: (nb, 0, 0)),
                  pl.BlockSpec((1, D_MODEL), lambda i: (0, 0)),
                  pl.BlockSpec((D_MODEL, 2 * D_KV), lambda i: (0, 0))],
        out_specs=[pl.BlockSpec((n, D_KV), lambda i: (0, 0)),
                   pl.BlockSpec((n, D_KV), lambda i: (0, 0))],
        out_shape=[jax.ShapeDtypeStruct((n, D_KV), BF16)] * 2,
        compiler_params=_params("arbitrary"),
        name="ctx_kv",
    )(ctx2d, mods, g1, w_kv)


def _rope(t, cos, sin):
    lane = lax.broadcasted_iota(jnp.int32, (t.shape[0], LANES), 1)
    first = (lane & (2 * ROPE_HALF - 1)) < ROPE_HALF
    outs = []
    for j in range(t.shape[1] // LANES):
        tj = t[:, j * LANES:(j + 1) * LANES]
        up = pltpu.roll(tj, LANES - ROPE_HALF, 1)
        dn = pltpu.roll(tj, ROPE_HALF, 1)
        outs.append(tj * cos + jnp.where(first, up, dn) * sin)
    return outs


def _inproj_kernel(x_ref, mod_ref, g_ref, w_ref, lng_ref, lnb_ref, cos_ref, sin_ref,
                   gu_ref, vn_ref, q_ref, k_ref, v_ref, sga_ref, sgb_ref):
    mod = mod_ref[0]
    h = _rms_mod(x_ref[...], g_ref[...], mod[0:1], mod[1:2]).astype(BF16)

    def proj(lo, hi):
        return _dot(h, w_ref[:, lo:hi])

    gu_ref[...] = _gelu(proj(C_U, C_V)).astype(BF16)

    v = _gelu(proj(C_V, C_Q))
    mu = jnp.mean(v, axis=-1, keepdims=True)
    vc = v - mu
    var = jnp.mean(vc * vc, axis=-1, keepdims=True)
    vn_ref[...] = (vc * lax.rsqrt(var + EPS) * lng_ref[...] + lnb_ref[...]).astype(BF16)

    cos = cos_ref[...]
    sin = sin_ref[...]
    q = _rope(proj(C_Q, C_K) * (HEAD_DIM ** -0.5), cos, sin)
    for j, qj in enumerate(q):
        q_ref[:, j * LANES:(j + 1) * LANES] = qj.astype(BF16)
    k = _rope(proj(C_K, C_VAL), cos, sin)
    k_ref[...] = k[0].astype(BF16)
    v_ref[...] = proj(C_VAL, C_GA).astype(BF16)
    sga_ref[...] = jax.nn.sigmoid(proj(C_GA, C_GB)).astype(BF16)
    sgb_ref[...] = jax.nn.sigmoid(proj(C_GB, D_IN)).astype(BF16)


def _inproj(x2d, mods, g1, w_in, lng, lnb, cos, sin, seq):
    t = x2d.shape[0]
    tm = min(TM_PROJ, seq)
    spb = seq // tm
    row = lambda w: pl.BlockSpec((tm, w), lambda i: (i, 0))
    const = lambda s: pl.BlockSpec(s, lambda i: (0,) * len(s))
    return pl.pallas_call(
        _inproj_kernel,
        grid=(t // tm,),
        in_specs=[row(D_MODEL),
                  pl.BlockSpec((1, 6, D_MODEL), lambda i: (i // spb, 0, 0)),
                  const((1, D_MODEL)),
                  const((D_MODEL, D_IN)),
                  const((1, D_A)), const((1, D_A)),
                  pl.BlockSpec((tm, LANES), lambda i: (i % spb, 0)),
                  pl.BlockSpec((tm, LANES), lambda i: (i % spb, 0))],
        out_specs=[row(D_A), row(D_A), row(D_Q), row(D_KV), row(D_KV), row(D_MODEL), row(D_MODEL)],
        out_shape=[jax.ShapeDtypeStruct((t, w), BF16)
                   for w in (D_A, D_A, D_Q, D_KV, D_KV, D_MODEL, D_MODEL)],
        compiler_params=_params("arbitrary"),
        name="inproj",
    )(x2d, mods, g1, w_in, lng, lnb, cos, sin)


def _mix_kernel(sink_ref, gu_ref, vn_ref, q_ref, kp_ref, kc_ref, kn_ref, vp_ref, vc_ref, vx_ref,
                kctx_ref, vctx_ref, ws_ref, bs_ref, a_ref, o_ref):
    n = pl.program_id(1)
    nblk = pl.num_programs(1)

    for g in range(G_A):
        sl = slice(g * CHUNK, (g + 1) * CHUNK)
        s = _dot(ws_ref[g], vn_ref[:, sl]) + bs_ref[:, sl]
        a_ref[:, sl] = (gu_ref[:, sl].astype(F32) * s).astype(BF16)

    kcat = jnp.concatenate([kp_ref[...], kc_ref[...], kn_ref[...], kctx_ref[...]], axis=0)
    vcat = jnp.concatenate([vp_ref[...], vc_ref[...], vx_ref[...], vctx_ref[...]], axis=0)
    nwin = 3 * WBLK
    nkeys = kcat.shape[0]
    rows = REP * WBLK
    qi = lax.broadcasted_iota(jnp.int32, (rows, nkeys), 0) & (WBLK - 1)
    kj = lax.broadcasted_iota(jnp.int32, (rows, nkeys), 1)
    lo = jnp.where(n == 0, WBLK, 0)
    hi = jnp.where(n == nblk - 1, 2 * WBLK, nwin)
    valid = (kj >= nwin) | ((kj >= qi) & (kj <= qi + 2 * WBLK) & (kj >= lo) & (kj < hi))
    rgrp = lax.broadcasted_iota(jnp.int32, (rows, 1), 0) // WBLK

    for kvh in range(N_KV):
        ksl = kcat[:, kvh * HEAD_DIM:(kvh + 1) * HEAD_DIM]
        vsl = vcat[:, kvh * HEAD_DIM:(kvh + 1) * HEAD_DIM]
        qs = jnp.concatenate(
            [q_ref[:, (kvh * REP + r) * HEAD_DIM:(kvh * REP + r + 1) * HEAD_DIM] for r in range(REP)],
            axis=0)
        sink = jnp.zeros((rows, 1), F32)
        for r in range(REP):
            sink = jnp.where(rgrp == r, sink_ref[kvh * REP + r], sink)
        s = lax.dot_general(qs, ksl, (((1,), (1,)), ((), ())), preferred_element_type=F32)
        s = jnp.where(valid, s, NEG_BIG)
        m = jnp.maximum(jnp.max(s, axis=-1, keepdims=True), sink)
        p = jnp.exp(s - m)
        den = jnp.sum(p, axis=-1, keepdims=True) + jnp.exp(sink - m)
        o = _dot(p.astype(BF16), vsl) / den
        for r in range(0, REP, 2):
            pair = jnp.concatenate([o[r * WBLK:(r + 1) * WBLK], o[(r + 1) * WBLK:(r + 2) * WBLK]], axis=1)
            c0 = (kvh * REP + r) * HEAD_DIM
            o_ref[:, c0:c0 + 2 * HEAD_DIM] = pair.astype(BF16)


def _mix(sink, gu, vn, q, k, v, kctx, vctx, ws, bs_full, batch, seq):
    t = gu.shape[0]
    nblk = seq // WBLK
    nctx = kctx.shape[0] // batch
    cur = lambda w: pl.BlockSpec((WBLK, w), lambda b, n: (b * nblk + n, 0))
    prev = pl.BlockSpec((WBLK, D_KV), lambda b, n: (b * nblk + jnp.maximum(n - 1, 0), 0))
    nxt = pl.BlockSpec((WBLK, D_KV), lambda b, n: (b * nblk + jnp.minimum(n + 1, nblk - 1), 0))
    same = pl.BlockSpec((WBLK, D_KV), lambda b, n: (b * nblk + n, 0))
    cblk = pl.BlockSpec((nctx, D_KV), lambda b, n: (b, 0))
    return pl.pallas_call(
        _mix_kernel,
        grid=(batch, nblk),
        in_specs=[pl.BlockSpec(memory_space=pltpu.SMEM),
                  cur(D_A), cur(D_A), cur(D_Q),
                  prev, same, nxt, prev, same, nxt,
                  cblk, cblk,
                  pl.BlockSpec((G_A, CHUNK, CHUNK), lambda b, n: (0, 0, 0)),
                  pl.BlockSpec((CHUNK, D_A), lambda b, n: (0, 0))],
        out_specs=[cur(D_A), cur(D_Q)],
        out_shape=[jax.ShapeDtypeStruct((t, D_A), BF16), jax.ShapeDtypeStruct((t, D_Q), BF16)],
        compiler_params=_params("arbitrary", "arbitrary"),
        name="mix",
    )(sink, gu, vn, q, k, k, k, v, v, v, kctx, vctx, ws, bs_full)


def _merge_kernel(a_ref, o_ref, sga_ref, sgb_ref, x_ref, mod_ref, g2_ref, wa_ref, wb_ref, wo_ref,
                  rwh_ref, rwl_ref, rb_ref,
                  x1_ref, h2_ref, idx_ref, gate_ref, rank_ref, cnt_ref, base_ref):
    i = pl.program_id(0)

    @pl.when(i == 0)
    def _():
        base_ref[...] = jnp.zeros_like(base_ref)

    mod = mod_ref[0]
    ya = _dot(a_ref[...], wa_ref[...])
    yb = _dot(o_ref[...], wb_ref[...])
    y = sga_ref[...].astype(F32) * ya + sgb_ref[...].astype(F32) * yb
    x1 = x_ref[...] + mod[2:3] * _dot(y.astype(BF16), wo_ref[...])
    x1_ref[...] = x1
    h2 = _rms_mod(x1, g2_ref[...], mod[3:4], mod[4:5])
    h2_ref[...] = h2

    hh = h2.astype(BF16)
    hl = (h2 - hh.astype(F32)).astype(BF16)
    dn = (((1,), (1,)), ((), ()))
    logits = (lax.dot_general(rwh_ref[...], hh, dn, preferred_element_type=F32)
              + lax.dot_general(rwl_ref[...], hh, dn, preferred_element_type=F32)
              + lax.dot_general(rwh_ref[...], hl, dn, preferred_element_type=F32))
    scores = jax.nn.sigmoid(logits)
    tm = scores.shape[1]
    eio = lax.broadcasted_iota(jnp.int32, scores.shape, 0).astype(F32)
    work = scores + rb_ref[...]
    picked = jnp.zeros(scores.shape, F32)
    idxs, vals = [], []
    for _ in range(TOP_K):
        m = jnp.max(work, axis=0, keepdims=True)
        ik = jnp.min(jnp.where(work == m, eio, float(N_EXPERTS)), axis=0, keepdims=True)
        oh = eio == ik
        vals.append(jnp.sum(jnp.where(oh, scores, 0.0), axis=0, keepdims=True))
        idxs.append(ik)
        work = jnp.where(oh, -jnp.inf, work)
        picked = picked + oh.astype(F32)
    total = vals[0]
    for vk in vals[1:]:
        total = total + vk

    tr = lax.broadcasted_iota(jnp.int32, (tm, tm), 0)
    tc = lax.broadcasted_iota(jnp.int32, (tm, tm), 1)
    before = (tr < tc).astype(BF16)
    base = base_ref[...]
    rank_e = _dot(picked.astype(BF16), before) + base
    ranks = [jnp.sum(jnp.where(eio == ik, rank_e, 0.0), axis=0, keepdims=True) for ik in idxs]
    base = base + jnp.sum(picked, axis=1, keepdims=True)
    base_ref[...] = base
    cnt_ref[...] = base

    idx_ref[...] = jnp.concatenate(idxs, axis=0).astype(jnp.int32)
    rank_ref[...] = jnp.concatenate(ranks, axis=0).astype(jnp.int32)
    gate_ref[...] = jnp.concatenate(vals, axis=0) * (ROUTE_SCALE / total)


def _merge(a, o, sga, sgb, x2d, mods, g2, wa, wb, wo, rwh, rwl, rb, seq):
    t = x2d.shape[0]
    tm = min(TM_MERGE, seq)
    spb = seq // tm
    row = lambda w: pl.BlockSpec((tm, w), lambda i: (i, 0))
    col = lambda: pl.BlockSpec((TOP_K, tm), lambda i: (0, i))
    const = lambda s: pl.BlockSpec(s, lambda i: (0,) * len(s))
    return pl.pallas_call(
        _merge_kernel,
        grid=(t // tm,),
        in_specs=[row(D_A), row(D_Q), row(D_MODEL), row(D_MODEL), row(D_MODEL),
                  pl.BlockSpec((1, 6, D_MODEL), lambda i: (i // spb, 0, 0)),
                  const((1, D_MODEL)),
                  const((D_A, D_MODEL)), const((D_Q, D_MODEL)), const((D_MODEL, D_MODEL)),
                  const((N_EXPERTS, D_MODEL)), const((N_EXPERTS, D_MODEL)), const((N_EXPERTS, 1))],
        out_specs=[row(D_MODEL), row(D_MODEL), col(), col(), col(), const((N_EXPERTS, 1))],
        out_shape=[jax.ShapeDtypeStruct((t, D_MODEL), F32),
                   jax.ShapeDtypeStruct((t, D_MODEL), F32),
                   jax.ShapeDtypeStruct((TOP_K, t), jnp.int32),
                   jax.ShapeDtypeStruct((TOP_K, t), F32),
                   jax.ShapeDtypeStruct((TOP_K, t), jnp.int32),
                   jax.ShapeDtypeStruct((N_EXPERTS, 1), F32)],
        scratch_shapes=[pltpu.VMEM((N_EXPERTS, 1), F32)],
        compiler_params=_params("arbitrary"),
        name="merge",
    )(a, o, sga, sgb, x2d, mods, g2, wa, wb, wo, rwh, rwl, rb)


def _row_copy(src, si, dst, di, sem):
    return pltpu.make_async_copy(src.at[pl.ds(si, 1), :], dst.at[pl.ds(di, 1), :], sem)


def _rows_wait(ref, nrows, sem):
    pltpu.make_async_copy(ref.at[pl.ds(0, nrows), :], ref.at[pl.ds(0, nrows), :], sem).wait()


def _dispatch_kernel(dest_ref, h_ref, zeros_ref, xs_ref, sem):
    del zeros_ref
    i = pl.program_id(0)
    tm = dest_ref.shape[1]

    def body(j, carry):
        for k in range(TOP_K):
            _row_copy(h_ref, i * tm + j, xs_ref, dest_ref[k, j], sem).start()
        return carry

    lax.fori_loop(0, tm, body, 0)
    _rows_wait(xs_ref, tm * TOP_K, sem)


def _dispatch(dest, h2, nrows):
    t = h2.shape[0]
    tm = min(TM_DMA, t)
    return pl.pallas_call(
        _dispatch_kernel,
        grid=(t // tm,),
        in_specs=[pl.BlockSpec((TOP_K, tm), lambda i: (0, i), memory_space=pltpu.SMEM),
                  pl.BlockSpec(memory_space=pl.ANY),
                  pl.BlockSpec(memory_space=pl.ANY)],
        out_specs=pl.BlockSpec(memory_space=pl.ANY),
        out_shape=jax.ShapeDtypeStruct((nrows, D_MODEL), F32),
        scratch_shapes=[pltpu.SemaphoreType.DMA(())],
        input_output_aliases={2: 0},
        compiler_params=_params("arbitrary"),
        name="dispatch",
    )(dest, h2, jnp.zeros((nrows, D_MODEL), F32))


def _expert_kernel(bexp_ref, bsrc_ref, nval_ref, xs_ref, wg_ref, wu_ref, wd_ref, ys_ref):
    b = pl.program_id(0)

    @pl.when(b < nval_ref[0])
    def _():
        x = xs_ref[...].astype(BF16)
        g = _dot(x, wg_ref[0].astype(BF16))
        u = _dot(x, wu_ref[0].astype(BF16))
        act = (_silu(g) * u).astype(BF16)
        ys_ref[...] = _dot(act, wd_ref[0].astype(BF16))

    @pl.when(b >= nval_ref[0])
    def _():
        ys_ref[...] = jnp.zeros_like(ys_ref)


def _experts(bexp, bsrc, nval, xs, wg, wu, wd):
    nrows = xs.shape[0]
    nb = nrows // EXP_BLK
    grid_spec = pltpu.PrefetchScalarGridSpec(
        num_scalar_prefetch=3,
        grid=(nb,),
        in_specs=[pl.BlockSpec((EXP_BLK, D_MODEL), lambda b, be, bs, nv: (bs[b], 0)),
                  pl.BlockSpec((1, D_MODEL, D_EXPERT), lambda b, be, bs, nv: (be[b], 0, 0)),
                  pl.BlockSpec((1, D_MODEL, D_EXPERT), lambda b, be, bs, nv: (be[b], 0, 0)),
                  pl.BlockSpec((1, D_EXPERT, D_MODEL), lambda b, be, bs, nv: (be[b], 0, 0))],
        out_specs=pl.BlockSpec((EXP_BLK, D_MODEL), lambda b, be, bs, nv: (b, 0)),
    )
    return pl.pallas_call(
        _expert_kernel,
        grid_spec=grid_spec,
        out_shape=jax.ShapeDtypeStruct((nrows, D_MODEL), F32),
        compiler_params=_params("arbitrary"),
        name="experts",
    )(bexp, bsrc, nval, xs, wg, wu, wd)


def _combine_kernel(dest_ref, ys_ref, gate_ref, x1_ref, h2_ref, mod_ref, sg_ref, su_ref, sd_ref, fg_ref,
                    out_ref, buf_ref, sem):
    tm = dest_ref.shape[1]

    def body(j, carry):
        for k in range(TOP_K):
            _row_copy(ys_ref, dest_ref[k, j], buf_ref.at[k], j, sem).start()
        return carry

    lax.fori_loop(0, tm, body, 0)

    mod = mod_ref[0]
    h = h2_ref[...].astype(BF16)
    act = (_silu(_dot(h, sg_ref[...])) * _dot(h, su_ref[...])).astype(BF16)
    moe = _dot(act, sd_ref[...])

    g = gate_ref[...]
    g0 = g.astype(BF16)
    r1 = g - g0.astype(F32)
    g1 = r1.astype(BF16)
    g2 = (r1 - g1.astype(F32)).astype(BF16)
    eye = (lax.broadcasted_iota(jnp.int32, (tm, tm), 0)
           == lax.broadcasted_iota(jnp.int32, (tm, tm), 1)).astype(BF16)
    dn = (((1,), (1,)), ((), ()))
    gcol = (lax.dot_general(eye, g0, dn, preferred_element_type=F32)
            + lax.dot_general(eye, g1, dn, preferred_element_type=F32)
            + lax.dot_general(eye, g2, dn, preferred_element_type=F32))

    for k in range(TOP_K):
        _rows_wait(buf_ref.at[k], tm, sem)
    for k in range(TOP_K):
        moe = moe + gcol[:, k:k + 1] * buf_ref[k]

    x2 = x1_ref[...] + mod[5:6] * moe
    ms = jnp.mean(x2 * x2, axis=-1, keepdims=True)
    out_ref[...] = x2 * lax.rsqrt(ms + EPS) * fg_ref[...]


def _combine(dest, ys, gate, x1, h2, mods, sg, su, sd, fg, seq):
    t = x1.shape[0]
    tm = min(TM_DMA, seq)
    spb = seq // tm
    row = lambda: pl.BlockSpec((tm, D_MODEL), lambda i: (i, 0))
    const = lambda s: pl.BlockSpec(s, lambda i: (0,) * len(s))
    return pl.pallas_call(
        _combine_kernel,
        grid=(t // tm,),
        in_specs=[pl.BlockSpec((TOP_K, tm), lambda i: (0, i), memory_space=pltpu.SMEM),
                  pl.BlockSpec(memory_space=pl.ANY),
                  pl.BlockSpec((TOP_K, tm), lambda i: (0, i)),
                  row(), row(),
                  pl.BlockSpec((1, 6, D_MODEL), lambda i: (i // spb, 0, 0)),
                  const((D_MODEL, D_SHARED)), const((D_MODEL, D_SHARED)), const((D_SHARED, D_MODEL)),
                  const((1, D_MODEL))],
        out_specs=row(),
        out_shape=jax.ShapeDtypeStruct((t, D_MODEL), F32),
        scratch_shapes=[pltpu.VMEM((TOP_K, tm, D_MODEL), F32), pltpu.SemaphoreType.DMA(())],
        compiler_params=_params("arbitrary"),
        name="combine",
    )(dest, ys, gate, x1, h2, mods, sg, su, sd, fg)


def _rope_tables(seq):
    pos = jnp.arange(seq)
    n_freq = HEAD_DIM // 4
    inv = ROPE_THETA ** (-jnp.arange(n_freq, dtype=F32) / n_freq)
    ang_r = (pos // GRID_W)[:, None].astype(F32) * inv
    ang_c = (pos % GRID_W)[:, None].astype(F32) * inv
    cr, sr, cc, sc = jnp.cos(ang_r), jnp.sin(ang_r), jnp.cos(ang_c), jnp.sin(ang_c)
    cos = jnp.concatenate([cr, cr, cc, cc], axis=1)
    sin = jnp.concatenate([-sr, sr, -sc, sc], axis=1)
    reps = LANES // HEAD_DIM
    return jnp.tile(cos, (1, reps)), jnp.tile(sin, (1, reps))


def _layer(x2d, ctx2d, mods, batch, seq, norm1_g, norm2_g, w_in, ln_g, ln_b, gmlp_ws, gmlp_bs, sink,
           w_a, w_b, w_o, router_w, router_b, e_gate, e_up, e_down, s_gate, s_up, s_down, final_g):
    t = x2d.shape[0]
    g1 = norm1_g.reshape(1, D_MODEL)
    w_in_b = w_in.astype(BF16)
    cos, sin = _rope_tables(seq)

    kctx, vctx = _ctx_kv(ctx2d, mods, g1, w_in_b[:, C_K:C_GA])
    gu, vn, q, k, v, sga, sgb = _inproj(x2d, mods, g1, w_in_b, ln_g.reshape(1, D_A), ln_b.reshape(1, D_A),
                                        cos, sin, seq)

    bs_full = jnp.repeat(gmlp_bs.T, D_A // G_A, axis=1)
    a, o = _mix(sink, gu, vn, q, k, v, kctx, vctx, gmlp_ws.astype(BF16), bs_full, batch, seq)

    rwt = router_w.T
    rwh = rwt.astype(BF16)
    rwl = (rwt - rwh.astype(F32)).astype(BF16)
    x1, h2, idx, gate, rank, counts = _merge(
        a, o, sga, sgb, x2d, mods, norm2_g.reshape(1, D_MODEL),
        w_a.astype(BF16), w_b.astype(BF16), w_o.astype(BF16), rwh, rwl, router_b.reshape(N_EXPERTS, 1), seq)

    cnt = counts[:, 0].astype(jnp.int32)
    pcnt = (cnt + EXP_BLK - 1) // EXP_BLK * EXP_BLK
    pend = jnp.cumsum(pcnt)
    pstart = pend - pcnt
    dest = jnp.take(pstart, idx, axis=0) + rank
    nblocks = (t * TOP_K) // EXP_BLK + N_EXPERTS
    nval = (pend[-1] // EXP_BLK).astype(jnp.int32)
    bsrc = jnp.minimum(jnp.arange(nblocks, dtype=jnp.int32), nval - 1)
    bexp = jnp.sum((pend[None, :] <= (bsrc * EXP_BLK)[:, None]).astype(jnp.int32), axis=1)
    bexp = jnp.minimum(bexp, N_EXPERTS - 1)

    xs = _dispatch(dest, h2, nblocks * EXP_BLK)
    ys = _experts(bexp, bsrc, nval.reshape(1), xs, e_gate, e_up, e_down)
    return _combine(dest, ys, gate, x1, h2, mods, s_gate.astype(BF16), s_up.astype(BF16),
                    s_down.astype(BF16), final_g.reshape(1, D_MODEL), seq)


def kernel(x, c, ctx, c_ctx, ada_w, ada_b, norm1_g, norm2_g, w_in, gmlp_ln_g, gmlp_ln_b, gmlp_ws, gmlp_bs,
           attn_sink, w_branch_a, w_branch_b, w_out, router_w, router_b, exp_w_gate, exp_w_up, exp_w_down,
           sh_w_gate, sh_w_up, sh_w_down, final_g):
    batch, seq, _ = x.shape
    depth = ada_w.shape[0]
    assert depth == 1, "the context stream is only carried as keys/values of a single layer"
    assert batch + 1 <= 8 and seq % WBLK == 0
    cond = jnp.concatenate([c, c_ctx[None], jnp.zeros((8 - batch - 1, D_MODEL), F32)], axis=0)
    mods = _ada(cond, ada_w[0], ada_b[0])[:batch + 1].reshape(batch + 1, 6, D_MODEL)
    out = _layer(x.reshape(batch * seq, D_MODEL), ctx.reshape(-1, D_MODEL), mods, batch, seq,
                 norm1_g[0], norm2_g[0], w_in[0], gmlp_ln_g[0], gmlp_ln_b[0], gmlp_ws[0], gmlp_bs[0],
                 attn_sink[0], w_branch_a[0], w_branch_b[0], w_out[0], router_w[0], router_b[0],
                 exp_w_gate[0], exp_w_up[0], exp_w_down[0], sh_w_gate[0], sh_w_up[0], sh_w_down[0], final_g)
    return out.reshape(batch, seq, D_MODEL)
```

```python
import functools

import jax
import jax.numpy as jnp
from jax import lax
from jax.experimental import pallas as pl
from jax.experimental.pallas import tpu as pltpu

F32 = jnp.float32
BF16 = jnp.bfloat16

D_MODEL = 1024
EPS = 1e-6
GRID_W = 64
D_A = D_MODEL // 2
G_A = 4
CHUNK = 128
N_HEADS = 8
N_KV = 2
REP = N_HEADS // N_KV
HEAD_DIM = 64
D_Q = N_HEADS * HEAD_DIM
D_KV = N_KV * HEAD_DIM
WBLK = 128
ROPE_THETA = 10000.0
N_EXPERTS = 256
TOP_K = 8
D_EXPERT = D_MODEL // 4
D_SHARED = D_MODEL // 4
ROUTE_SCALE = 2.5

C_U = 0
C_V = D_A
C_Q = 2 * D_A
C_K = C_Q + D_Q
C_VAL = C_K + D_KV
C_GA = C_VAL + D_KV
C_GB = C_GA + D_MODEL
D_IN = C_GB + D_MODEL

LANES = 128
ROPE_HALF = HEAD_DIM // 4
NEG_BIG = -1e30

TM_PROJ = 512
TM_MERGE = 256
TM_DMA = 256
EXP_BLK = 256
SUBLANES = 8
ROW_TILES = D_MODEL // LANES
assert ROW_TILES == SUBLANES
VMEM_LIMIT = 56 * 1024 * 1024


def _gelu(x):
    return 0.5 * x * (1.0 + jnp.tanh(0.7978845608028654 * (x + 0.044715 * x * x * x)))


def _silu(x):
    return x * jax.nn.sigmoid(x)


def _dot(a, b):
    return jnp.dot(a, b, preferred_element_type=F32)


def _rms_mod(x, g, shift, scale):
    ms = jnp.mean(x * x, axis=-1, keepdims=True)
    return (x * lax.rsqrt(ms + EPS)) * g * (1.0 + scale) + shift


def _params(*sem):
    return pltpu.CompilerParams(dimension_semantics=sem, vmem_limit_bytes=VMEM_LIMIT)


def _ada_kernel(c_ref, w_ref, b_ref, o_ref):
    c = c_ref[...]
    s = _silu(c).astype(BF16)
    o_ref[...] = _dot(s, w_ref[...].astype(BF16)) + b_ref[...]


def _ada(cond8, ada_w, ada_b):
    n = ada_w.shape[1]
    tn = 1536
    return pl.pallas_call(
        _ada_kernel,
        grid=(n // tn,),
        in_specs=[pl.BlockSpec((8, D_MODEL), lambda j: (0, 0)),
                  pl.BlockSpec((D_MODEL, tn), lambda j: (0, j)),
                  pl.BlockSpec((1, tn), lambda j: (0, j))],
        out_specs=pl.BlockSpec((8, tn), lambda j: (0, j)),
        out_shape=jax.ShapeDtypeStruct((8, n), F32),
        compiler_params=_params("arbitrary"),
        name="ada",
    )(cond8, ada_w, ada_b.reshape(1, n))


def _ctx_kernel(x_ref, mod_ref, g_ref, w_ref, k_ref, v_ref):
    mod = mod_ref[0]
    h = _rms_mod(x_ref[...], g_ref[...], mod[0:1], mod[1:2]).astype(BF16)
    z = _dot(h, w_ref[...])
    k_ref[...] = z[:, :D_KV].astype(BF16)
    v_ref[...] = z[:, D_KV:].astype(BF16)


def _ctx_kv(ctx2d, mods, g1, w_kv):
    n = ctx2d.shape[0]
    nb = mods.shape[0] - 1
    return pl.pallas_call(
        _ctx_kernel,
        grid=(1,),
        in_specs=[pl.BlockSpec((n, D_MODEL), lambda i: (0, 0)),
                  pl.BlockSpec((1, 6, D_MODEL), lambda i: (nb, 0, 0)),
                  pl.BlockSpec((1, D_MODEL), lambda i: (0, 0)),
                  pl.BlockSpec((D_MODEL, 2 * D_KV), lambda i: (0, 0))],
        out_specs=[pl.BlockSpec((n, D_KV), lambda i: (0, 0)),
                   pl.BlockSpec((n, D_KV), lambda i: (0, 0))],
        out_shape=[jax.ShapeDtypeStruct((n, D_KV), BF16)] * 2,
        compiler_params=_params("arbitrary"),
        name="ctx_kv",
    )(ctx2d, mods, g1, w_kv)


def _rope(t, cos, sin):
    lane = lax.broadcasted_iota(jnp.int32, (t.shape[0], LANES), 1)
    first = (lane & (2 * ROPE_HALF - 1)) < ROPE_HALF
    outs = []
    for j in range(t.shape[1] // LANES):
        tj = t[:, j * LANES:(j + 1) * LANES]
        up = pltpu.roll(tj, LANES - ROPE_HALF, 1)
        dn = pltpu.roll(tj, ROPE_HALF, 1)
        outs.append(tj * cos + jnp.where(first, up, dn) * sin)
    return outs


def _inproj_kernel(x_ref, mod_ref, g_ref, w_ref, lng_ref, lnb_ref, cos_ref, sin_ref,
                   gu_ref, vn_ref, q_ref, k_ref, v_ref, sga_ref, sgb_ref):
    mod = mod_ref[0]
    h = _rms_mod(x_ref[...], g_ref[...], mod[0:1], mod[1:2]).astype(BF16)

    def proj(lo, hi):
        return _dot(h, w_ref[:, lo:hi])

    gu_ref[...] = _gelu(proj(C_U, C_V)).astype(BF16)

    v = _gelu(proj(C_V, C_Q))
    mu = jnp.mean(v, axis=-1, keepdims=True)
    vc = v - mu
    var = jnp.mean(vc * vc, axis=-1, keepdims=True)
    vn_ref[...] = (vc * lax.rsqrt(var + EPS) * lng_ref[...] + lnb_ref[...]).astype(BF16)

    cos = cos_ref[...]
    sin = sin_ref[...]
    q = _rope(proj(C_Q, C_K) * (HEAD_DIM ** -0.5), cos, sin)
    for j, qj in enumerate(q):
        q_ref[:, j * LANES:(j + 1) * LANES] = qj.astype(BF16)
    k = _rope(proj(C_K, C_VAL), cos, sin)
    k_ref[...] = k[0].astype(BF16)
    v_ref[...] = proj(C_VAL, C_GA).astype(BF16)
    sga_ref[...] = jax.nn.sigmoid(proj(C_GA, C_GB)).astype(BF16)
    sgb_ref[...] = jax.nn.sigmoid(proj(C_GB, D_IN)).astype(BF16)


def _inproj(x2d, mods, g1, w_in, lng, lnb, cos, sin, seq):
    t = x2d.shape[0]
    tm = min(TM_PROJ, seq)
    spb = seq // tm
    row = lambda w: pl.BlockSpec((tm, w), lambda i: (i, 0))
    const = lambda s: pl.BlockSpec(s, lambda i: (0,) * len(s))
    return pl.pallas_call(
        _inproj_kernel,
        grid=(t // tm,),
        in_specs=[row(D_MODEL),
                  pl.BlockSpec((1, 6, D_MODEL), lambda i: (i // spb, 0, 0)),
                  const((1, D_MODEL)),
                  const((D_MODEL, D_IN)),
                  const((1, D_A)), const((1, D_A)),
                  pl.BlockSpec((tm, LANES), lambda i: (i % spb, 0)),
                  pl.BlockSpec((tm, LANES), lambda i: (i % spb, 0))],
        out_specs=[row(D_A), row(D_A), row(D_Q), row(D_KV), row(D_KV), row(D_MODEL), row(D_MODEL)],
        out_shape=[jax.ShapeDtypeStruct((t, w), BF16)
                   for w in (D_A, D_A, D_Q, D_KV, D_KV, D_MODEL, D_MODEL)],
        compiler_params=_params("arbitrary"),
        name="inproj",
    )(x2d, mods, g1, w_in, lng, lnb, cos, sin)


def _mix_kernel(sink_ref, gu_ref, vn_ref, q_ref, kp_ref, kc_ref, kn_ref, vp_ref, vc_ref, vx_ref,
                kctx_ref, vctx_ref, ws_ref, bs_ref, a_ref, o_ref):
    n = pl.program_id(1)
    nblk = pl.num_programs(1)

    for g in range(G_A):
        sl = slice(g * CHUNK, (g + 1) * CHUNK)
        s = _dot(ws_ref[g], vn_ref[:, sl]) + bs_ref[:, sl]
        a_ref[:, sl] = (gu_ref[:, sl].astype(F32) * s).astype(BF16)

    kcat = jnp.concatenate([kp_ref[...], kc_ref[...], kn_ref[...], kctx_ref[...]], axis=0)
    vcat = jnp.concatenate([vp_ref[...], vc_ref[...], vx_ref[...], vctx_ref[...]], axis=0)
    nwin = 3 * WBLK
    nkeys = kcat.shape[0]
    rows = REP * WBLK
    qi = lax.broadcasted_iota(jnp.int32, (rows, nkeys), 0) & (WBLK - 1)
    kj = lax.broadcasted_iota(jnp.int32, (rows, nkeys), 1)
    lo = jnp.where(n == 0, WBLK, 0)
    hi = jnp.where(n == nblk - 1, 2 * WBLK, nwin)
    valid = (kj >= nwin) | ((kj >= qi) & (kj <= qi + 2 * WBLK) & (kj >= lo) & (kj < hi))
    rgrp = lax.broadcasted_iota(jnp.int32, (rows, 1), 0) // WBLK

    for kvh in range(N_KV):
        ksl = kcat[:, kvh * HEAD_DIM:(kvh + 1) * HEAD_DIM]
        vsl = vcat[:, kvh * HEAD_DIM:(kvh + 1) * HEAD_DIM]
        qs = jnp.concatenate(
            [q_ref[:, (kvh * REP + r) * HEAD_DIM:(kvh * REP + r + 1) * HEAD_DIM] for r in range(REP)],
            axis=0)
        sink = jnp.zeros((rows, 1), F32)
        for r in range(REP):
            sink = jnp.where(rgrp == r, sink_ref[kvh * REP + r], sink)
        s = lax.dot_general(qs, ksl, (((1,), (1,)), ((), ())), preferred_element_type=F32)
        s = jnp.where(valid, s, NEG_BIG)
        m = jnp.maximum(jnp.max(s, axis=-1, keepdims=True), sink)
        p = jnp.exp(s - m)
        den = jnp.sum(p, axis=-1, keepdims=True) + jnp.exp(sink - m)
        o = _dot(p.astype(BF16), vsl) / den
        for r in range(0, REP, 2):
            pair = jnp.concatenate([o[r * WBLK:(r + 1) * WBLK], o[(r + 1) * WBLK:(r + 2) * WBLK]], axis=1)
            c0 = (kvh * REP + r) * HEAD_DIM
            o_ref[:, c0:c0 + 2 * HEAD_DIM] = pair.astype(BF16)


def _mix(sink, gu, vn, q, k, v, kctx, vctx, ws, bs_full, batch, seq):
    t = gu.shape[0]
    nblk = seq // WBLK
    nctx = kctx.shape[0] // batch
    cur = lambda w: pl.BlockSpec((WBLK, w), lambda b, n: (b * nblk + n, 0))
    prev = pl.BlockSpec((WBLK, D_KV), lambda b, n: (b * nblk + jnp.maximum(n - 1, 0), 0))
    nxt = pl.BlockSpec((WBLK, D_KV), lambda b, n: (b * nblk + jnp.minimum(n + 1, nblk - 1), 0))
    same = pl.BlockSpec((WBLK, D_KV), lambda b, n: (b * nblk + n, 0))
    cblk = pl.BlockSpec((nctx, D_KV), lambda b, n: (b, 0))
    return pl.pallas_call(
        _mix_kernel,
        grid=(batch, nblk),
        in_specs=[pl.BlockSpec(memory_space=pltpu.SMEM),
                  cur(D_A), cur(D_A), cur(D_Q),
                  prev, same, nxt, prev, same, nxt,
                  cblk, cblk,
                  pl.BlockSpec((G_A, CHUNK, CHUNK), lambda b, n: (0, 0, 0)),
                  pl.BlockSpec((CHUNK, D_A), lambda b, n: (0, 0))],
        out_specs=[cur(D_A), cur(D_Q)],
        out_shape=[jax.ShapeDtypeStruct((t, D_A), BF16), jax.ShapeDtypeStruct((t, D_Q), BF16)],
        compiler_params=_params("arbitrary", "arbitrary"),
        name="mix",
    )(sink, gu, vn, q, k, k, k, v, v, v, kctx, vctx, ws, bs_full)


def _store_row_tiles(ref, val):
    rows = val.shape[0]
    for s in range(ROW_TILES):
        ref[pl.ds(s, rows, stride=ROW_TILES), :] = val[:, s * LANES:(s + 1) * LANES]


def _load_row_tiles(ref, s):
    return ref[pl.ds(s, ref.shape[0] // ROW_TILES, stride=ROW_TILES), :]


def _merge_kernel(a_ref, o_ref, sga_ref, sgb_ref, x_ref, mod_ref, g2_ref, wa_ref, wb_ref, wo_ref,
                  rwh_ref, rwl_ref, rb_ref,
                  x1_ref, h2t_ref, h2b_ref, idx_ref, gate_ref, rank_ref, cnt_ref, base_ref):
    i = pl.program_id(0)

    @pl.when(i == 0)
    def _():
        base_ref[...] = jnp.zeros_like(base_ref)

    mod = mod_ref[0]
    ya = _dot(a_ref[...], wa_ref[...])
    yb = _dot(o_ref[...], wb_ref[...])
    y = sga_ref[...].astype(F32) * ya + sgb_ref[...].astype(F32) * yb
    x1 = x_ref[...] + mod[2:3] * _dot(y.astype(BF16), wo_ref[...])
    x1_ref[...] = x1
    h2 = _rms_mod(x1, g2_ref[...], mod[3:4], mod[4:5])
    h2b_ref[...] = h2.astype(BF16)
    _store_row_tiles(h2t_ref, h2)

    hh = h2.astype(BF16)
    hl = (h2 - hh.astype(F32)).astype(BF16)
    dn = (((1,), (1,)), ((), ()))
    logits = (lax.dot_general(rwh_ref[...], hh, dn, preferred_element_type=F32)
              + lax.dot_general(rwl_ref[...], hh, dn, preferred_element_type=F32)
              + lax.dot_general(rwh_ref[...], hl, dn, preferred_element_type=F32))
    scores = jax.nn.sigmoid(logits)
    tm = scores.shape[1]
    eio = lax.broadcasted_iota(jnp.int32, scores.shape, 0).astype(F32)
    work = scores + rb_ref[...]
    picked = jnp.zeros(scores.shape, F32)
    idxs, vals = [], []
    for _ in range(TOP_K):
        m = jnp.max(work, axis=0, keepdims=True)
        ik = jnp.min(jnp.where(work == m, eio, float(N_EXPERTS)), axis=0, keepdims=True)
        oh = eio == ik
        vals.append(jnp.sum(jnp.where(oh, scores, 0.0), axis=0, keepdims=True))
        idxs.append(ik)
        work = jnp.where(oh, -jnp.inf, work)
        picked = picked + oh.astype(F32)
    total = vals[0]
    for vk in vals[1:]:
        total = total + vk

    tr = lax.broadcasted_iota(jnp.int32, (tm, tm), 0)
    tc = lax.broadcasted_iota(jnp.int32, (tm, tm), 1)
    before = (tr < tc).astype(BF16)
    base = base_ref[...]
    rank_e = _dot(picked.astype(BF16), before) + base
    ranks = [jnp.sum(jnp.where(eio == ik, rank_e, 0.0), axis=0, keepdims=True) for ik in idxs]
    base = base + jnp.sum(picked, axis=1, keepdims=True)
    base_ref[...] = base
    cnt_ref[...] = base

    idx_ref[...] = jnp.concatenate(idxs, axis=0).astype(jnp.int32)
    rank_ref[...] = jnp.concatenate(ranks, axis=0).astype(jnp.int32)
    gate_ref[...] = jnp.concatenate(vals, axis=0) * (ROUTE_SCALE / total)


def _merge(a, o, sga, sgb, x2d, mods, g2, wa, wb, wo, rwh, rwl, rb, seq):
    t = x2d.shape[0]
    tm = min(TM_MERGE, seq)
    spb = seq // tm
    row = lambda w: pl.BlockSpec((tm, w), lambda i: (i, 0))
    col = lambda: pl.BlockSpec((TOP_K, tm), lambda i: (0, i))
    const = lambda s: pl.BlockSpec(s, lambda i: (0,) * len(s))
    return pl.pallas_call(
        _merge_kernel,
        grid=(t // tm,),
        in_specs=[row(D_A), row(D_Q), row(D_MODEL), row(D_MODEL), row(D_MODEL),
                  pl.BlockSpec((1, 6, D_MODEL), lambda i: (i // spb, 0, 0)),
                  const((1, D_MODEL)),
                  const((D_A, D_MODEL)), const((D_Q, D_MODEL)), const((D_MODEL, D_MODEL)),
                  const((N_EXPERTS, D_MODEL)), const((N_EXPERTS, D_MODEL)), const((N_EXPERTS, 1))],
        out_specs=[row(D_MODEL), pl.BlockSpec((tm * ROW_TILES, LANES), lambda i: (i, 0)), row(D_MODEL),
                   col(), col(), col(), const((N_EXPERTS, 1))],
        out_shape=[jax.ShapeDtypeStruct((t, D_MODEL), F32),
                   jax.ShapeDtypeStruct((t * ROW_TILES, LANES), F32),
                   jax.ShapeDtypeStruct((t, D_MODEL), BF16),
                   jax.ShapeDtypeStruct((TOP_K, t), jnp.int32),
                   jax.ShapeDtypeStruct((TOP_K, t), F32),
                   jax.ShapeDtypeStruct((TOP_K, t), jnp.int32),
                   jax.ShapeDtypeStruct((N_EXPERTS, 1), F32)],
        scratch_shapes=[pltpu.VMEM((N_EXPERTS, 1), F32)],
        compiler_params=_params("arbitrary"),
        name="merge",
    )(a, o, sga, sgb, x2d, mods, g2, wa, wb, wo, rwh, rwl, rb)


def _row_copy(src, si, dst, di, sem):
    return pltpu.make_async_copy(src.at[pl.ds(pl.multiple_of(si * ROW_TILES, ROW_TILES), ROW_TILES), :],
                                 dst.at[pl.ds(pl.multiple_of(di * ROW_TILES, ROW_TILES), ROW_TILES), :], sem)


def _rows_wait(ref, nrows, sem):
    n = nrows * ROW_TILES
    pltpu.make_async_copy(ref.at[pl.ds(0, n), :], ref.at[pl.ds(0, n), :], sem).wait()


def _dispatch_kernel(nsteps, pad0_ref, padn_ref, nval_ref, dest_ref, h_ref, xs_ref, zero_ref, sem, zsem):
    i = pl.program_id(0)
    tm = dest_ref.shape[1]
    nblocks = xs_ref.shape[0] // (EXP_BLK * ROW_TILES)
    experts_per_step = -(-N_EXPERTS // nsteps)
    tail_per_step = -(-nblocks // nsteps)

    @pl.when(i == 0)
    def _():
        zero_ref[...] = jnp.zeros_like(zero_ref)

    def body(j, carry):
        for k in range(TOP_K):
            _row_copy(h_ref, j, xs_ref, dest_ref[k, j], sem).start()
        return carry

    lax.fori_loop(0, tm, body, 0)

    def zero_fill(act):
        def pad_body(r, carry):
            e = i * experts_per_step + r

            @pl.when(e < N_EXPERTS)
            def _():
                first = pad0_ref[e]
                n = padn_ref[e]
                bit = EXP_BLK // 2
                while bit:
                    off = first + (n & ~(2 * bit - 1))

                    @pl.when((n & bit) != 0)
                    def _(bit=bit, off=off):
                        act(pltpu.make_async_copy(
                            zero_ref.at[pl.ds(0, bit * ROW_TILES), :],
                            xs_ref.at[pl.ds(pl.multiple_of(off * ROW_TILES, ROW_TILES), bit * ROW_TILES), :], zsem))
                    bit //= 2
            return carry

        lax.fori_loop(0, experts_per_step, pad_body, 0)

        def tail_body(r, carry):
            blk = nval_ref[0] + i * tail_per_step + r

            @pl.when(blk < nblocks)
            def _():
                rows = EXP_BLK * ROW_TILES
                act(pltpu.make_async_copy(zero_ref, xs_ref.at[pl.ds(pl.multiple_of(blk * rows, rows), rows), :],
                                          zsem))
            return carry

        lax.fori_loop(0, tail_per_step, tail_body, 0)

    zero_fill(lambda cp: cp.start())
    _rows_wait(xs_ref, tm * TOP_K, sem)
    zero_fill(lambda cp: cp.wait())


def _dispatch(pad0, padn, nval, dest, h2t, nrows):
    t = h2t.shape[0] // ROW_TILES
    tm = min(TM_DMA, t)
    nsteps = t // tm
    grid_spec = pltpu.PrefetchScalarGridSpec(
        num_scalar_prefetch=3,
        grid=(nsteps,),
        in_specs=[pl.BlockSpec((TOP_K, tm), lambda i, *_: (0, i), memory_space=pltpu.SMEM),
                  pl.BlockSpec((tm * ROW_TILES, LANES), lambda i, *_: (i, 0))],
        out_specs=pl.BlockSpec(memory_space=pl.ANY),
        scratch_shapes=[pltpu.VMEM((EXP_BLK * ROW_TILES, LANES), F32),
                        pltpu.SemaphoreType.DMA(()), pltpu.SemaphoreType.DMA(())],
    )
    return pl.pallas_call(
        functools.partial(_dispatch_kernel, nsteps),
        grid_spec=grid_spec,
        out_shape=jax.ShapeDtypeStruct((nrows * ROW_TILES, LANES), F32),
        compiler_params=_params("arbitrary"),
        name="dispatch",
    )(pad0, padn, nval, dest, h2t)


def _expert_kernel(bexp_ref, bsrc_ref, nval_ref, xs_ref, wg_ref, wu_ref, wd_ref, ys_ref, xb_ref):
    b = pl.program_id(0)

    @pl.when(b < nval_ref[0])
    def _():
        for s in range(ROW_TILES):
            xb_ref[:, s * LANES:(s + 1) * LANES] = _load_row_tiles(xs_ref, s).astype(BF16)
        x = xb_ref[...]
        g = _dot(x, wg_ref[0].astype(BF16))
        u = _dot(x, wu_ref[0].astype(BF16))
        act = (_silu(g) * u).astype(BF16)
        _store_row_tiles(ys_ref, _dot(act, wd_ref[0].astype(BF16)))

    @pl.when(b >= nval_ref[0])
    def _():
        ys_ref[...] = jnp.zeros_like(ys_ref)


def _experts(bexp, bsrc, nval, xs, wg, wu, wd):
    nrows = xs.shape[0] // ROW_TILES
    nb = nrows // EXP_BLK
    blk = (EXP_BLK * ROW_TILES, LANES)
    grid_spec = pltpu.PrefetchScalarGridSpec(
        num_scalar_prefetch=3,
        grid=(nb,),
        in_specs=[pl.BlockSpec(blk, lambda b, be, bs, nv: (bs[b], 0)),
                  pl.BlockSpec((1, D_MODEL, D_EXPERT), lambda b, be, bs, nv: (be[b], 0, 0)),
                  pl.BlockSpec((1, D_MODEL, D_EXPERT), lambda b, be, bs, nv: (be[b], 0, 0)),
                  pl.BlockSpec((1, D_EXPERT, D_MODEL), lambda b, be, bs, nv: (be[b], 0, 0))],
        out_specs=pl.BlockSpec(blk, lambda b, be, bs, nv: (b, 0)),
        scratch_shapes=[pltpu.VMEM((EXP_BLK, D_MODEL), BF16)],
    )
    return pl.pallas_call(
        _expert_kernel,
        grid_spec=grid_spec,
        out_shape=jax.ShapeDtypeStruct((nrows * ROW_TILES, LANES), F32),
        compiler_params=_params("arbitrary"),
        name="experts",
    )(bexp, bsrc, nval, xs, wg, wu, wd)


def _combine_kernel(dest_ref, ys_ref, gate_ref, x1_ref, h2_ref, mod_ref, sg_ref, su_ref, sd_ref, fg_ref,
                    out_ref, buf_ref, x2_ref, sem):
    tm = dest_ref.shape[1]

    def body(j, carry):
        for k in range(TOP_K):
            _row_copy(ys_ref, dest_ref[k, j], buf_ref.at[k], j, sem).start()
        return carry

    lax.fori_loop(0, tm, body, 0)

    mod = mod_ref[0]
    h = h2_ref[...]
    act = (_silu(_dot(h, sg_ref[...])) * _dot(h, su_ref[...])).astype(BF16)
    moe = _dot(act, sd_ref[...])

    g = gate_ref[...]
    g0 = g.astype(BF16)
    r1 = g - g0.astype(F32)
    g1 = r1.astype(BF16)
    g2 = (r1 - g1.astype(F32)).astype(BF16)
    eye = (lax.broadcasted_iota(jnp.int32, (tm, tm), 0)
           == lax.broadcasted_iota(jnp.int32, (tm, tm), 1)).astype(BF16)
    dn = (((1,), (1,)), ((), ()))
    gcol = (lax.dot_general(eye, g0, dn, preferred_element_type=F32)
            + lax.dot_general(eye, g1, dn, preferred_element_type=F32)
            + lax.dot_general(eye, g2, dn, preferred_element_type=F32))

    gk = [jnp.broadcast_to(gcol[:, k:k + 1], (tm, LANES)) for k in range(TOP_K)]
    for k in range(TOP_K):
        _rows_wait(buf_ref.at[k], tm, sem)
    ssq = jnp.zeros((tm, 1), F32)
    for s in range(ROW_TILES):
        sl = slice(s * LANES, (s + 1) * LANES)
        m = moe[:, sl]
        for k in range(TOP_K):
            m = m + gk[k] * _load_row_tiles(buf_ref.at[k], s)
        x2 = x1_ref[:, sl] + mod[5:6, sl] * m
        x2_ref[:, sl] = x2
        ssq = ssq + jnp.sum(x2 * x2, axis=-1, keepdims=True)
    out_ref[...] = x2_ref[...] * lax.rsqrt(ssq * (1.0 / D_MODEL) + EPS) * fg_ref[...]


def _combine(dest, ys, gate, x1, h2, mods, sg, su, sd, fg, seq):
    t = x1.shape[0]
    tm = min(TM_DMA, seq)
    spb = seq // tm
    row = lambda: pl.BlockSpec((tm, D_MODEL), lambda i: (i, 0))
    const = lambda s: pl.BlockSpec(s, lambda i: (0,) * len(s))
    return pl.pallas_call(
        _combine_kernel,
        grid=(t // tm,),
        in_specs=[pl.BlockSpec((TOP_K, tm), lambda i: (0, i), memory_space=pltpu.SMEM),
                  pl.BlockSpec(memory_space=pl.ANY),
                  pl.BlockSpec((TOP_K, tm), lambda i: (0, i)),
                  row(), row(),
                  pl.BlockSpec((1, 6, D_MODEL), lambda i: (i // spb, 0, 0)),
                  const((D_MODEL, D_SHARED)), const((D_MODEL, D_SHARED)), const((D_SHARED, D_MODEL)),
                  const((1, D_MODEL))],
        out_specs=row(),
        out_shape=jax.ShapeDtypeStruct((t, D_MODEL), F32),
        scratch_shapes=[pltpu.VMEM((TOP_K, tm * ROW_TILES, LANES), F32), pltpu.VMEM((tm, D_MODEL), F32),
                        pltpu.SemaphoreType.DMA(())],
        compiler_params=_params("arbitrary"),
        name="combine",
    )(dest, ys, gate, x1, h2, mods, sg, su, sd, fg)


def _rope_tables(seq):
    pos = jnp.arange(seq)
    n_freq = HEAD_DIM // 4
    inv = ROPE_THETA ** (-jnp.arange(n_freq, dtype=F32) / n_freq)
    ang_r = (pos // GRID_W)[:, None].astype(F32) * inv
    ang_c = (pos % GRID_W)[:, None].astype(F32) * inv
    cr, sr, cc, sc = jnp.cos(ang_r), jnp.sin(ang_r), jnp.cos(ang_c), jnp.sin(ang_c)
    cos = jnp.concatenate([cr, cr, cc, cc], axis=1)
    sin = jnp.concatenate([-sr, sr, -sc, sc], axis=1)
    reps = LANES // HEAD_DIM
    return jnp.tile(cos, (1, reps)), jnp.tile(sin, (1, reps))


def _layer(x2d, ctx2d, mods, batch, seq, norm1_g, norm2_g, w_in, ln_g, ln_b, gmlp_ws, gmlp_bs, sink,
           w_a, w_b, w_o, router_w, router_b, e_gate, e_up, e_down, s_gate, s_up, s_down, final_g):
    t = x2d.shape[0]
    g1 = norm1_g.reshape(1, D_MODEL)
    w_in_b = w_in.astype(BF16)
    cos, sin = _rope_tables(seq)

    kctx, vctx = _ctx_kv(ctx2d, mods, g1, w_in_b[:, C_K:C_GA])
    gu, vn, q, k, v, sga, sgb = _inproj(x2d, mods, g1, w_in_b, ln_g.reshape(1, D_A), ln_b.reshape(1, D_A),
                                        cos, sin, seq)

    bs_full = jnp.repeat(gmlp_bs.T, D_A // G_A, axis=1)
    a, o = _mix(sink, gu, vn, q, k, v, kctx, vctx, gmlp_ws.astype(BF16), bs_full, batch, seq)

    rwt = router_w.T
    rwh = rwt.astype(BF16)
    rwl = (rwt - rwh.astype(F32)).astype(BF16)
    x1, h2t, h2b, idx, gate, rank, counts = _merge(
        a, o, sga, sgb, x2d, mods, norm2_g.reshape(1, D_MODEL),
        w_a.astype(BF16), w_b.astype(BF16), w_o.astype(BF16), rwh, rwl, router_b.reshape(N_EXPERTS, 1), seq)

    cnt = counts[:, 0].astype(jnp.int32)
    pcnt = (cnt + EXP_BLK - 1) // EXP_BLK * EXP_BLK
    pend = jnp.cumsum(pcnt)
    pstart = pend - pcnt
    onehot = idx[:, :, None] == jnp.arange(N_EXPERTS, dtype=jnp.int32)
    dest = jnp.sum(jnp.where(onehot, pstart, 0), axis=-1) + rank
    nblocks = (t * TOP_K) // EXP_BLK + N_EXPERTS
    nval = (pend[-1] // EXP_BLK).astype(jnp.int32)
    bsrc = jnp.minimum(jnp.arange(nblocks, dtype=jnp.int32), nval - 1)
    bexp = jnp.sum((pend[None, :] <= (bsrc * EXP_BLK)[:, None]).astype(jnp.int32), axis=1)
    bexp = jnp.minimum(bexp, N_EXPERTS - 1)

    nval = nval.reshape(1)
    xs = _dispatch(pstart + cnt, pcnt - cnt, nval, dest, h2t, nblocks * EXP_BLK)
    ys = _experts(bexp, bsrc, nval, xs, e_gate, e_up, e_down)
    return _combine(dest, ys, gate, x1, h2b, mods, s_gate.astype(BF16), s_up.astype(BF16),
                    s_down.astype(BF16), final_g.reshape(1, D_MODEL), seq)


def kernel(x, c, ctx, c_ctx, ada_w, ada_b, norm1_g, norm2_g, w_in, gmlp_ln_g, gmlp_ln_b, gmlp_ws, gmlp_bs,
           attn_sink, w_branch_a, w_branch_b, w_out, router_w, router_b, exp_w_gate, exp_w_up, exp_w_down,
           sh_w_gate, sh_w_up, sh_w_down, final_g):
    batch, seq, _ = x.shape
    depth = ada_w.shape[0]
    assert depth == 1, "the context stream is only carried as keys/values of a single layer"
    assert batch + 1 <= 8 and seq % WBLK == 0
    cond = jnp.concatenate([c, c_ctx[None], jnp.zeros((8 - batch - 1, D_MODEL), F32)], axis=0)
    mods = _ada(cond, ada_w[0], ada_b[0])[:batch + 1].reshape(batch + 1, 6, D_MODEL)
    out = _layer(x.reshape(batch * seq, D_MODEL), ctx.reshape(-1, D_MODEL), mods, batch, seq,
                 norm1_g[0], norm2_g[0], w_in[0], gmlp_ln_g[0], gmlp_ln_b[0], gmlp_ws[0], gmlp_bs[0],
                 attn_sink[0], w_branch_a[0], w_branch_b[0], w_out[0], router_w[0], router_b[0],
                 exp_w_gate[0], exp_w_up[0], exp_w_down[0], sh_w_gate[0], sh_w_up[0], sh_w_down[0], final_g)
    return out.reshape(batch, seq, D_MODEL)
```

```python
import functools

import jax
import jax.numpy as jnp
from jax import lax
from jax.experimental import pallas as pl
from jax.experimental.pallas import tpu as pltpu

F32 = jnp.float32
BF16 = jnp.bfloat16

D_MODEL = 1024
EPS = 1e-6
GRID_W = 64
D_A = D_MODEL // 2
G_A = 4
CHUNK = 128
N_HEADS = 8
N_KV = 2
REP = N_HEADS // N_KV
HEAD_DIM = 64
D_Q = N_HEADS * HEAD_DIM
D_KV = N_KV * HEAD_DIM
WBLK = 128
ROPE_THETA = 10000.0
N_EXPERTS = 256
TOP_K = 8
D_EXPERT = D_MODEL // 4
D_SHARED = D_MODEL // 4
ROUTE_SCALE = 2.5

C_U = 0
C_V = D_A
C_Q = 2 * D_A
C_K = C_Q + D_Q
C_VAL = C_K + D_KV
C_GA = C_VAL + D_KV
C_GB = C_GA + D_MODEL
D_IN = C_GB + D_MODEL

LANES = 128
ROPE_HALF = HEAD_DIM // 4
NEG_BIG = -1e30

TM_PROJ = 512
TM_MERGE = 256
TM_DMA = 256
EXP_BLK = 256
SUBLANES = 8
ROW_TILES = D_MODEL // LANES
assert ROW_TILES == SUBLANES
VMEM_LIMIT = 56 * 1024 * 1024


def _gelu(x):
    return 0.5 * x * (1.0 + jnp.tanh(0.7978845608028654 * (x + 0.044715 * x * x * x)))


def _silu(x):
    return x * jax.nn.sigmoid(x)


def _dot(a, b):
    return jnp.dot(a, b, preferred_element_type=F32)


def _rms_mod(x, g, shift, scale):
    ms = jnp.mean(x * x, axis=-1, keepdims=True)
    return (x * lax.rsqrt(ms + EPS)) * g * (1.0 + scale) + shift


def _params(*sem):
    return pltpu.CompilerParams(dimension_semantics=sem, vmem_limit_bytes=VMEM_LIMIT)


def _ada_kernel(c_ref, w_ref, b_ref, o_ref):
    c = c_ref[...]
    s = _silu(c).astype(BF16)
    o_ref[...] = _dot(s, w_ref[...].astype(BF16)) + b_ref[...]


def _ada(cond8, ada_w, ada_b):
    n = ada_w.shape[1]
    tn = 1536
    return pl.pallas_call(
        _ada_kernel,
        grid=(n // tn,),
        in_specs=[pl.BlockSpec((8, D_MODEL), lambda j: (0, 0)),
                  pl.BlockSpec((D_MODEL, tn), lambda j: (0, j)),
                  pl.BlockSpec((1, tn), lambda j: (0, j))],
        out_specs=pl.BlockSpec((8, tn), lambda j: (0, j)),
        out_shape=jax.ShapeDtypeStruct((8, n), F32),
        compiler_params=_params("arbitrary"),
        name="ada",
    )(cond8, ada_w, ada_b.reshape(1, n))


def _ctx_kernel(x_ref, mod_ref, g_ref, w_ref, k_ref, v_ref):
    mod = mod_ref[0]
    h = _rms_mod(x_ref[...], g_ref[...], mod[0:1], mod[1:2]).astype(BF16)
    z = _dot(h, w_ref[...])
    k_ref[...] = z[:, :D_KV].astype(BF16)
    v_ref[...] = z[:, D_KV:].astype(BF16)


def _ctx_kv(ctx2d, mods, g1, w_kv):
    n = ctx2d.shape[0]
    nb = mods.shape[0] - 1
    return pl.pallas_call(
        _ctx_kernel,
        grid=(1,),
        in_specs=[pl.BlockSpec((n, D_MODEL), lambda i: (0, 0)),
                  pl.BlockSpec((1, 6, D_MODEL), lambda i: (nb, 0, 0)),
                  pl.BlockSpec((1, D_MODEL), lambda i: (0, 0)),
                  pl.BlockSpec((D_MODEL, 2 * D_KV), lambda i: (0, 0))],
        out_specs=[pl.BlockSpec((n, D_KV), lambda i: (0, 0)),
                   pl.BlockSpec((n, D_KV), lambda i: (0, 0))],
        out_shape=[jax.ShapeDtypeStruct((n, D_KV), BF16)] * 2,
        compiler_params=_params("arbitrary"),
        name="ctx_kv",
    )(ctx2d, mods, g1, w_kv)


def _rope(t, cos, sin):
    lane = lax.broadcasted_iota(jnp.int32, (t.shape[0], LANES), 1)
    first = (lane & (2 * ROPE_HALF - 1)) < ROPE_HALF
    outs = []
    for j in range(t.shape[1] // LANES):
        tj = t[:, j * LANES:(j + 1) * LANES]
        up = pltpu.roll(tj, LANES - ROPE_HALF, 1)
        dn = pltpu.roll(tj, ROPE_HALF, 1)
        outs.append(tj * cos + jnp.where(first, up, dn) * sin)
    return outs


def _inproj_kernel(x_ref, mod_ref, g_ref, w_ref, lng_ref, lnb_ref, cos_ref, sin_ref,
                   gu_ref, vn_ref, q_ref, k_ref, v_ref, sga_ref, sgb_ref):
    mod = mod_ref[0]
    h = _rms_mod(x_ref[...], g_ref[...], mod[0:1], mod[1:2]).astype(BF16)

    def proj(lo, hi):
        return _dot(h, w_ref[:, lo:hi])

    gu_ref[...] = _gelu(proj(C_U, C_V)).astype(BF16)

    v = _gelu(proj(C_V, C_Q))
    mu = jnp.mean(v, axis=-1, keepdims=True)
    vc = v - mu
    var = jnp.mean(vc * vc, axis=-1, keepdims=True)
    vn_ref[...] = (vc * lax.rsqrt(var + EPS) * lng_ref[...] + lnb_ref[...]).astype(BF16)

    cos = cos_ref[...]
    sin = sin_ref[...]
    q = _rope(proj(C_Q, C_K) * (HEAD_DIM ** -0.5), cos, sin)
    for j, qj in enumerate(q):
        q_ref[:, j * LANES:(j + 1) * LANES] = qj.astype(BF16)
    k = _rope(proj(C_K, C_VAL), cos, sin)
    k_ref[...] = k[0].astype(BF16)
    v_ref[...] = proj(C_VAL, C_GA).astype(BF16)
    sga_ref[...] = jax.nn.sigmoid(proj(C_GA, C_GB)).astype(BF16)
    sgb_ref[...] = jax.nn.sigmoid(proj(C_GB, D_IN)).astype(BF16)


def _inproj(x2d, mods, g1, w_in, lng, lnb, cos, sin, seq):
    t = x2d.shape[0]
    tm = min(TM_PROJ, seq)
    spb = seq // tm
    row = lambda w: pl.BlockSpec((tm, w), lambda i: (i, 0))
    const = lambda s: pl.BlockSpec(s, lambda i: (0,) * len(s))
    return pl.pallas_call(
        _inproj_kernel,
        grid=(t // tm,),
        in_specs=[row(D_MODEL),
                  pl.BlockSpec((1, 6, D_MODEL), lambda i: (i // spb, 0, 0)),
                  const((1, D_MODEL)),
                  const((D_MODEL, D_IN)),
                  const((1, D_A)), const((1, D_A)),
                  pl.BlockSpec((tm, LANES), lambda i: (i % spb, 0)),
                  pl.BlockSpec((tm, LANES), lambda i: (i % spb, 0))],
        out_specs=[row(D_A), row(D_A), row(D_Q), row(D_KV), row(D_KV), row(D_MODEL), row(D_MODEL)],
        out_shape=[jax.ShapeDtypeStruct((t, w), BF16)
                   for w in (D_A, D_A, D_Q, D_KV, D_KV, D_MODEL, D_MODEL)],
        compiler_params=_params("arbitrary"),
        name="inproj",
    )(x2d, mods, g1, w_in, lng, lnb, cos, sin)


def _mix_kernel(sink_ref, gu_ref, vn_ref, q_ref, kp_ref, kc_ref, kn_ref, vp_ref, vc_ref, vx_ref,
                kctx_ref, vctx_ref, ws_ref, bs_ref, a_ref, o_ref):
    n = pl.program_id(1)
    nblk = pl.num_programs(1)

    for g in range(G_A):
        sl = slice(g * CHUNK, (g + 1) * CHUNK)
        s = _dot(ws_ref[g], vn_ref[:, sl]) + bs_ref[:, sl]
        a_ref[:, sl] = (gu_ref[:, sl].astype(F32) * s).astype(BF16)

    kcat = jnp.concatenate([kp_ref[...], kc_ref[...], kn_ref[...], kctx_ref[...]], axis=0)
    vcat = jnp.concatenate([vp_ref[...], vc_ref[...], vx_ref[...], vctx_ref[...]], axis=0)
    nwin = 3 * WBLK
    nkeys = kcat.shape[0]
    rows = REP * WBLK
    qi = lax.broadcasted_iota(jnp.int32, (rows, nkeys), 0) & (WBLK - 1)
    kj = lax.broadcasted_iota(jnp.int32, (rows, nkeys), 1)
    lo = jnp.where(n == 0, WBLK, 0)
    hi = jnp.where(n == nblk - 1, 2 * WBLK, nwin)
    valid = (kj >= nwin) | ((kj >= qi) & (kj <= qi + 2 * WBLK) & (kj >= lo) & (kj < hi))
    rgrp = lax.broadcasted_iota(jnp.int32, (rows, 1), 0) // WBLK

    for kvh in range(N_KV):
        ksl = kcat[:, kvh * HEAD_DIM:(kvh + 1) * HEAD_DIM]
        vsl = vcat[:, kvh * HEAD_DIM:(kvh + 1) * HEAD_DIM]
        qs = jnp.concatenate(
            [q_ref[:, (kvh * REP + r) * HEAD_DIM:(kvh * REP + r + 1) * HEAD_DIM] for r in range(REP)],
            axis=0)
        sink = jnp.zeros((rows, 1), F32)
        for r in range(REP):
            sink = jnp.where(rgrp == r, sink_ref[kvh * REP + r], sink)
        s = lax.dot_general(qs, ksl, (((1,), (1,)), ((), ())), preferred_element_type=F32)
        s = jnp.where(valid, s, NEG_BIG)
        m = jnp.maximum(jnp.max(s, axis=-1, keepdims=True), sink)
        p = jnp.exp(s - m)
        den = jnp.sum(p, axis=-1, keepdims=True) + jnp.exp(sink - m)
        o = _dot(p.astype(BF16), vsl) / den
        for r in range(0, REP, 2):
            pair = jnp.concatenate([o[r * WBLK:(r + 1) * WBLK], o[(r + 1) * WBLK:(r + 2) * WBLK]], axis=1)
            c0 = (kvh * REP + r) * HEAD_DIM
            o_ref[:, c0:c0 + 2 * HEAD_DIM] = pair.astype(BF16)


def _mix(sink, gu, vn, q, k, v, kctx, vctx, ws, bs_full, batch, seq):
    t = gu.shape[0]
    nblk = seq // WBLK
    nctx = kctx.shape[0] // batch
    cur = lambda w: pl.BlockSpec((WBLK, w), lambda b, n: (b * nblk + n, 0))
    prev = pl.BlockSpec((WBLK, D_KV), lambda b, n: (b * nblk + jnp.maximum(n - 1, 0), 0))
    nxt = pl.BlockSpec((WBLK, D_KV), lambda b, n: (b * nblk + jnp.minimum(n + 1, nblk - 1), 0))
    same = pl.BlockSpec((WBLK, D_KV), lambda b, n: (b * nblk + n, 0))
    cblk = pl.BlockSpec((nctx, D_KV), lambda b, n: (b, 0))
    return pl.pallas_call(
        _mix_kernel,
        grid=(batch, nblk),
        in_specs=[pl.BlockSpec(memory_space=pltpu.SMEM),
                  cur(D_A), cur(D_A), cur(D_Q),
                  prev, same, nxt, prev, same, nxt,
                  cblk, cblk,
                  pl.BlockSpec((G_A, CHUNK, CHUNK), lambda b, n: (0, 0, 0)),
                  pl.BlockSpec((CHUNK, D_A), lambda b, n: (0, 0))],
        out_specs=[cur(D_A), cur(D_Q)],
        out_shape=[jax.ShapeDtypeStruct((t, D_A), BF16), jax.ShapeDtypeStruct((t, D_Q), BF16)],
        compiler_params=_params("arbitrary", "arbitrary"),
        name="mix",
    )(sink, gu, vn, q, k, k, k, v, v, v, kctx, vctx, ws, bs_full)


def _store_row_tiles(ref, val):
    rows = val.shape[0]
    for s in range(ROW_TILES):
        ref[pl.ds(s, rows, stride=ROW_TILES), :] = val[:, s * LANES:(s + 1) * LANES]


def _load_row_tiles(ref, s):
    return ref[pl.ds(s, ref.shape[0] // ROW_TILES, stride=ROW_TILES), :]


def _merge_kernel(a_ref, o_ref, sga_ref, sgb_ref, x_ref, mod_ref, g2_ref, wa_ref, wb_ref, wo_ref,
                  rwh_ref, rwl_ref, rb_ref,
                  x1_ref, h2t_ref, h2b_ref, idx_ref, gate_ref, rank_ref, cnt_ref, base_ref):
    i = pl.program_id(0)

    @pl.when(i == 0)
    def _():
        base_ref[...] = jnp.zeros_like(base_ref)

    mod = mod_ref[0]
    ya = _dot(a_ref[...], wa_ref[...])
    yb = _dot(o_ref[...], wb_ref[...])
    y = sga_ref[...].astype(F32) * ya + sgb_ref[...].astype(F32) * yb
    x1 = x_ref[...] + mod[2:3] * _dot(y.astype(BF16), wo_ref[...])
    x1_ref[...] = x1
    h2 = _rms_mod(x1, g2_ref[...], mod[3:4], mod[4:5])
    h2b_ref[...] = h2.astype(BF16)
    _store_row_tiles(h2t_ref, h2)

    hh = h2.astype(BF16)
    hl = (h2 - hh.astype(F32)).astype(BF16)
    dn = (((1,), (1,)), ((), ()))
    logits = (lax.dot_general(rwh_ref[...], hh, dn, preferred_element_type=F32)
              + lax.dot_general(rwl_ref[...], hh, dn, preferred_element_type=F32)
              + lax.dot_general(rwh_ref[...], hl, dn, preferred_element_type=F32))
    scores = jax.nn.sigmoid(logits)
    tm = scores.shape[1]
    eio = lax.broadcasted_iota(jnp.int32, scores.shape, 0).astype(F32)
    work = scores + rb_ref[...]
    picked = jnp.zeros(scores.shape, F32)
    idxs, vals = [], []
    for _ in range(TOP_K):
        m = jnp.max(work, axis=0, keepdims=True)
        ik = jnp.min(jnp.where(work == m, eio, float(N_EXPERTS)), axis=0, keepdims=True)
        oh = eio == ik
        vals.append(jnp.sum(jnp.where(oh, scores, 0.0), axis=0, keepdims=True))
        idxs.append(ik)
        work = jnp.where(oh, -jnp.inf, work)
        picked = picked + oh.astype(F32)
    total = vals[0]
    for vk in vals[1:]:
        total = total + vk

    tr = lax.broadcasted_iota(jnp.int32, (tm, tm), 0)
    tc = lax.broadcasted_iota(jnp.int32, (tm, tm), 1)
    before = (tr < tc).astype(BF16)
    base = base_ref[...]
    rank_e = _dot(picked.astype(BF16), before) + base
    ranks = [jnp.sum(jnp.where(eio == ik, rank_e, 0.0), axis=0, keepdims=True) for ik in idxs]
    base = base + jnp.sum(picked, axis=1, keepdims=True)
    base_ref[...] = base
    cnt_ref[...] = base

    idx_ref[...] = jnp.concatenate(idxs, axis=0).astype(jnp.int32)
    rank_ref[...] = jnp.concatenate(ranks, axis=0).astype(jnp.int32)
    gate_ref[...] = jnp.concatenate(vals, axis=0) * (ROUTE_SCALE / total)


def _merge(a, o, sga, sgb, x2d, mods, g2, wa, wb, wo, rwh, rwl, rb, seq):
    t = x2d.shape[0]
    tm = min(TM_MERGE, seq)
    spb = seq // tm
    row = lambda w: pl.BlockSpec((tm, w), lambda i: (i, 0))
    col = lambda: pl.BlockSpec((TOP_K, tm), lambda i: (0, i))
    const = lambda s: pl.BlockSpec(s, lambda i: (0,) * len(s))
    return pl.pallas_call(
        _merge_kernel,
        grid=(t // tm,),
        in_specs=[row(D_A), row(D_Q), row(D_MODEL), row(D_MODEL), row(D_MODEL),
                  pl.BlockSpec((1, 6, D_MODEL), lambda i: (i // spb, 0, 0)),
                  const((1, D_MODEL)),
                  const((D_A, D_MODEL)), const((D_Q, D_MODEL)), const((D_MODEL, D_MODEL)),
                  const((N_EXPERTS, D_MODEL)), const((N_EXPERTS, D_MODEL)), const((N_EXPERTS, 1))],
        out_specs=[row(D_MODEL), pl.BlockSpec((tm * ROW_TILES, LANES), lambda i: (i, 0)), row(D_MODEL),
                   col(), col(), col(), const((N_EXPERTS, 1))],
        out_shape=[jax.ShapeDtypeStruct((t, D_MODEL), F32),
                   jax.ShapeDtypeStruct((t * ROW_TILES, LANES), F32),
                   jax.ShapeDtypeStruct((t, D_MODEL), BF16),
                   jax.ShapeDtypeStruct((TOP_K, t), jnp.int32),
                   jax.ShapeDtypeStruct((TOP_K, t), F32),
                   jax.ShapeDtypeStruct((TOP_K, t), jnp.int32),
                   jax.ShapeDtypeStruct((N_EXPERTS, 1), F32)],
        scratch_shapes=[pltpu.VMEM((N_EXPERTS, 1), F32)],
        compiler_params=_params("arbitrary"),
        name="merge",
    )(a, o, sga, sgb, x2d, mods, g2, wa, wb, wo, rwh, rwl, rb)


def _row_copy(src, si, dst, di, sem):
    return pltpu.make_async_copy(src.at[pl.ds(pl.multiple_of(si * ROW_TILES, ROW_TILES), ROW_TILES), :],
                                 dst.at[pl.ds(pl.multiple_of(di * ROW_TILES, ROW_TILES), ROW_TILES), :], sem)


def _rows_wait(ref, nrows, sem):
    n = nrows * ROW_TILES
    pltpu.make_async_copy(ref.at[pl.ds(0, n), :], ref.at[pl.ds(0, n), :], sem).wait()


def _dispatch_kernel(nsteps, pad0_ref, padn_ref, nval_ref, dest_ref, h_ref, xs_ref, zero_ref, sem, zsem):
    i = pl.program_id(0)
    tm = dest_ref.shape[1]
    nblocks = xs_ref.shape[0] // (EXP_BLK * ROW_TILES)
    experts_per_step = -(-N_EXPERTS // nsteps)
    tail_per_step = -(-nblocks // nsteps)

    @pl.when(i == 0)
    def _():
        zero_ref[...] = jnp.zeros_like(zero_ref)

    def body(j, carry):
        for k in range(TOP_K):
            _row_copy(h_ref, j, xs_ref, dest_ref[k, j], sem).start(priority=k % 2)
        return carry

    lax.fori_loop(0, tm, body, 0)

    def zero_fill(act):
        def pad_body(r, carry):
            e = i * experts_per_step + r

            @pl.when(e < N_EXPERTS)
            def _():
                first = pad0_ref[e]
                n = padn_ref[e]
                bit = EXP_BLK // 2
                while bit:
                    off = first + (n & ~(2 * bit - 1))

                    @pl.when((n & bit) != 0)
                    def _(bit=bit, off=off):
                        act(pltpu.make_async_copy(
                            zero_ref.at[pl.ds(0, bit * ROW_TILES), :],
                            xs_ref.at[pl.ds(pl.multiple_of(off * ROW_TILES, ROW_TILES), bit * ROW_TILES), :], zsem))
                    bit //= 2
            return carry

        lax.fori_loop(0, experts_per_step, pad_body, 0)

        def tail_body(r, carry):
            blk = nval_ref[0] + i * tail_per_step + r

            @pl.when(blk < nblocks)
            def _():
                rows = EXP_BLK * ROW_TILES
                act(pltpu.make_async_copy(zero_ref, xs_ref.at[pl.ds(pl.multiple_of(blk * rows, rows), rows), :],
                                          zsem))
            return carry

        lax.fori_loop(0, tail_per_step, tail_body, 0)

    zero_fill(lambda cp: cp.start())
    _rows_wait(xs_ref, tm * TOP_K, sem)
    zero_fill(lambda cp: cp.wait())


def _dispatch(pad0, padn, nval, dest, h2t, nrows):
    t = h2t.shape[0] // ROW_TILES
    tm = min(TM_DMA, t)
    nsteps = t // tm
    grid_spec = pltpu.PrefetchScalarGridSpec(
        num_scalar_prefetch=3,
        grid=(nsteps,),
        in_specs=[pl.BlockSpec((TOP_K, tm), lambda i, *_: (0, i), memory_space=pltpu.SMEM),
                  pl.BlockSpec((tm * ROW_TILES, LANES), lambda i, *_: (i, 0))],
        out_specs=pl.BlockSpec(memory_space=pl.ANY),
        scratch_shapes=[pltpu.VMEM((EXP_BLK * ROW_TILES, LANES), F32),
                        pltpu.SemaphoreType.DMA(()), pltpu.SemaphoreType.DMA(())],
    )
    return pl.pallas_call(
        functools.partial(_dispatch_kernel, nsteps),
        grid_spec=grid_spec,
        out_shape=jax.ShapeDtypeStruct((nrows * ROW_TILES, LANES), F32),
        compiler_params=_params("arbitrary"),
        name="dispatch",
    )(pad0, padn, nval, dest, h2t)


def _expert_kernel(bexp_ref, bsrc_ref, nval_ref, xs_ref, wg_ref, wu_ref, wd_ref, ys_ref, xb_ref):
    b = pl.program_id(0)

    @pl.when(b < nval_ref[0])
    def _():
        for s in range(ROW_TILES):
            xb_ref[:, s * LANES:(s + 1) * LANES] = _load_row_tiles(xs_ref, s).astype(BF16)
        x = xb_ref[...]
        g = _dot(x, wg_ref[0].astype(BF16))
        u = _dot(x, wu_ref[0].astype(BF16))
        act = (_silu(g) * u).astype(BF16)
        _store_row_tiles(ys_ref, _dot(act, wd_ref[0].astype(BF16)))

    @pl.when(b >= nval_ref[0])
    def _():
        ys_ref[...] = jnp.zeros_like(ys_ref)


def _experts(bexp, bsrc, nval, xs, wg, wu, wd):
    nrows = xs.shape[0] // ROW_TILES
    nb = nrows // EXP_BLK
    blk = (EXP_BLK * ROW_TILES, LANES)
    grid_spec = pltpu.PrefetchScalarGridSpec(
        num_scalar_prefetch=3,
        grid=(nb,),
        in_specs=[pl.BlockSpec(blk, lambda b, be, bs, nv: (bs[b], 0)),
                  pl.BlockSpec((1, D_MODEL, D_EXPERT), lambda b, be, bs, nv: (be[b], 0, 0)),
                  pl.BlockSpec((1, D_MODEL, D_EXPERT), lambda b, be, bs, nv: (be[b], 0, 0)),
                  pl.BlockSpec((1, D_EXPERT, D_MODEL), lambda b, be, bs, nv: (be[b], 0, 0))],
        out_specs=pl.BlockSpec(blk, lambda b, be, bs, nv: (b, 0)),
        scratch_shapes=[pltpu.VMEM((EXP_BLK, D_MODEL), BF16)],
    )
    return pl.pallas_call(
        _expert_kernel,
        grid_spec=grid_spec,
        out_shape=jax.ShapeDtypeStruct((nrows * ROW_TILES, LANES), F32),
        compiler_params=_params("arbitrary"),
        name="experts",
    )(bexp, bsrc, nval, xs, wg, wu, wd)


def _combine_kernel(dest_ref, ys_ref, gate_ref, x1_ref, h2_ref, mod_ref, sg_ref, su_ref, sd_ref, fg_ref,
                    out_ref, buf_ref, x2_ref, sem):
    tm = dest_ref.shape[1]

    def body(j, carry):
        for k in range(TOP_K):
            _row_copy(ys_ref, dest_ref[k, j], buf_ref.at[k], j, sem).start(priority=k % 2)
        return carry

    lax.fori_loop(0, tm, body, 0)

    mod = mod_ref[0]
    h = h2_ref[...]
    act = (_silu(_dot(h, sg_ref[...])) * _dot(h, su_ref[...])).astype(BF16)
    moe = _dot(act, sd_ref[...])

    g = gate_ref[...]
    g0 = g.astype(BF16)
    r1 = g - g0.astype(F32)
    g1 = r1.astype(BF16)
    g2 = (r1 - g1.astype(F32)).astype(BF16)
    eye = (lax.broadcasted_iota(jnp.int32, (tm, tm), 0)
           == lax.broadcasted_iota(jnp.int32, (tm, tm), 1)).astype(BF16)
    dn = (((1,), (1,)), ((), ()))
    gcol = (lax.dot_general(eye, g0, dn, preferred_element_type=F32)
            + lax.dot_general(eye, g1, dn, preferred_element_type=F32)
            + lax.dot_general(eye, g2, dn, preferred_element_type=F32))

    gk = [jnp.broadcast_to(gcol[:, k:k + 1], (tm, LANES)) for k in range(TOP_K)]
    for k in range(TOP_K):
        _rows_wait(buf_ref.at[k], tm, sem)
    ssq = jnp.zeros((tm, 1), F32)
    for s in range(ROW_TILES):
        sl = slice(s * LANES, (s + 1) * LANES)
        m = moe[:, sl]
        for k in range(TOP_K):
            m = m + gk[k] * _load_row_tiles(buf_ref.at[k], s)
        x2 = x1_ref[:, sl] + mod[5:6, sl] * m
        x2_ref[:, sl] = x2
        ssq = ssq + jnp.sum(x2 * x2, axis=-1, keepdims=True)
    out_ref[...] = x2_ref[...] * lax.rsqrt(ssq * (1.0 / D_MODEL) + EPS) * fg_ref[...]


def _combine(dest, ys, gate, x1, h2, mods, sg, su, sd, fg, seq):
    t = x1.shape[0]
    tm = min(TM_DMA, seq)
    spb = seq // tm
    row = lambda: pl.BlockSpec((tm, D_MODEL), lambda i: (i, 0))
    const = lambda s: pl.BlockSpec(s, lambda i: (0,) * len(s))
    return pl.pallas_call(
        _combine_kernel,
        grid=(t // tm,),
        in_specs=[pl.BlockSpec((TOP_K, tm), lambda i: (0, i), memory_space=pltpu.SMEM),
                  pl.BlockSpec(memory_space=pl.ANY),
                  pl.BlockSpec((TOP_K, tm), lambda i: (0, i)),
                  row(), row(),
                  pl.BlockSpec((1, 6, D_MODEL), lambda i: (i // spb, 0, 0)),
                  const((D_MODEL, D_SHARED)), const((D_MODEL, D_SHARED)), const((D_SHARED, D_MODEL)),
                  const((1, D_MODEL))],
        out_specs=row(),
        out_shape=jax.ShapeDtypeStruct((t, D_MODEL), F32),
        scratch_shapes=[pltpu.VMEM((TOP_K, tm * ROW_TILES, LANES), F32), pltpu.VMEM((tm, D_MODEL), F32),
                        pltpu.SemaphoreType.DMA(())],
        compiler_params=_params("arbitrary"),
        name="combine",
    )(dest, ys, gate, x1, h2, mods, sg, su, sd, fg)


def _rope_tables(seq):
    pos = jnp.arange(seq)
    n_freq = HEAD_DIM // 4
    inv = ROPE_THETA ** (-jnp.arange(n_freq, dtype=F32) / n_freq)
    ang_r = (pos // GRID_W)[:, None].astype(F32) * inv
    ang_c = (pos % GRID_W)[:, None].astype(F32) * inv
    cr, sr, cc, sc = jnp.cos(ang_r), jnp.sin(ang_r), jnp.cos(ang_c), jnp.sin(ang_c)
    cos = jnp.concatenate([cr, cr, cc, cc], axis=1)
    sin = jnp.concatenate([-sr, sr, -sc, sc], axis=1)
    reps = LANES // HEAD_DIM
    return jnp.tile(cos, (1, reps)), jnp.tile(sin, (1, reps))


def _layer(x2d, ctx2d, mods, batch, seq, norm1_g, norm2_g, w_in, ln_g, ln_b, gmlp_ws, gmlp_bs, sink,
           w_a, w_b, w_o, router_w, router_b, e_gate, e_up, e_down, s_gate, s_up, s_down, final_g):
    t = x2d.shape[0]
    g1 = norm1_g.reshape(1, D_MODEL)
    w_in_b = w_in.astype(BF16)
    cos, sin = _rope_tables(seq)

    kctx, vctx = _ctx_kv(ctx2d, mods, g1, w_in_b[:, C_K:C_GA])
    gu, vn, q, k, v, sga, sgb = _inproj(x2d, mods, g1, w_in_b, ln_g.reshape(1, D_A), ln_b.reshape(1, D_A),
                                        cos, sin, seq)

    bs_full = jnp.repeat(gmlp_bs.T, D_A // G_A, axis=1)
    a, o = _mix(sink, gu, vn, q, k, v, kctx, vctx, gmlp_ws.astype(BF16), bs_full, batch, seq)

    rwt = router_w.T
    rwh = rwt.astype(BF16)
    rwl = (rwt - rwh.astype(F32)).astype(BF16)
    x1, h2t, h2b, idx, gate, rank, counts = _merge(
        a, o, sga, sgb, x2d, mods, norm2_g.reshape(1, D_MODEL),
        w_a.astype(BF16), w_b.astype(BF16), w_o.astype(BF16), rwh, rwl, router_b.reshape(N_EXPERTS, 1), seq)

    cnt = counts[:, 0].astype(jnp.int32)
    pcnt = (cnt + EXP_BLK - 1) // EXP_BLK * EXP_BLK
    pend = jnp.cumsum(pcnt)
    pstart = pend - pcnt
    onehot = idx[:, :, None] == jnp.arange(N_EXPERTS, dtype=jnp.int32)
    dest = jnp.sum(jnp.where(onehot, pstart, 0), axis=-1) + rank
    nblocks = (t * TOP_K) // EXP_BLK + N_EXPERTS
    nval = (pend[-1] // EXP_BLK).astype(jnp.int32)
    bsrc = jnp.minimum(jnp.arange(nblocks, dtype=jnp.int32), nval - 1)
    bexp = jnp.sum((pend[None, :] <= (bsrc * EXP_BLK)[:, None]).astype(jnp.int32), axis=1)
    bexp = jnp.minimum(bexp, N_EXPERTS - 1)

    nval = nval.reshape(1)
    xs = _dispatch(pstart + cnt, pcnt - cnt, nval, dest, h2t, nblocks * EXP_BLK)
    ys = _experts(bexp, bsrc, nval, xs, e_gate, e_up, e_down)
    return _combine(dest, ys, gate, x1, h2b, mods, s_gate.astype(BF16), s_up.astype(BF16),
                    s_down.astype(BF16), final_g.reshape(1, D_MODEL), seq)


def kernel(x, c, ctx, c_ctx, ada_w, ada_b, norm1_g, norm2_g, w_in, gmlp_ln_g, gmlp_ln_b, gmlp_ws, gmlp_bs,
           attn_sink, w_branch_a, w_branch_b, w_out, router_w, router_b, exp_w_gate, exp_w_up, exp_w_down,
           sh_w_gate, sh_w_up, sh_w_down, final_g):
    batch, seq, _ = x.shape
    depth = ada_w.shape[0]
    assert depth == 1, "the context stream is only carried as keys/values of a single layer"
    assert batch + 1 <= 8 and seq % WBLK == 0
    cond = jnp.concatenate([c, c_ctx[None], jnp.zeros((8 - batch - 1, D_MODEL), F32)], axis=0)
    mods = _ada(cond, ada_w[0], ada_b[0])[:batch + 1].reshape(batch + 1, 6, D_MODEL)
    out = _layer(x.reshape(batch * seq, D_MODEL), ctx.reshape(-1, D_MODEL), mods, batch, seq,
                 norm1_g[0], norm2_g[0], w_in[0], gmlp_ln_g[0], gmlp_ln_b[0], gmlp_ws[0], gmlp_bs[0],
                 attn_sink[0], w_branch_a[0], w_branch_b[0], w_out[0], router_w[0], router_b[0],
                 exp_w_gate[0], exp_w_up[0], exp_w_down[0], sh_w_gate[0], sh_w_up[0], sh_w_down[0], final_g)
    return out.reshape(batch, seq, D_MODEL)
```

```python
import functools

import jax
import jax.numpy as jnp
from jax import lax
from jax.experimental import pallas as pl
from jax.experimental.pallas import tpu as pltpu

F32 = jnp.float32
BF16 = jnp.bfloat16

D_MODEL = 1024
EPS = 1e-6
GRID_W = 64
D_A = D_MODEL // 2
G_A = 4
CHUNK = 128
N_HEADS = 8
N_KV = 2
REP = N_HEADS // N_KV
HEAD_DIM = 64
D_Q = N_HEADS * HEAD_DIM
D_KV = N_KV * HEAD_DIM
WBLK = 128
ROPE_THETA = 10000.0
N_EXPERTS = 256
TOP_K = 8
D_EXPERT = D_MODEL // 4
D_SHARED = D_MODEL // 4
ROUTE_SCALE = 2.5

C_U = 0
C_V = D_A
C_Q = 2 * D_A
C_K = C_Q + D_Q
C_VAL = C_K + D_KV
C_GA = C_VAL + D_KV
C_GB = C_GA + D_MODEL
D_IN = C_GB + D_MODEL

LANES = 128
ROPE_HALF = HEAD_DIM // 4
NEG_BIG = -1e30

TM_PROJ = 512
TM_MERGE = 256
TM_DMA = 256
EXP_BLK = 256
IMAP_GROUP = 8
SUBLANES = 8
ROW_TILES = D_MODEL // LANES
assert ROW_TILES == SUBLANES
VMEM_LIMIT = 56 * 1024 * 1024


def _gelu(x):
    return 0.5 * x * (1.0 + jnp.tanh(0.7978845608028654 * (x + 0.044715 * x * x * x)))


def _silu(x):
    return x * jax.nn.sigmoid(x)


def _dot(a, b):
    return jnp.dot(a, b, preferred_element_type=F32)


def _rms_mod(x, g, shift, scale):
    ms = jnp.mean(x * x, axis=-1, keepdims=True)
    return (x * lax.rsqrt(ms + EPS)) * g * (1.0 + scale) + shift


def _params(*sem):
    return pltpu.CompilerParams(dimension_semantics=sem, vmem_limit_bytes=VMEM_LIMIT)


def _ada_kernel(c_ref, w_ref, b_ref, o_ref):
    c = c_ref[...]
    s = _silu(c).astype(BF16)
    o_ref[...] = _dot(s, w_ref[...].astype(BF16)) + b_ref[...]


def _ada(cond8, ada_w, ada_b):
    n = ada_w.shape[1]
    tn = 1536
    return pl.pallas_call(
        _ada_kernel,
        grid=(n // tn,),
        in_specs=[pl.BlockSpec((8, D_MODEL), lambda j: (0, 0)),
                  pl.BlockSpec((D_MODEL, tn), lambda j: (0, j)),
                  pl.BlockSpec((1, tn), lambda j: (0, j))],
        out_specs=pl.BlockSpec((8, tn), lambda j: (0, j)),
        out_shape=jax.ShapeDtypeStruct((8, n), F32),
        compiler_params=_params("arbitrary"),
        name="ada",
    )(cond8, ada_w, ada_b.reshape(1, n))


def _ctx_kernel(x_ref, mod_ref, g_ref, w_ref, k_ref, v_ref):
    mod = mod_ref[0]
    h = _rms_mod(x_ref[...], g_ref[...], mod[0:1], mod[1:2]).astype(BF16)
    z = _dot(h, w_ref[...])
    k_ref[...] = z[:, :D_KV].astype(BF16)
    v_ref[...] = z[:, D_KV:].astype(BF16)


def _ctx_kv(ctx2d, mods, g1, w_kv):
    n = ctx2d.shape[0]
    nb = mods.shape[0] - 1
    return pl.pallas_call(
        _ctx_kernel,
        grid=(1,),
        in_specs=[pl.BlockSpec((n, D_MODEL), lambda i: (0, 0)),
                  pl.BlockSpec((1, 6, D_MODEL), lambda i: (nb, 0, 0)),
                  pl.BlockSpec((1, D_MODEL), lambda i: (0, 0)),
                  pl.BlockSpec((D_MODEL, 2 * D_KV), lambda i: (0, 0))],
        out_specs=[pl.BlockSpec((n, D_KV), lambda i: (0, 0)),
                   pl.BlockSpec((n, D_KV), lambda i: (0, 0))],
        out_shape=[jax.ShapeDtypeStruct((n, D_KV), BF16)] * 2,
        compiler_params=_params("arbitrary"),
        name="ctx_kv",
    )(ctx2d, mods, g1, w_kv)


def _rope(t, cos, sin):
    lane = lax.broadcasted_iota(jnp.int32, (t.shape[0], LANES), 1)
    first = (lane & (2 * ROPE_HALF - 1)) < ROPE_HALF
    outs = []
    for j in range(t.shape[1] // LANES):
        tj = t[:, j * LANES:(j + 1) * LANES]
        up = pltpu.roll(tj, LANES - ROPE_HALF, 1)
        dn = pltpu.roll(tj, ROPE_HALF, 1)
        outs.append(tj * cos + jnp.where(first, up, dn) * sin)
    return outs


def _inproj_kernel(x_ref, mod_ref, g_ref, w_ref, lng_ref, lnb_ref, cos_ref, sin_ref,
                   gu_ref, vn_ref, q_ref, k_ref, v_ref, sga_ref, sgb_ref):
    mod = mod_ref[0]
    h = _rms_mod(x_ref[...], g_ref[...], mod[0:1], mod[1:2]).astype(BF16)

    def proj(lo, hi):
        return _dot(h, w_ref[:, lo:hi])

    gu_ref[...] = _gelu(proj(C_U, C_V)).astype(BF16)

    v = _gelu(proj(C_V, C_Q))
    mu = jnp.mean(v, axis=-1, keepdims=True)
    vc = v - mu
    var = jnp.mean(vc * vc, axis=-1, keepdims=True)
    vn_ref[...] = (vc * lax.rsqrt(var + EPS) * lng_ref[...] + lnb_ref[...]).astype(BF16)

    cos = cos_ref[...]
    sin = sin_ref[...]
    q = _rope(proj(C_Q, C_K) * (HEAD_DIM ** -0.5), cos, sin)
    for j, qj in enumerate(q):
        q_ref[:, j * LANES:(j + 1) * LANES] = qj.astype(BF16)
    k = _rope(proj(C_K, C_VAL), cos, sin)
    k_ref[...] = k[0].astype(BF16)
    v_ref[...] = proj(C_VAL, C_GA).astype(BF16)
    sga_ref[...] = jax.nn.sigmoid(proj(C_GA, C_GB)).astype(BF16)
    sgb_ref[...] = jax.nn.sigmoid(proj(C_GB, D_IN)).astype(BF16)


def _inproj(x2d, mods, g1, w_in, lng, lnb, cos, sin, seq):
    t = x2d.shape[0]
    tm = min(TM_PROJ, seq)
    spb = seq // tm
    row = lambda w: pl.BlockSpec((tm, w), lambda i: (i, 0))
    const = lambda s: pl.BlockSpec(s, lambda i: (0,) * len(s))
    return pl.pallas_call(
        _inproj_kernel,
        grid=(t // tm,),
        in_specs=[row(D_MODEL),
                  pl.BlockSpec((1, 6, D_MODEL), lambda i: (i // spb, 0, 0)),
                  const((1, D_MODEL)),
                  const((D_MODEL, D_IN)),
                  const((1, D_A)), const((1, D_A)),
                  pl.BlockSpec((tm, LANES), lambda i: (i % spb, 0)),
                  pl.BlockSpec((tm, LANES), lambda i: (i % spb, 0))],
        out_specs=[row(D_A), row(D_A), row(D_Q), row(D_KV), row(D_KV), row(D_MODEL), row(D_MODEL)],
        out_shape=[jax.ShapeDtypeStruct((t, w), BF16)
                   for w in (D_A, D_A, D_Q, D_KV, D_KV, D_MODEL, D_MODEL)],
        compiler_params=_params("arbitrary"),
        name="inproj",
    )(x2d, mods, g1, w_in, lng, lnb, cos, sin)


def _mix_kernel(sink_ref, gu_ref, vn_ref, q_ref, kp_ref, kc_ref, kn_ref, vp_ref, vc_ref, vx_ref,
                kctx_ref, vctx_ref, ws_ref, bs_ref, a_ref, o_ref):
    n = pl.program_id(1)
    nblk = pl.num_programs(1)

    for g in range(G_A):
        sl = slice(g * CHUNK, (g + 1) * CHUNK)
        s = _dot(ws_ref[g], vn_ref[:, sl]) + bs_ref[:, sl]
        a_ref[:, sl] = (gu_ref[:, sl].astype(F32) * s).astype(BF16)

    kcat = jnp.concatenate([kp_ref[...], kc_ref[...], kn_ref[...], kctx_ref[...]], axis=0)
    vcat = jnp.concatenate([vp_ref[...], vc_ref[...], vx_ref[...], vctx_ref[...]], axis=0)
    nwin = 3 * WBLK
    nkeys = kcat.shape[0]
    rows = REP * WBLK
    qi = lax.broadcasted_iota(jnp.int32, (rows, nkeys), 0) & (WBLK - 1)
    kj = lax.broadcasted_iota(jnp.int32, (rows, nkeys), 1)
    lo = jnp.where(n == 0, WBLK, 0)
    hi = jnp.where(n == nblk - 1, 2 * WBLK, nwin)
    valid = (kj >= nwin) | ((kj >= qi) & (kj <= qi + 2 * WBLK) & (kj >= lo) & (kj < hi))
    rgrp = lax.broadcasted_iota(jnp.int32, (rows, 1), 0) // WBLK

    for kvh in range(N_KV):
        ksl = kcat[:, kvh * HEAD_DIM:(kvh + 1) * HEAD_DIM]
        vsl = vcat[:, kvh * HEAD_DIM:(kvh + 1) * HEAD_DIM]
        qs = jnp.concatenate(
            [q_ref[:, (kvh * REP + r) * HEAD_DIM:(kvh * REP + r + 1) * HEAD_DIM] for r in range(REP)],
            axis=0)
        sink = jnp.zeros((rows, 1), F32)
        for r in range(REP):
            sink = jnp.where(rgrp == r, sink_ref[kvh * REP + r], sink)
        s = lax.dot_general(qs, ksl, (((1,), (1,)), ((), ())), preferred_element_type=F32)
        s = jnp.where(valid, s, NEG_BIG)
        m = jnp.maximum(jnp.max(s, axis=-1, keepdims=True), sink)
        p = jnp.exp(s - m)
        den = jnp.sum(p, axis=-1, keepdims=True) + jnp.exp(sink - m)
        o = _dot(p.astype(BF16), vsl) / den
        for r in range(0, REP, 2):
            pair = jnp.concatenate([o[r * WBLK:(r + 1) * WBLK], o[(r + 1) * WBLK:(r + 2) * WBLK]], axis=1)
            c0 = (kvh * REP + r) * HEAD_DIM
            o_ref[:, c0:c0 + 2 * HEAD_DIM] = pair.astype(BF16)


def _mix(sink, gu, vn, q, k, v, kctx, vctx, ws, bs_full, batch, seq):
    t = gu.shape[0]
    nblk = seq // WBLK
    nctx = kctx.shape[0] // batch
    cur = lambda w: pl.BlockSpec((WBLK, w), lambda b, n: (b * nblk + n, 0))
    prev = pl.BlockSpec((WBLK, D_KV), lambda b, n: (b * nblk + jnp.maximum(n - 1, 0), 0))
    nxt = pl.BlockSpec((WBLK, D_KV), lambda b, n: (b * nblk + jnp.minimum(n + 1, nblk - 1), 0))
    same = pl.BlockSpec((WBLK, D_KV), lambda b, n: (b * nblk + n, 0))
    cblk = pl.BlockSpec((nctx, D_KV), lambda b, n: (b, 0))
    return pl.pallas_call(
        _mix_kernel,
        grid=(batch, nblk),
        in_specs=[pl.BlockSpec(memory_space=pltpu.SMEM),
                  cur(D_A), cur(D_A), cur(D_Q),
                  prev, same, nxt, prev, same, nxt,
                  cblk, cblk,
                  pl.BlockSpec((G_A, CHUNK, CHUNK), lambda b, n: (0, 0, 0)),
                  pl.BlockSpec((CHUNK, D_A), lambda b, n: (0, 0))],
        out_specs=[cur(D_A), cur(D_Q)],
        out_shape=[jax.ShapeDtypeStruct((t, D_A), BF16), jax.ShapeDtypeStruct((t, D_Q), BF16)],
        compiler_params=_params("arbitrary", "arbitrary"),
        name="mix",
    )(sink, gu, vn, q, k, k, k, v, v, v, kctx, vctx, ws, bs_full)


def _store_row_tiles(ref, val):
    rows = val.shape[0]
    for s in range(ROW_TILES):
        ref[pl.ds(s, rows, stride=ROW_TILES), :] = val[:, s * LANES:(s + 1) * LANES]


def _load_row_tiles(ref, s):
    return ref[pl.ds(s, ref.shape[0] // ROW_TILES, stride=ROW_TILES), :]


def _merge_kernel(a_ref, o_ref, sga_ref, sgb_ref, x_ref, mod_ref, g2_ref, wa_ref, wb_ref, wo_ref,
                  rwh_ref, rwl_ref, rb_ref,
                  x1_ref, h2t_ref, h2b_ref, gate_ref, ci_ref, ck_ref, cum_ref, cnt_ref, base_ref):
    i = pl.program_id(0)

    @pl.when(i == 0)
    def _():
        base_ref[...] = jnp.zeros_like(base_ref)

    mod = mod_ref[0]
    ya = _dot(a_ref[...], wa_ref[...])
    yb = _dot(o_ref[...], wb_ref[...])
    y = sga_ref[...].astype(F32) * ya + sgb_ref[...].astype(F32) * yb
    x1 = x_ref[...] + mod[2:3] * _dot(y.astype(BF16), wo_ref[...])
    x1_ref[...] = x1
    h2 = _rms_mod(x1, g2_ref[...], mod[3:4], mod[4:5])
    h2b_ref[...] = h2.astype(BF16)
    _store_row_tiles(h2t_ref, h2)

    hh = h2.astype(BF16)
    hl = (h2 - hh.astype(F32)).astype(BF16)
    dn = (((1,), (1,)), ((), ()))
    logits = (lax.dot_general(rwh_ref[...], hh, dn, preferred_element_type=F32)
              + lax.dot_general(rwl_ref[...], hh, dn, preferred_element_type=F32)
              + lax.dot_general(rwh_ref[...], hl, dn, preferred_element_type=F32))
    scores = jax.nn.sigmoid(logits)
    tm = scores.shape[1]
    eio = lax.broadcasted_iota(jnp.int32, scores.shape, 0).astype(F32)
    work = scores + rb_ref[...]
    picked = jnp.zeros(scores.shape, F32)
    choice = jnp.zeros(scores.shape, F32)
    vals = []
    for k in range(TOP_K):
        m = jnp.max(work, axis=0, keepdims=True)
        ik = jnp.min(jnp.where(work == m, eio, float(N_EXPERTS)), axis=0, keepdims=True)
        oh = eio == ik
        vals.append(jnp.sum(jnp.where(oh, scores, 0.0), axis=0, keepdims=True))
        work = jnp.where(oh, -jnp.inf, work)
        picked = picked + oh.astype(F32)
        choice = choice + jnp.where(oh, float(k + 1), 0.0)
    total = vals[0]
    for vk in vals[1:]:
        total = total + vk
    gate_ref[...] = jnp.concatenate(vals, axis=0) * (ROUTE_SCALE / total)

    tr = lax.broadcasted_iota(jnp.int32, (tm, tm), 0)
    tc = lax.broadcasted_iota(jnp.int32, (tm, tm), 1)
    upto = (tr <= tc).astype(BF16)
    ci_ref[...] = _dot(picked.astype(BF16), upto).astype(BF16)
    ck_ref[...] = choice.astype(BF16)
    base = base_ref[...] + jnp.sum(picked, axis=1, keepdims=True)
    base_ref[...] = base
    cum_ref[0] = base
    cnt_ref[...] = base


def _merge(a, o, sga, sgb, x2d, mods, g2, wa, wb, wo, rwh, rwl, rb, seq):
    t = x2d.shape[0]
    tm = min(TM_MERGE, seq)
    spb = seq // tm
    row = lambda w: pl.BlockSpec((tm, w), lambda i: (i, 0))
    col = lambda: pl.BlockSpec((TOP_K, tm), lambda i: (0, i))
    ecol = lambda: pl.BlockSpec((N_EXPERTS, tm), lambda i: (0, i))
    const = lambda s: pl.BlockSpec(s, lambda i: (0,) * len(s))
    return pl.pallas_call(
        _merge_kernel,
        grid=(t // tm,),
        in_specs=[row(D_A), row(D_Q), row(D_MODEL), row(D_MODEL), row(D_MODEL),
                  pl.BlockSpec((1, 6, D_MODEL), lambda i: (i // spb, 0, 0)),
                  const((1, D_MODEL)),
                  const((D_A, D_MODEL)), const((D_Q, D_MODEL)), const((D_MODEL, D_MODEL)),
                  const((N_EXPERTS, D_MODEL)), const((N_EXPERTS, D_MODEL)), const((N_EXPERTS, 1))],
        out_specs=[row(D_MODEL), pl.BlockSpec((tm * ROW_TILES, LANES), lambda i: (i, 0)), row(D_MODEL),
                   col(), ecol(), ecol(),
                   pl.BlockSpec((1, N_EXPERTS, 1), lambda i: (i, 0, 0)), const((N_EXPERTS, 1))],
        out_shape=[jax.ShapeDtypeStruct((t, D_MODEL), F32),
                   jax.ShapeDtypeStruct((t * ROW_TILES, LANES), F32),
                   jax.ShapeDtypeStruct((t, D_MODEL), BF16),
                   jax.ShapeDtypeStruct((TOP_K, t), F32),
                   jax.ShapeDtypeStruct((N_EXPERTS, t), BF16),
                   jax.ShapeDtypeStruct((N_EXPERTS, t), BF16),
                   jax.ShapeDtypeStruct((t // tm, N_EXPERTS, 1), F32),
                   jax.ShapeDtypeStruct((N_EXPERTS, 1), F32)],
        scratch_shapes=[pltpu.VMEM((N_EXPERTS, 1), F32)],
        compiler_params=_params("arbitrary"),
        name="merge",
    )(a, o, sga, sgb, x2d, mods, g2, wa, wb, wo, rwh, rwl, rb)


def _row_copy(src, si, dst, di, sem):
    return pltpu.make_async_copy(src.at[pl.ds(pl.multiple_of(si * ROW_TILES, ROW_TILES), ROW_TILES), :],
                                 dst.at[pl.ds(pl.multiple_of(di * ROW_TILES, ROW_TILES), ROW_TILES), :], sem)


def _rows_wait(ref, nrows, sem):
    n = nrows * ROW_TILES
    pltpu.make_async_copy(ref.at[pl.ds(0, n), :], ref.at[pl.ds(0, n), :], sem).wait()


def _slots_kernel(ntok, bexp_ref, r0_ref, cnt_ref, *refs):
    ci_refs = refs[:IMAP_GROUP]
    ck_refs = refs[IMAP_GROUP:2 * IMAP_GROUP]
    cum_ref, tok_ref, sid_ref = refs[2 * IMAP_GROUP:]
    i = pl.program_id(0)
    nt, tm = ci_refs[0].shape[1:]
    lane = lax.broadcasted_iota(jnp.int32, (1, EXP_BLK), 1)
    tile = lax.broadcasted_iota(jnp.int32, (nt, EXP_BLK), 0).astype(F32)
    dn = (((0,), (0,)), ((), ()))
    for g in range(IMAP_GROUP):
        b = i * IMAP_GROUP + g
        r = (r0_ref[b] + lane).astype(F32)
        cum = cum_ref[g]
        jstar = jnp.sum((cum <= r).astype(F32), axis=0, keepdims=True)
        onehot = (tile == jstar).astype(BF16)
        before = jnp.sum(jnp.where(tile == jstar - 1.0, cum, 0.0), axis=0, keepdims=True)
        rin = r - before
        isel = lax.dot_general(ci_refs[g][0], onehot, dn, preferred_element_type=F32)
        ksel = lax.dot_general(ck_refs[g][0], onehot, dn, preferred_element_type=F32)
        tloc = jnp.sum((isel <= rin).astype(F32), axis=0, keepdims=True)
        kp1 = jnp.sum(jnp.where(isel == rin + 1.0, ksel, 0.0), axis=0, keepdims=True)
        tok = (jstar * tm + tloc).astype(jnp.int32)
        valid = (r0_ref[b] + lane) < cnt_ref[b]
        tok = jnp.where(valid, tok, 0)
        tok_ref[g] = tok
        sid_ref[g] = jnp.where(valid, (kp1.astype(jnp.int32) - 1) * ntok + tok, TOP_K * ntok + lane)


def _slots(bexp, r0, cntb, ci3, ck3, cumb, ntok):
    nblocks = bexp.shape[0]
    nt, tm = ci3.shape[1:]
    per = lambda g: pl.BlockSpec((1, nt, tm), lambda i, be, *_: (be[i * IMAP_GROUP + g], 0, 0))
    ids = lambda: pl.BlockSpec((IMAP_GROUP, 1, EXP_BLK), lambda i, *_: (i, 0, 0))
    grid_spec = pltpu.PrefetchScalarGridSpec(
        num_scalar_prefetch=3,
        grid=(nblocks // IMAP_GROUP,),
        in_specs=([per(g) for g in range(IMAP_GROUP)] + [per(g) for g in range(IMAP_GROUP)]
                  + [pl.BlockSpec((IMAP_GROUP, nt, 1), lambda i, *_: (i, 0, 0))]),
        out_specs=[ids(), ids()],
    )
    return pl.pallas_call(
        functools.partial(_slots_kernel, ntok),
        grid_spec=grid_spec,
        out_shape=[jax.ShapeDtypeStruct((nblocks, 1, EXP_BLK), jnp.int32)] * 2,
        compiler_params=_params("arbitrary"),
        name="slots",
    )(bexp, r0, cntb, *([ci3] * IMAP_GROUP), *([ck3] * IMAP_GROUP), cumb)


def _expert_kernel(bexp_ref, nval_ref, tok_ref, nxt_ref, sid_ref, h_ref, wg_ref, wu_ref, wd_ref, ys_ref,
                   xin_ref, yout_ref, xb_ref, gsem, ssem):
    b = pl.program_id(0)
    nv = nval_ref[0]
    slot = b % 2
    spare = ys_ref.shape[0] // ROW_TILES - EXP_BLK

    def gather(ids_ref, s):
        def body(r, carry):
            _row_copy(h_ref, ids_ref[0, 0, r], xin_ref.at[s], r, gsem.at[s]).start(priority=0)
            return carry
        lax.fori_loop(0, EXP_BLK, body, 0, unroll=8)

    def scatter(s):
        def body(r, carry):
            _row_copy(yout_ref.at[s], r, ys_ref, sid_ref[0, 0, r], ssem.at[s]).start(priority=1)
            return carry
        lax.fori_loop(0, EXP_BLK, body, 0, unroll=8)

    @pl.when(b == 0)
    def _():
        yout_ref[0] = jnp.zeros(yout_ref.shape[1:], F32)
        cp = pltpu.make_async_copy(
            yout_ref.at[0], ys_ref.at[pl.ds(spare * ROW_TILES, EXP_BLK * ROW_TILES), :], ssem.at[0])
        cp.start()
        cp.wait()
        gather(tok_ref, 0)

    @pl.when(b + 1 < nv)
    def _():
        gather(nxt_ref, 1 - slot)

    @pl.when(b < nv)
    def _():
        _rows_wait(xin_ref.at[slot], EXP_BLK, gsem.at[slot])
        for s in range(ROW_TILES):
            xb_ref[:, s * LANES:(s + 1) * LANES] = _load_row_tiles(xin_ref.at[slot], s).astype(BF16)
        x = xb_ref[...]
        g = _dot(x, wg_ref[0].astype(BF16))
        u = _dot(x, wu_ref[0].astype(BF16))
        act = (_silu(g) * u).astype(BF16)
        _store_row_tiles(yout_ref.at[slot], _dot(act, wd_ref[0].astype(BF16)))

        @pl.when(b >= 1)
        def _():
            _rows_wait(yout_ref.at[1 - slot], EXP_BLK, ssem.at[1 - slot])

        scatter(slot)

        @pl.when(b == nv - 1)
        def _():
            _rows_wait(yout_ref.at[slot], EXP_BLK, ssem.at[slot])


def _experts(bexp, nval, tok, sid, h2t, wg, wu, wd):
    nb = bexp.shape[0]
    ntok = h2t.shape[0] // ROW_TILES
    ids = lambda f: pl.BlockSpec((1, 1, EXP_BLK), f, memory_space=pltpu.SMEM)
    wspec = lambda s: pl.BlockSpec((1,) + s, lambda b, be, nv: (be[b], 0, 0))
    grid_spec = pltpu.PrefetchScalarGridSpec(
        num_scalar_prefetch=2,
        grid=(nb,),
        in_specs=[ids(lambda b, be, nv: (b, 0, 0)),
                  ids(lambda b, be, nv: (jnp.minimum(b + 1, nb - 1), 0, 0)),
                  ids(lambda b, be, nv: (b, 0, 0)),
                  pl.BlockSpec(memory_space=pl.ANY),
                  wspec((D_MODEL, D_EXPERT)), wspec((D_MODEL, D_EXPERT)), wspec((D_EXPERT, D_MODEL))],
        out_specs=pl.BlockSpec(memory_space=pl.ANY),
        scratch_shapes=[pltpu.VMEM((2, EXP_BLK * ROW_TILES, LANES), F32),
                        pltpu.VMEM((2, EXP_BLK * ROW_TILES, LANES), F32),
                        pltpu.VMEM((EXP_BLK, D_MODEL), BF16),
                        pltpu.SemaphoreType.DMA((2,)), pltpu.SemaphoreType.DMA((2,))],
    )
    nslots = TOP_K * ntok + EXP_BLK
    return pl.pallas_call(
        _expert_kernel,
        grid_spec=grid_spec,
        out_shape=jax.ShapeDtypeStruct((nslots * ROW_TILES, LANES), F32),
        compiler_params=_params("arbitrary"),
        name="experts",
    )(bexp, nval, tok, tok, sid, h2t, wg, wu, wd)


def _combine_kernel(*refs):
    ys_refs = refs[:TOP_K]
    gate_ref, x1_ref, h2_ref, mod_ref, sg_ref, su_ref, sd_ref, fg_ref, out_ref, x2_ref = refs[TOP_K:]
    tm = x1_ref.shape[0]

    mod = mod_ref[0]
    h = h2_ref[...]
    act = (_silu(_dot(h, sg_ref[...])) * _dot(h, su_ref[...])).astype(BF16)
    moe = _dot(act, sd_ref[...])

    g = gate_ref[...]
    g0 = g.astype(BF16)
    r1 = g - g0.astype(F32)
    g1 = r1.astype(BF16)
    g2 = (r1 - g1.astype(F32)).astype(BF16)
    eye = (lax.broadcasted_iota(jnp.int32, (tm, tm), 0)
           == lax.broadcasted_iota(jnp.int32, (tm, tm), 1)).astype(BF16)
    dn = (((1,), (1,)), ((), ()))
    gcol = (lax.dot_general(eye, g0, dn, preferred_element_type=F32)
            + lax.dot_general(eye, g1, dn, preferred_element_type=F32)
            + lax.dot_general(eye, g2, dn, preferred_element_type=F32))

    gk = [jnp.broadcast_to(gcol[:, k:k + 1], (tm, LANES)) for k in range(TOP_K)]
    ssq = jnp.zeros((tm, 1), F32)
    for s in range(ROW_TILES):
        sl = slice(s * LANES, (s + 1) * LANES)
        m = moe[:, sl]
        for k in range(TOP_K):
            m = m + gk[k] * _load_row_tiles(ys_refs[k], s)
        x2 = x1_ref[:, sl] + mod[5:6, sl] * m
        x2_ref[:, sl] = x2
        ssq = ssq + jnp.sum(x2 * x2, axis=-1, keepdims=True)
    out_ref[...] = x2_ref[...] * lax.rsqrt(ssq * (1.0 / D_MODEL) + EPS) * fg_ref[...]


def _combine(ys, gate, x1, h2, mods, sg, su, sd, fg, seq):
    t = x1.shape[0]
    tm = min(TM_DMA, seq)
    spb = seq // tm
    nt = t // tm
    row = lambda: pl.BlockSpec((tm, D_MODEL), lambda i: (i, 0))
    const = lambda s: pl.BlockSpec(s, lambda i: (0,) * len(s))
    choice = lambda k: pl.BlockSpec((tm * ROW_TILES, LANES), lambda i: (k * nt + i, 0))
    return pl.pallas_call(
        _combine_kernel,
        grid=(nt,),
        in_specs=[choice(k) for k in range(TOP_K)] + [
            pl.BlockSpec((TOP_K, tm), lambda i: (0, i)),
            row(), row(),
            pl.BlockSpec((1, 6, D_MODEL), lambda i: (i // spb, 0, 0)),
            const((D_MODEL, D_SHARED)), const((D_MODEL, D_SHARED)), const((D_SHARED, D_MODEL)),
            const((1, D_MODEL))],
        out_specs=row(),
        out_shape=jax.ShapeDtypeStruct((t, D_MODEL), F32),
        scratch_shapes=[pltpu.VMEM((tm, D_MODEL), F32)],
        compiler_params=_params("arbitrary"),
        name="combine",
    )(*([ys] * TOP_K), gate, x1, h2, mods, sg, su, sd, fg)


def _rope_tables(seq):
    pos = jnp.arange(seq)
    n_freq = HEAD_DIM // 4
    inv = ROPE_THETA ** (-jnp.arange(n_freq, dtype=F32) / n_freq)
    ang_r = (pos // GRID_W)[:, None].astype(F32) * inv
    ang_c = (pos % GRID_W)[:, None].astype(F32) * inv
    cr, sr, cc, sc = jnp.cos(ang_r), jnp.sin(ang_r), jnp.cos(ang_c), jnp.sin(ang_c)
    cos = jnp.concatenate([cr, cr, cc, cc], axis=1)
    sin = jnp.concatenate([-sr, sr, -sc, sc], axis=1)
    reps = LANES // HEAD_DIM
    return jnp.tile(cos, (1, reps)), jnp.tile(sin, (1, reps))


def _layer(x2d, ctx2d, mods, batch, seq, norm1_g, norm2_g, w_in, ln_g, ln_b, gmlp_ws, gmlp_bs, sink,
           w_a, w_b, w_o, router_w, router_b, e_gate, e_up, e_down, s_gate, s_up, s_down, final_g):
    t = x2d.shape[0]
    g1 = norm1_g.reshape(1, D_MODEL)
    w_in_b = w_in.astype(BF16)
    cos, sin = _rope_tables(seq)

    kctx, vctx = _ctx_kv(ctx2d, mods, g1, w_in_b[:, C_K:C_GA])
    gu, vn, q, k, v, sga, sgb = _inproj(x2d, mods, g1, w_in_b, ln_g.reshape(1, D_A), ln_b.reshape(1, D_A),
                                        cos, sin, seq)

    bs_full = jnp.repeat(gmlp_bs.T, D_A // G_A, axis=1)
    a, o = _mix(sink, gu, vn, q, k, v, kctx, vctx, gmlp_ws.astype(BF16), bs_full, batch, seq)

    rwt = router_w.T
    rwh = rwt.astype(BF16)
    rwl = (rwt - rwh.astype(F32)).astype(BF16)
    x1, h2t, h2b, gate, ci, ck, cum, counts = _merge(
        a, o, sga, sgb, x2d, mods, norm2_g.reshape(1, D_MODEL),
        w_a.astype(BF16), w_b.astype(BF16), w_o.astype(BF16), rwh, rwl, router_b.reshape(N_EXPERTS, 1), seq)

    cnt = counts[:, 0].astype(jnp.int32)
    pcnt = (cnt + EXP_BLK - 1) // EXP_BLK * EXP_BLK
    pend = jnp.cumsum(pcnt)
    pstart = pend - pcnt
    nblocks = (t * TOP_K) // EXP_BLK + N_EXPERTS
    nval = (pend[-1] // EXP_BLK).astype(jnp.int32)
    bsrc = jnp.minimum(jnp.arange(nblocks, dtype=jnp.int32), nval - 1)
    bexp = jnp.sum((pend[None, :] <= (bsrc * EXP_BLK)[:, None]).astype(jnp.int32), axis=1)
    bexp = jnp.minimum(bexp, N_EXPERTS - 1)
    r0 = bsrc * EXP_BLK - jnp.take(pstart, bexp)
    cntb = jnp.take(cnt, bexp)
    ntile = cum.shape[0]
    cumb = jnp.take(cum[:, :, 0], bexp, axis=1).T[:, :, None]

    tok, sid = _slots(bexp, r0, cntb, ci.reshape(N_EXPERTS, ntile, -1), ck.reshape(N_EXPERTS, ntile, -1),
                      cumb, t)
    ys = _experts(bexp, nval.reshape(1), tok, sid, h2t, e_gate, e_up, e_down)
    return _combine(ys, gate, x1, h2b, mods, s_gate.astype(BF16), s_up.astype(BF16),
                    s_down.astype(BF16), final_g.reshape(1, D_MODEL), seq)


def kernel(x, c, ctx, c_ctx, ada_w, ada_b, norm1_g, norm2_g, w_in, gmlp_ln_g, gmlp_ln_b, gmlp_ws, gmlp_bs,
           attn_sink, w_branch_a, w_branch_b, w_out, router_w, router_b, exp_w_gate, exp_w_up, exp_w_down,
           sh_w_gate, sh_w_up, sh_w_down, final_g):
    batch, seq, _ = x.shape
    depth = ada_w.shape[0]
    assert depth == 1, "the context stream is only carried as keys/values of a single layer"
    assert batch + 1 <= 8 and seq % WBLK == 0
    cond = jnp.concatenate([c, c_ctx[None], jnp.zeros((8 - batch - 1, D_MODEL), F32)], axis=0)
    mods = _ada(cond, ada_w[0], ada_b[0])[:batch + 1].reshape(batch + 1, 6, D_MODEL)
    out = _layer(x.reshape(batch * seq, D_MODEL), ctx.reshape(-1, D_MODEL), mods, batch, seq,
                 norm1_g[0], norm2_g[0], w_in[0], gmlp_ln_g[0], gmlp_ln_b[0], gmlp_ws[0], gmlp_bs[0],
                 attn_sink[0], w_branch_a[0], w_branch_b[0], w_out[0], router_w[0], router_b[0],
                 exp_w_gate[0], exp_w_up[0], exp_w_down[0], sh_w_gate[0], sh_w_up[0], sh_w_down[0], final_g)
    return out.reshape(batch, seq, D_MODEL)
```

```python
import functools

import jax
import jax.numpy as jnp
from jax import lax
from jax.experimental import pallas as pl
from jax.experimental.pallas import tpu as pltpu

F32 = jnp.float32
BF16 = jnp.bfloat16

D_MODEL = 1024
EPS = 1e-6
GRID_W = 64
D_A = D_MODEL // 2
G_A = 4
CHUNK = 128
N_HEADS = 8
N_KV = 2
REP = N_HEADS // N_KV
HEAD_DIM = 64
D_Q = N_HEADS * HEAD_DIM
D_KV = N_KV * HEAD_DIM
WBLK = 128
ROPE_THETA = 10000.0
N_EXPERTS = 256
TOP_K = 8
D_EXPERT = D_MODEL // 4
D_SHARED = D_MODEL // 4
ROUTE_SCALE = 2.5

C_U = 0
C_V = D_A
C_Q = 2 * D_A
C_K = C_Q + D_Q
C_VAL = C_K + D_KV
C_GA = C_VAL + D_KV
C_GB = C_GA + D_MODEL
D_IN = C_GB + D_MODEL

LANES = 128
ROPE_HALF = HEAD_DIM // 4
NEG_BIG = -1e30

TM_PROJ = 512
TM_MERGE = 256
TM_DMA = 256
EXP_BLK = 256
W_SLOTS = 3
SUBLANES = 8
ROW_TILES = D_MODEL // LANES
assert ROW_TILES == SUBLANES
VMEM_LIMIT = 56 * 1024 * 1024


def _gelu(x):
    return 0.5 * x * (1.0 + jnp.tanh(0.7978845608028654 * (x + 0.044715 * x * x * x)))


def _silu(x):
    return x * jax.nn.sigmoid(x)


def _dot(a, b):
    return jnp.dot(a, b, preferred_element_type=F32)


def _rms_mod(x, g, shift, scale):
    ms = jnp.mean(x * x, axis=-1, keepdims=True)
    return (x * lax.rsqrt(ms + EPS)) * g * (1.0 + scale) + shift


def _params(*sem):
    return pltpu.CompilerParams(dimension_semantics=sem, vmem_limit_bytes=VMEM_LIMIT)


def _ada_kernel(c_ref, w_ref, b_ref, o_ref):
    c = c_ref[...]
    s = _silu(c).astype(BF16)
    o_ref[...] = _dot(s, w_ref[...].astype(BF16)) + b_ref[...]


def _ada(cond8, ada_w, ada_b):
    n = ada_w.shape[1]
    tn = 1536
    return pl.pallas_call(
        _ada_kernel,
        grid=(n // tn,),
        in_specs=[pl.BlockSpec((8, D_MODEL), lambda j: (0, 0)),
                  pl.BlockSpec((D_MODEL, tn), lambda j: (0, j)),
                  pl.BlockSpec((1, tn), lambda j: (0, j))],
        out_specs=pl.BlockSpec((8, tn), lambda j: (0, j)),
        out_shape=jax.ShapeDtypeStruct((8, n), F32),
        compiler_params=_params("arbitrary"),
        name="ada",
    )(cond8, ada_w, ada_b.reshape(1, n))


def _ctx_kernel(x_ref, mod_ref, g_ref, w_ref, k_ref, v_ref):
    mod = mod_ref[0]
    h = _rms_mod(x_ref[...], g_ref[...], mod[0:1], mod[1:2]).astype(BF16)
    z = _dot(h, w_ref[...])
    k_ref[...] = z[:, :D_KV].astype(BF16)
    v_ref[...] = z[:, D_KV:].astype(BF16)


def _ctx_kv(ctx2d, mods, g1, w_kv):
    n = ctx2d.shape[0]
    nb = mods.shape[0] - 1
    return pl.pallas_call(
        _ctx_kernel,
        grid=(1,),
        in_specs=[pl.BlockSpec((n, D_MODEL), lambda i: (0, 0)),
                  pl.BlockSpec((1, 6, D_MODEL), lambda i: (nb, 0, 0)),
                  pl.BlockSpec((1, D_MODEL), lambda i: (0, 0)),
                  pl.BlockSpec((D_MODEL, 2 * D_KV), lambda i: (0, 0))],
        out_specs=[pl.BlockSpec((n, D_KV), lambda i: (0, 0)),
                   pl.BlockSpec((n, D_KV), lambda i: (0, 0))],
        out_shape=[jax.ShapeDtypeStruct((n, D_KV), BF16)] * 2,
        compiler_params=_params("arbitrary"),
        name="ctx_kv",
    )(ctx2d, mods, g1, w_kv)


def _rope(t, cos, sin):
    lane = lax.broadcasted_iota(jnp.int32, (t.shape[0], LANES), 1)
    first = (lane & (2 * ROPE_HALF - 1)) < ROPE_HALF
    outs = []
    for j in range(t.shape[1] // LANES):
        tj = t[:, j * LANES:(j + 1) * LANES]
        up = pltpu.roll(tj, LANES - ROPE_HALF, 1)
        dn = pltpu.roll(tj, ROPE_HALF, 1)
        outs.append(tj * cos + jnp.where(first, up, dn) * sin)
    return outs


def _inproj_kernel(x_ref, mod_ref, g_ref, w_ref, lng_ref, lnb_ref, cos_ref, sin_ref,
                   gu_ref, vn_ref, q_ref, k_ref, v_ref, sga_ref, sgb_ref):
    mod = mod_ref[0]
    h = _rms_mod(x_ref[...], g_ref[...], mod[0:1], mod[1:2]).astype(BF16)

    def proj(lo, hi):
        return _dot(h, w_ref[:, lo:hi])

    gu_ref[...] = _gelu(proj(C_U, C_V)).astype(BF16)

    v = _gelu(proj(C_V, C_Q))
    mu = jnp.mean(v, axis=-1, keepdims=True)
    vc = v - mu
    var = jnp.mean(vc * vc, axis=-1, keepdims=True)
    vn_ref[...] = (vc * lax.rsqrt(var + EPS) * lng_ref[...] + lnb_ref[...]).astype(BF16)

    cos = cos_ref[...]
    sin = sin_ref[...]
    q = _rope(proj(C_Q, C_K) * (HEAD_DIM ** -0.5), cos, sin)
    for j, qj in enumerate(q):
        q_ref[:, j * LANES:(j + 1) * LANES] = qj.astype(BF16)
    k = _rope(proj(C_K, C_VAL), cos, sin)
    k_ref[...] = k[0].astype(BF16)
    v_ref[...] = proj(C_VAL, C_GA).astype(BF16)
    sga_ref[...] = jax.nn.sigmoid(proj(C_GA, C_GB)).astype(BF16)
    sgb_ref[...] = jax.nn.sigmoid(proj(C_GB, D_IN)).astype(BF16)


def _inproj(x2d, mods, g1, w_in, lng, lnb, cos, sin, seq):
    t = x2d.shape[0]
    tm = min(TM_PROJ, seq)
    spb = seq // tm
    row = lambda w: pl.BlockSpec((tm, w), lambda i: (i, 0))
    const = lambda s: pl.BlockSpec(s, lambda i: (0,) * len(s))
    return pl.pallas_call(
        _inproj_kernel,
        grid=(t // tm,),
        in_specs=[row(D_MODEL),
                  pl.BlockSpec((1, 6, D_MODEL), lambda i: (i // spb, 0, 0)),
                  const((1, D_MODEL)),
                  const((D_MODEL, D_IN)),
                  const((1, D_A)), const((1, D_A)),
                  pl.BlockSpec((tm, LANES), lambda i: (i % spb, 0)),
                  pl.BlockSpec((tm, LANES), lambda i: (i % spb, 0))],
        out_specs=[row(D_A), row(D_A), row(D_Q), row(D_KV), row(D_KV), row(D_MODEL), row(D_MODEL)],
        out_shape=[jax.ShapeDtypeStruct((t, w), BF16)
                   for w in (D_A, D_A, D_Q, D_KV, D_KV, D_MODEL, D_MODEL)],
        compiler_params=_params("arbitrary"),
        name="inproj",
    )(x2d, mods, g1, w_in, lng, lnb, cos, sin)


def _mix_kernel(sink_ref, gu_ref, vn_ref, q_ref, kp_ref, kc_ref, kn_ref, vp_ref, vc_ref, vx_ref,
                kctx_ref, vctx_ref, ws_ref, bs_ref, a_ref, o_ref):
    n = pl.program_id(1)
    nblk = pl.num_programs(1)

    for g in range(G_A):
        sl = slice(g * CHUNK, (g + 1) * CHUNK)
        s = _dot(ws_ref[g], vn_ref[:, sl]) + bs_ref[:, sl]
        a_ref[:, sl] = (gu_ref[:, sl].astype(F32) * s).astype(BF16)

    kcat = jnp.concatenate([kp_ref[...], kc_ref[...], kn_ref[...], kctx_ref[...]], axis=0)
    vcat = jnp.concatenate([vp_ref[...], vc_ref[...], vx_ref[...], vctx_ref[...]], axis=0)
    nwin = 3 * WBLK
    nkeys = kcat.shape[0]
    rows = REP * WBLK
    qi = lax.broadcasted_iota(jnp.int32, (rows, nkeys), 0) & (WBLK - 1)
    kj = lax.broadcasted_iota(jnp.int32, (rows, nkeys), 1)
    lo = jnp.where(n == 0, WBLK, 0)
    hi = jnp.where(n == nblk - 1, 2 * WBLK, nwin)
    valid = (kj >= nwin) | ((kj >= qi) & (kj <= qi + 2 * WBLK) & (kj >= lo) & (kj < hi))
    rgrp = lax.broadcasted_iota(jnp.int32, (rows, 1), 0) // WBLK

    for kvh in range(N_KV):
        ksl = kcat[:, kvh * HEAD_DIM:(kvh + 1) * HEAD_DIM]
        vsl = vcat[:, kvh * HEAD_DIM:(kvh + 1) * HEAD_DIM]
        qs = jnp.concatenate(
            [q_ref[:, (kvh * REP + r) * HEAD_DIM:(kvh * REP + r + 1) * HEAD_DIM] for r in range(REP)],
            axis=0)
        sink = jnp.zeros((rows, 1), F32)
        for r in range(REP):
            sink = jnp.where(rgrp == r, sink_ref[kvh * REP + r], sink)
        s = lax.dot_general(qs, ksl, (((1,), (1,)), ((), ())), preferred_element_type=F32)
        s = jnp.where(valid, s, NEG_BIG)
        m = jnp.maximum(jnp.max(s, axis=-1, keepdims=True), sink)
        p = jnp.exp(s - m)
        den = jnp.sum(p, axis=-1, keepdims=True) + jnp.exp(sink - m)
        o = _dot(p.astype(BF16), vsl) / den
        for r in range(0, REP, 2):
            pair = jnp.concatenate([o[r * WBLK:(r + 1) * WBLK], o[(r + 1) * WBLK:(r + 2) * WBLK]], axis=1)
            c0 = (kvh * REP + r) * HEAD_DIM
            o_ref[:, c0:c0 + 2 * HEAD_DIM] = pair.astype(BF16)


def _mix(sink, gu, vn, q, k, v, kctx, vctx, ws, bs_full, batch, seq):
    t = gu.shape[0]
    nblk = seq // WBLK
    nctx = kctx.shape[0] // batch
    cur = lambda w: pl.BlockSpec((WBLK, w), lambda b, n: (b * nblk + n, 0))
    prev = pl.BlockSpec((WBLK, D_KV), lambda b, n: (b * nblk + jnp.maximum(n - 1, 0), 0))
    nxt = pl.BlockSpec((WBLK, D_KV), lambda b, n: (b * nblk + jnp.minimum(n + 1, nblk - 1), 0))
    same = pl.BlockSpec((WBLK, D_KV), lambda b, n: (b * nblk + n, 0))
    cblk = pl.BlockSpec((nctx, D_KV), lambda b, n: (b, 0))
    return pl.pallas_call(
        _mix_kernel,
        grid=(batch, nblk),
        in_specs=[pl.BlockSpec(memory_space=pltpu.SMEM),
                  cur(D_A), cur(D_A), cur(D_Q),
                  prev, same, nxt, prev, same, nxt,
                  cblk, cblk,
                  pl.BlockSpec((G_A, CHUNK, CHUNK), lambda b, n: (0, 0, 0)),
                  pl.BlockSpec((CHUNK, D_A), lambda b, n: (0, 0))],
        out_specs=[cur(D_A), cur(D_Q)],
        out_shape=[jax.ShapeDtypeStruct((t, D_A), BF16), jax.ShapeDtypeStruct((t, D_Q), BF16)],
        compiler_params=_params("arbitrary", "arbitrary"),
        name="mix",
    )(sink, gu, vn, q, k, k, k, v, v, v, kctx, vctx, ws, bs_full)


def _store_row_tiles(ref, val):
    rows = val.shape[0]
    for s in range(ROW_TILES):
        ref[pl.ds(s, rows, stride=ROW_TILES), :] = val[:, s * LANES:(s + 1) * LANES]


def _load_row_tiles(ref, s):
    return ref[pl.ds(s, ref.shape[0] // ROW_TILES, stride=ROW_TILES), :]


def _merge_kernel(a_ref, o_ref, sga_ref, sgb_ref, x_ref, mod_ref, g2_ref, wa_ref, wb_ref, wo_ref,
                  rwh_ref, rwl_ref, rb_ref,
                  x1_ref, h2t_ref, h2b_ref, idx_ref, gate_ref, rank_ref, cnt_ref, base_ref):
    i = pl.program_id(0)

    @pl.when(i == 0)
    def _():
        base_ref[...] = jnp.zeros_like(base_ref)

    mod = mod_ref[0]
    ya = _dot(a_ref[...], wa_ref[...])
    yb = _dot(o_ref[...], wb_ref[...])
    y = sga_ref[...].astype(F32) * ya + sgb_ref[...].astype(F32) * yb
    x1 = x_ref[...] + mod[2:3] * _dot(y.astype(BF16), wo_ref[...])
    x1_ref[...] = x1
    h2 = _rms_mod(x1, g2_ref[...], mod[3:4], mod[4:5])
    h2b_ref[...] = h2.astype(BF16)
    _store_row_tiles(h2t_ref, h2)

    hh = h2.astype(BF16)
    hl = (h2 - hh.astype(F32)).astype(BF16)
    dn = (((1,), (1,)), ((), ()))
    logits = (lax.dot_general(rwh_ref[...], hh, dn, preferred_element_type=F32)
              + lax.dot_general(rwl_ref[...], hh, dn, preferred_element_type=F32)
              + lax.dot_general(rwh_ref[...], hl, dn, preferred_element_type=F32))
    scores = jax.nn.sigmoid(logits)
    tm = scores.shape[1]
    eio = lax.broadcasted_iota(jnp.int32, scores.shape, 0).astype(F32)
    work = scores + rb_ref[...]
    picked = jnp.zeros(scores.shape, F32)
    idxs, vals = [], []
    for _ in range(TOP_K):
        m = jnp.max(work, axis=0, keepdims=True)
        ik = jnp.min(jnp.where(work == m, eio, float(N_EXPERTS)), axis=0, keepdims=True)
        oh = eio == ik
        vals.append(jnp.sum(jnp.where(oh, scores, 0.0), axis=0, keepdims=True))
        idxs.append(ik)
        work = jnp.where(oh, -jnp.inf, work)
        picked = picked + oh.astype(F32)
    total = vals[0]
    for vk in vals[1:]:
        total = total + vk

    tr = lax.broadcasted_iota(jnp.int32, (tm, tm), 0)
    tc = lax.broadcasted_iota(jnp.int32, (tm, tm), 1)
    before = (tr < tc).astype(BF16)
    base = base_ref[...]
    rank_e = _dot(picked.astype(BF16), before) + base
    ranks = [jnp.sum(jnp.where(eio == ik, rank_e, 0.0), axis=0, keepdims=True) for ik in idxs]
    base = base + jnp.sum(picked, axis=1, keepdims=True)
    base_ref[...] = base
    cnt_ref[...] = base

    idx_ref[...] = jnp.concatenate(idxs, axis=0).astype(jnp.int32)
    rank_ref[...] = jnp.concatenate(ranks, axis=0).astype(jnp.int32)
    gate_ref[...] = jnp.concatenate(vals, axis=0) * (ROUTE_SCALE / total)


def _merge(a, o, sga, sgb, x2d, mods, g2, wa, wb, wo, rwh, rwl, rb, seq):
    t = x2d.shape[0]
    tm = min(TM_MERGE, seq)
    spb = seq // tm
    row = lambda w: pl.BlockSpec((tm, w), lambda i: (i, 0))
    col = lambda: pl.BlockSpec((TOP_K, tm), lambda i: (0, i))
    const = lambda s: pl.BlockSpec(s, lambda i: (0,) * len(s))
    return pl.pallas_call(
        _merge_kernel,
        grid=(t // tm,),
        in_specs=[row(D_A), row(D_Q), row(D_MODEL), row(D_MODEL), row(D_MODEL),
                  pl.BlockSpec((1, 6, D_MODEL), lambda i: (i // spb, 0, 0)),
                  const((1, D_MODEL)),
                  const((D_A, D_MODEL)), const((D_Q, D_MODEL)), const((D_MODEL, D_MODEL)),
                  const((N_EXPERTS, D_MODEL)), const((N_EXPERTS, D_MODEL)), const((N_EXPERTS, 1))],
        out_specs=[row(D_MODEL), pl.BlockSpec((tm * ROW_TILES, LANES), lambda i: (i, 0)), row(D_MODEL),
                   col(), col(), col(), const((N_EXPERTS, 1))],
        out_shape=[jax.ShapeDtypeStruct((t, D_MODEL), F32),
                   jax.ShapeDtypeStruct((t * ROW_TILES, LANES), F32),
                   jax.ShapeDtypeStruct((t, D_MODEL), BF16),
                   jax.ShapeDtypeStruct((TOP_K, t), jnp.int32),
                   jax.ShapeDtypeStruct((TOP_K, t), F32),
                   jax.ShapeDtypeStruct((TOP_K, t), jnp.int32),
                   jax.ShapeDtypeStruct((N_EXPERTS, 1), F32)],
        scratch_shapes=[pltpu.VMEM((N_EXPERTS, 1), F32)],
        compiler_params=_params("arbitrary"),
        name="merge",
    )(a, o, sga, sgb, x2d, mods, g2, wa, wb, wo, rwh, rwl, rb)


def _row_copy(src, si, dst, di, sem):
    return pltpu.make_async_copy(src.at[pl.ds(pl.multiple_of(si * ROW_TILES, ROW_TILES), ROW_TILES), :],
                                 dst.at[pl.ds(pl.multiple_of(di * ROW_TILES, ROW_TILES), ROW_TILES), :], sem)


def _rows_wait(ref, nrows, sem):
    n = nrows * ROW_TILES
    pltpu.make_async_copy(ref.at[pl.ds(0, n), :], ref.at[pl.ds(0, n), :], sem).wait()


def _dispatch_kernel(nsteps, pad0_ref, padn_ref, nval_ref, dest_ref, h_ref, xs_ref, zero_ref, sem, zsem):
    i = pl.program_id(0)
    tm = dest_ref.shape[1]
    nblocks = xs_ref.shape[0] // (EXP_BLK * ROW_TILES)
    experts_per_step = -(-N_EXPERTS // nsteps)
    tail_per_step = -(-nblocks // nsteps)

    @pl.when(i == 0)
    def _():
        zero_ref[...] = jnp.zeros_like(zero_ref)

    def body(j, carry):
        for k in range(TOP_K):
            _row_copy(h_ref, j, xs_ref, dest_ref[k, j], sem).start(priority=k % 2)
        return carry

    lax.fori_loop(0, tm, body, 0)

    def zero_fill(act):
        def pad_body(r, carry):
            e = i * experts_per_step + r

            @pl.when(e < N_EXPERTS)
            def _():
                first = pad0_ref[e]
                n = padn_ref[e]
                bit = EXP_BLK // 2
                while bit:
                    off = first + (n & ~(2 * bit - 1))

                    @pl.when((n & bit) != 0)
                    def _(bit=bit, off=off):
                        act(pltpu.make_async_copy(
                            zero_ref.at[pl.ds(0, bit * ROW_TILES), :],
                            xs_ref.at[pl.ds(pl.multiple_of(off * ROW_TILES, ROW_TILES), bit * ROW_TILES), :], zsem))
                    bit //= 2
            return carry

        lax.fori_loop(0, experts_per_step, pad_body, 0)

        def tail_body(r, carry):
            blk = nval_ref[0] + i * tail_per_step + r

            @pl.when(blk < nblocks)
            def _():
                rows = EXP_BLK * ROW_TILES
                act(pltpu.make_async_copy(zero_ref, xs_ref.at[pl.ds(pl.multiple_of(blk * rows, rows), rows), :],
                                          zsem))
            return carry

        lax.fori_loop(0, tail_per_step, tail_body, 0)

    zero_fill(lambda cp: cp.start())
    _rows_wait(xs_ref, tm * TOP_K, sem)
    zero_fill(lambda cp: cp.wait())


def _dispatch(pad0, padn, nval, dest, h2t, nrows):
    t = h2t.shape[0] // ROW_TILES
    tm = min(TM_DMA, t)
    nsteps = t // tm
    grid_spec = pltpu.PrefetchScalarGridSpec(
        num_scalar_prefetch=3,
        grid=(nsteps,),
        in_specs=[pl.BlockSpec((TOP_K, tm), lambda i, *_: (0, i), memory_space=pltpu.SMEM),
                  pl.BlockSpec((tm * ROW_TILES, LANES), lambda i, *_: (i, 0))],
        out_specs=pl.BlockSpec(memory_space=pl.ANY),
        scratch_shapes=[pltpu.VMEM((EXP_BLK * ROW_TILES, LANES), F32),
                        pltpu.SemaphoreType.DMA(()), pltpu.SemaphoreType.DMA(())],
    )
    return pl.pallas_call(
        functools.partial(_dispatch_kernel, nsteps),
        grid_spec=grid_spec,
        out_shape=jax.ShapeDtypeStruct((nrows * ROW_TILES, LANES), F32),
        compiler_params=_params("arbitrary"),
        name="dispatch",
    )(pad0, padn, nval, dest, h2t)


def _expert_kernel(bsrc_ref, nval_ref, first_ref, run_ref, rexp_ref, nruns_ref,
                   xs_ref, wg_hbm, wu_hbm, wd_hbm, ys_ref,
                   xb_ref, wg_f, wu_f, wd_f, wg_b, wu_b, wd_b, wsem):
    b = pl.program_id(0)
    nruns = nruns_ref[0]

    def weights(j, act):
        e = rexp_ref[j]
        s = j % W_SLOTS
        for hbm, buf in ((wg_hbm, wg_f), (wu_hbm, wu_f), (wd_hbm, wd_f)):
            act(pltpu.make_async_copy(hbm.at[e], buf.at[s], wsem.at[s]))

    @pl.when(b == 0)
    def _():
        weights(0, lambda cp: cp.start())

        @pl.when(nruns > 1)
        def _():
            weights(1, lambda cp: cp.start())

    @pl.when(b < nval_ref[0])
    def _():
        j = run_ref[b]

        @pl.when(first_ref[b] == 1)
        def _():
            weights(j, lambda cp: cp.wait())

            @pl.when(j + 2 < nruns)
            def _():
                weights(j + 2, lambda cp: cp.start())

            s = j % W_SLOTS
            wg_b[...] = wg_f[s].astype(BF16)
            wu_b[...] = wu_f[s].astype(BF16)
            wd_b[...] = wd_f[s].astype(BF16)

        for s in range(ROW_TILES):
            xb_ref[:, s * LANES:(s + 1) * LANES] = _load_row_tiles(xs_ref, s).astype(BF16)
        x = xb_ref[...]
        g = _dot(x, wg_b[...])
        u = _dot(x, wu_b[...])
        act = (_silu(g) * u).astype(BF16)
        _store_row_tiles(ys_ref, _dot(act, wd_b[...]))

    @pl.when(b >= nval_ref[0])
    def _():
        ys_ref[...] = jnp.zeros_like(ys_ref)


def _experts(bexp, bsrc, nval, xs, wg, wu, wd):
    nrows = xs.shape[0] // ROW_TILES
    nb = nrows // EXP_BLK
    blk = (EXP_BLK * ROW_TILES, LANES)
    blocks = jnp.arange(nb, dtype=jnp.int32)
    first = ((blocks == 0) | (bexp != jnp.roll(bexp, 1))).astype(jnp.int32)
    run = jnp.cumsum(first) - 1
    runs = jnp.arange(N_EXPERTS, dtype=jnp.int32)
    rexp = jnp.sum(jnp.where((first[None, :] == 1) & (run[None, :] == runs[:, None]), bexp[None, :], 0), axis=1)
    nruns = (run[nb - 1] + 1).reshape(1)
    grid_spec = pltpu.PrefetchScalarGridSpec(
        num_scalar_prefetch=6,
        grid=(nb,),
        in_specs=[pl.BlockSpec(blk, lambda b, bs, *_: (bs[b], 0)),
                  pl.BlockSpec(memory_space=pl.ANY), pl.BlockSpec(memory_space=pl.ANY),
                  pl.BlockSpec(memory_space=pl.ANY)],
        out_specs=pl.BlockSpec(blk, lambda b, *_: (b, 0)),
        scratch_shapes=[pltpu.VMEM((EXP_BLK, D_MODEL), BF16),
                        pltpu.VMEM((W_SLOTS, D_MODEL, D_EXPERT), F32),
                        pltpu.VMEM((W_SLOTS, D_MODEL, D_EXPERT), F32),
                        pltpu.VMEM((W_SLOTS, D_EXPERT, D_MODEL), F32),
                        pltpu.VMEM((D_MODEL, D_EXPERT), BF16),
                        pltpu.VMEM((D_MODEL, D_EXPERT), BF16),
                        pltpu.VMEM((D_EXPERT, D_MODEL), BF16),
                        pltpu.SemaphoreType.DMA((W_SLOTS,))],
    )
    return pl.pallas_call(
        _expert_kernel,
        grid_spec=grid_spec,
        out_shape=jax.ShapeDtypeStruct((nrows * ROW_TILES, LANES), F32),
        compiler_params=_params("arbitrary"),
        name="experts",
    )(bsrc, nval, first, run, rexp, nruns, xs, wg, wu, wd)


def _combine_kernel(dest_ref, ys_ref, gate_ref, x1_ref, h2_ref, mod_ref, sg_ref, su_ref, sd_ref, fg_ref,
                    out_ref, buf_ref, x2_ref, sem):
    tm = dest_ref.shape[1]

    def body(j, carry):
        for k in range(TOP_K):
            _row_copy(ys_ref, dest_ref[k, j], buf_ref.at[k], j, sem).start(priority=k % 2)
        return carry

    lax.fori_loop(0, tm, body, 0)

    mod = mod_ref[0]
    h = h2_ref[...]
    act = (_silu(_dot(h, sg_ref[...])) * _dot(h, su_ref[...])).astype(BF16)
    moe = _dot(act, sd_ref[...])

    g = gate_ref[...]
    g0 = g.astype(BF16)
    r1 = g - g0.astype(F32)
    g1 = r1.astype(BF16)
    g2 = (r1 - g1.astype(F32)).astype(BF16)
    eye = (lax.broadcasted_iota(jnp.int32, (tm, tm), 0)
           == lax.broadcasted_iota(jnp.int32, (tm, tm), 1)).astype(BF16)
    dn = (((1,), (1,)), ((), ()))
    gcol = (lax.dot_general(eye, g0, dn, preferred_element_type=F32)
            + lax.dot_general(eye, g1, dn, preferred_element_type=F32)
            + lax.dot_general(eye, g2, dn, preferred_element_type=F32))

    gk = [jnp.broadcast_to(gcol[:, k:k + 1], (tm, LANES)) for k in range(TOP_K)]
    for k in range(TOP_K):
        _rows_wait(buf_ref.at[k], tm, sem)
    ssq = jnp.zeros((tm, 1), F32)
    for s in range(ROW_TILES):
        sl = slice(s * LANES, (s + 1) * LANES)
        m = moe[:, sl]
        for k in range(TOP_K):
            m = m + gk[k] * _load_row_tiles(buf_ref.at[k], s)
        x2 = x1_ref[:, sl] + mod[5:6, sl] * m
        x2_ref[:, sl] = x2
        ssq = ssq + jnp.sum(x2 * x2, axis=-1, keepdims=True)
    out_ref[...] = x2_ref[...] * lax.rsqrt(ssq * (1.0 / D_MODEL) + EPS) * fg_ref[...]


def _combine(dest, ys, gate, x1, h2, mods, sg, su, sd, fg, seq):
    t = x1.shape[0]
    tm = min(TM_DMA, seq)
    spb = seq // tm
    row = lambda: pl.BlockSpec((tm, D_MODEL), lambda i: (i, 0))
    const = lambda s: pl.BlockSpec(s, lambda i: (0,) * len(s))
    return pl.pallas_call(
        _combine_kernel,
        grid=(t // tm,),
        in_specs=[pl.BlockSpec((TOP_K, tm), lambda i: (0, i), memory_space=pltpu.SMEM),
                  pl.BlockSpec(memory_space=pl.ANY),
                  pl.BlockSpec((TOP_K, tm), lambda i: (0, i)),
                  row(), row(),
                  pl.BlockSpec((1, 6, D_MODEL), lambda i: (i // spb, 0, 0)),
                  const((D_MODEL, D_SHARED)), const((D_MODEL, D_SHARED)), const((D_SHARED, D_MODEL)),
                  const((1, D_MODEL))],
        out_specs=row(),
        out_shape=jax.ShapeDtypeStruct((t, D_MODEL), F32),
        scratch_shapes=[pltpu.VMEM((TOP_K, tm * ROW_TILES, LANES), F32), pltpu.VMEM((tm, D_MODEL), F32),
                        pltpu.SemaphoreType.DMA(())],
        compiler_params=_params("arbitrary"),
        name="combine",
    )(dest, ys, gate, x1, h2, mods, sg, su, sd, fg)


def _rope_tables(seq):
    pos = jnp.arange(seq)
    n_freq = HEAD_DIM // 4
    inv = ROPE_THETA ** (-jnp.arange(n_freq, dtype=F32) / n_freq)
    ang_r = (pos // GRID_W)[:, None].astype(F32) * inv
    ang_c = (pos % GRID_W)[:, None].astype(F32) * inv
    cr, sr, cc, sc = jnp.cos(ang_r), jnp.sin(ang_r), jnp.cos(ang_c), jnp.sin(ang_c)
    cos = jnp.concatenate([cr, cr, cc, cc], axis=1)
    sin = jnp.concatenate([-sr, sr, -sc, sc], axis=1)
    reps = LANES // HEAD_DIM
    return jnp.tile(cos, (1, reps)), jnp.tile(sin, (1, reps))


def _layer(x2d, ctx2d, mods, batch, seq, norm1_g, norm2_g, w_in, ln_g, ln_b, gmlp_ws, gmlp_bs, sink,
           w_a, w_b, w_o, router_w, router_b, e_gate, e_up, e_down, s_gate, s_up, s_down, final_g):
    t = x2d.shape[0]
    g1 = norm1_g.reshape(1, D_MODEL)
    w_in_b = w_in.astype(BF16)
    cos, sin = _rope_tables(seq)

    kctx, vctx = _ctx_kv(ctx2d, mods, g1, w_in_b[:, C_K:C_GA])
    gu, vn, q, k, v, sga, sgb = _inproj(x2d, mods, g1, w_in_b, ln_g.reshape(1, D_A), ln_b.reshape(1, D_A),
                                        cos, sin, seq)

    bs_full = jnp.repeat(gmlp_bs.T, D_A // G_A, axis=1)
    a, o = _mix(sink, gu, vn, q, k, v, kctx, vctx, gmlp_ws.astype(BF16), bs_full, batch, seq)

    rwt = router_w.T
    rwh = rwt.astype(BF16)
    rwl = (rwt - rwh.astype(F32)).astype(BF16)
    x1, h2t, h2b, idx, gate, rank, counts = _merge(
        a, o, sga, sgb, x2d, mods, norm2_g.reshape(1, D_MODEL),
        w_a.astype(BF16), w_b.astype(BF16), w_o.astype(BF16), rwh, rwl, router_b.reshape(N_EXPERTS, 1), seq)

    cnt = counts[:, 0].astype(jnp.int32)
    pcnt = (cnt + EXP_BLK - 1) // EXP_BLK * EXP_BLK
    pend = jnp.cumsum(pcnt)
    pstart = pend - pcnt
    onehot = idx[:, :, None] == jnp.arange(N_EXPERTS, dtype=jnp.int32)
    dest = jnp.sum(jnp.where(onehot, pstart, 0), axis=-1) + rank
    nblocks = (t * TOP_K) // EXP_BLK + N_EXPERTS
    nval = (pend[-1] // EXP_BLK).astype(jnp.int32)
    bsrc = jnp.minimum(jnp.arange(nblocks, dtype=jnp.int32), nval - 1)
    bexp = jnp.sum((pend[None, :] <= (bsrc * EXP_BLK)[:, None]).astype(jnp.int32), axis=1)
    bexp = jnp.minimum(bexp, N_EXPERTS - 1)

    nval = nval.reshape(1)
    xs = _dispatch(pstart + cnt, pcnt - cnt, nval, dest, h2t, nblocks * EXP_BLK)
    ys = _experts(bexp, bsrc, nval, xs, e_gate, e_up, e_down)
    return _combine(dest, ys, gate, x1, h2b, mods, s_gate.astype(BF16), s_up.astype(BF16),
                    s_down.astype(BF16), final_g.reshape(1, D_MODEL), seq)


def kernel(x, c, ctx, c_ctx, ada_w, ada_b, norm1_g, norm2_g, w_in, gmlp_ln_g, gmlp_ln_b, gmlp_ws, gmlp_bs,
           attn_sink, w_branch_a, w_branch_b, w_out, router_w, router_b, exp_w_gate, exp_w_up, exp_w_down,
           sh_w_gate, sh_w_up, sh_w_down, final_g):
    batch, seq, _ = x.shape
    depth = ada_w.shape[0]
    assert depth == 1, "the context stream is only carried as keys/values of a single layer"
    assert batch + 1 <= 8 and seq % WBLK == 0
    cond = jnp.concatenate([c, c_ctx[None], jnp.zeros((8 - batch - 1, D_MODEL), F32)], axis=0)
    mods = _ada(cond, ada_w[0], ada_b[0])[:batch + 1].reshape(batch + 1, 6, D_MODEL)
    out = _layer(x.reshape(batch * seq, D_MODEL), ctx.reshape(-1, D_MODEL), mods, batch, seq,
                 norm1_g[0], norm2_g[0], w_in[0], gmlp_ln_g[0], gmlp_ln_b[0], gmlp_ws[0], gmlp_bs[0],
                 attn_sink[0], w_branch_a[0], w_branch_b[0], w_out[0], router_w[0], router_b[0],
                 exp_w_gate[0], exp_w_up[0], exp_w_down[0], sh_w_gate[0], sh_w_up[0], sh_w_down[0], final_g)
    return out.reshape(batch, seq, D_MODEL)
```

```python
import functools

import jax
import jax.numpy as jnp
from jax import lax
from jax.experimental import pallas as pl
from jax.experimental.pallas import tpu as pltpu

F32 = jnp.float32
BF16 = jnp.bfloat16

D_MODEL = 1024
EPS = 1e-6
GRID_W = 64
D_A = D_MODEL // 2
G_A = 4
CHUNK = 128
N_HEADS = 8
N_KV = 2
REP = N_HEADS // N_KV
HEAD_DIM = 64
D_Q = N_HEADS * HEAD_DIM
D_KV = N_KV * HEAD_DIM
WBLK = 128
ROPE_THETA = 10000.0
N_EXPERTS = 256
TOP_K = 8
D_EXPERT = D_MODEL // 4
D_SHARED = D_MODEL // 4
ROUTE_SCALE = 2.5

C_U = 0
C_V = D_A
C_Q = 2 * D_A
C_K = C_Q + D_Q
C_VAL = C_K + D_KV
C_GA = C_VAL + D_KV
C_GB = C_GA + D_MODEL
D_IN = C_GB + D_MODEL

LANES = 128
ROPE_HALF = HEAD_DIM // 4
NEG_BIG = -1e30

TM_PROJ = 512
TM_MERGE = 512
TM_DMA = 256
EXP_BLK = 256
W_SLOTS = 3
ROW_DT = jnp.uint32
ROW_TILES = D_MODEL // (2 * LANES)
VMEM_LIMIT = 56 * 1024 * 1024


def _gelu(x):
    return 0.5 * x * (1.0 + jnp.tanh(0.7978845608028654 * (x + 0.044715 * x * x * x)))


def _silu(x):
    return x * jax.nn.sigmoid(x)


def _dot(a, b):
    return jnp.dot(a, b, preferred_element_type=F32)


def _rms_mod(x, g, shift, scale):
    ms = jnp.mean(x * x, axis=-1, keepdims=True)
    return (x * lax.rsqrt(ms + EPS)) * g * (1.0 + scale) + shift


def _params(*sem):
    return pltpu.CompilerParams(dimension_semantics=sem, vmem_limit_bytes=VMEM_LIMIT)


def _ada_kernel(c_ref, w_ref, b_ref, o_ref):
    c = c_ref[...]
    s = _silu(c).astype(BF16)
    o_ref[...] = _dot(s, w_ref[...].astype(BF16)) + b_ref[...]


def _ada(cond8, ada_w, ada_b):
    n = ada_w.shape[1]
    tn = 1536
    return pl.pallas_call(
        _ada_kernel,
        grid=(n // tn,),
        in_specs=[pl.BlockSpec((8, D_MODEL), lambda j: (0, 0)),
                  pl.BlockSpec((D_MODEL, tn), lambda j: (0, j)),
                  pl.BlockSpec((1, tn), lambda j: (0, j))],
        out_specs=pl.BlockSpec((8, tn), lambda j: (0, j)),
        out_shape=jax.ShapeDtypeStruct((8, n), F32),
        compiler_params=_params("arbitrary"),
        name="ada",
    )(cond8, ada_w, ada_b.reshape(1, n))


def _ctx_kernel(x_ref, mod_ref, g_ref, w_ref, k_ref, v_ref):
    mod = mod_ref[0]
    h = _rms_mod(x_ref[...], g_ref[...], mod[0:1], mod[1:2]).astype(BF16)
    z = _dot(h, w_ref[...])
    k_ref[...] = z[:, :D_KV].astype(BF16)
    v_ref[...] = z[:, D_KV:].astype(BF16)


def _ctx_kv(ctx2d, mods, g1, w_kv):
    n = ctx2d.shape[0]
    nb = mods.shape[0] - 1
    return pl.pallas_call(
        _ctx_kernel,
        grid=(1,),
        in_specs=[pl.BlockSpec((n, D_MODEL), lambda i: (0, 0)),
                  pl.BlockSpec((1, 6, D_MODEL), lambda i: (nb, 0, 0)),
                  pl.BlockSpec((1, D_MODEL), lambda i: (0, 0)),
                  pl.BlockSpec((D_MODEL, 2 * D_KV), lambda i: (0, 0))],
        out_specs=[pl.BlockSpec((n, D_KV), lambda i: (0, 0)),
                   pl.BlockSpec((n, D_KV), lambda i: (0, 0))],
        out_shape=[jax.ShapeDtypeStruct((n, D_KV), BF16)] * 2,
        compiler_params=_params("arbitrary"),
        name="ctx_kv",
    )(ctx2d, mods, g1, w_kv)


def _rope(t, cos, sin):
    lane = lax.broadcasted_iota(jnp.int32, (t.shape[0], LANES), 1)
    first = (lane & (2 * ROPE_HALF - 1)) < ROPE_HALF
    outs = []
    for j in range(t.shape[1] // LANES):
        tj = t[:, j * LANES:(j + 1) * LANES]
        up = pltpu.roll(tj, LANES - ROPE_HALF, 1)
        dn = pltpu.roll(tj, ROPE_HALF, 1)
        outs.append(tj * cos + jnp.where(first, up, dn) * sin)
    return outs


def _inproj_kernel(x_ref, mod_ref, g_ref, w_ref, lng_ref, lnb_ref, cos_ref, sin_ref,
                   gu_ref, vn_ref, q_ref, k_ref, v_ref, sga_ref, sgb_ref):
    mod = mod_ref[0]
    h = _rms_mod(x_ref[...], g_ref[...], mod[0:1], mod[1:2]).astype(BF16)

    def proj(lo, hi):
        return _dot(h, w_ref[:, lo:hi])

    gu_ref[...] = _gelu(proj(C_U, C_V)).astype(BF16)

    v = _gelu(proj(C_V, C_Q))
    mu = jnp.mean(v, axis=-1, keepdims=True)
    vc = v - mu
    var = jnp.mean(vc * vc, axis=-1, keepdims=True)
    vn_ref[...] = (vc * lax.rsqrt(var + EPS) * lng_ref[...] + lnb_ref[...]).astype(BF16)

    cos = cos_ref[...]
    sin = sin_ref[...]
    q = _rope(proj(C_Q, C_K) * (HEAD_DIM ** -0.5), cos, sin)
    for j, qj in enumerate(q):
        q_ref[:, j * LANES:(j + 1) * LANES] = qj.astype(BF16)
    k = _rope(proj(C_K, C_VAL), cos, sin)
    k_ref[...] = k[0].astype(BF16)
    v_ref[...] = proj(C_VAL, C_GA).astype(BF16)
    sga_ref[...] = jax.nn.sigmoid(proj(C_GA, C_GB)).astype(BF16)
    sgb_ref[...] = jax.nn.sigmoid(proj(C_GB, D_IN)).astype(BF16)


def _inproj(x2d, mods, g1, w_in, lng, lnb, cos, sin, seq):
    t = x2d.shape[0]
    tm = min(TM_PROJ, seq)
    spb = seq // tm
    row = lambda w: pl.BlockSpec((tm, w), lambda i: (i, 0))
    const = lambda s: pl.BlockSpec(s, lambda i: (0,) * len(s))
    return pl.pallas_call(
        _inproj_kernel,
        grid=(t // tm,),
        in_specs=[row(D_MODEL),
                  pl.BlockSpec((1, 6, D_MODEL), lambda i: (i // spb, 0, 0)),
                  const((1, D_MODEL)),
                  const((D_MODEL, D_IN)),
                  const((1, D_A)), const((1, D_A)),
                  pl.BlockSpec((tm, LANES), lambda i: (i % spb, 0)),
                  pl.BlockSpec((tm, LANES), lambda i: (i % spb, 0))],
        out_specs=[row(D_A), row(D_A), row(D_Q), row(D_KV), row(D_KV), row(D_MODEL), row(D_MODEL)],
        out_shape=[jax.ShapeDtypeStruct((t, w), BF16)
                   for w in (D_A, D_A, D_Q, D_KV, D_KV, D_MODEL, D_MODEL)],
        compiler_params=_params("arbitrary"),
        name="inproj",
    )(x2d, mods, g1, w_in, lng, lnb, cos, sin)


def _mix_kernel(sink_ref, gu_ref, vn_ref, q_ref, kp_ref, kc_ref, kn_ref, vp_ref, vc_ref, vx_ref,
                kctx_ref, vctx_ref, ws_ref, bs_ref, a_ref, o_ref):
    n = pl.program_id(1)
    nblk = pl.num_programs(1)

    for g in range(G_A):
        sl = slice(g * CHUNK, (g + 1) * CHUNK)
        s = _dot(ws_ref[g], vn_ref[:, sl]) + bs_ref[:, sl]
        a_ref[:, sl] = (gu_ref[:, sl].astype(F32) * s).astype(BF16)

    kcat = jnp.concatenate([kp_ref[...], kc_ref[...], kn_ref[...], kctx_ref[...]], axis=0)
    vcat = jnp.concatenate([vp_ref[...], vc_ref[...], vx_ref[...], vctx_ref[...]], axis=0)
    nwin = 3 * WBLK
    nkeys = kcat.shape[0]
    rows = REP * WBLK
    qi = lax.broadcasted_iota(jnp.int32, (rows, nkeys), 0) & (WBLK - 1)
    kj = lax.broadcasted_iota(jnp.int32, (rows, nkeys), 1)
    lo = jnp.where(n == 0, WBLK, 0)
    hi = jnp.where(n == nblk - 1, 2 * WBLK, nwin)
    valid = (kj >= nwin) | ((kj >= qi) & (kj <= qi + 2 * WBLK) & (kj >= lo) & (kj < hi))
    rgrp = lax.broadcasted_iota(jnp.int32, (rows, 1), 0) // WBLK

    for kvh in range(N_KV):
        ksl = kcat[:, kvh * HEAD_DIM:(kvh + 1) * HEAD_DIM]
        vsl = vcat[:, kvh * HEAD_DIM:(kvh + 1) * HEAD_DIM]
        qs = jnp.concatenate(
            [q_ref[:, (kvh * REP + r) * HEAD_DIM:(kvh * REP + r + 1) * HEAD_DIM] for r in range(REP)],
            axis=0)
        sink = jnp.zeros((rows, 1), F32)
        for r in range(REP):
            sink = jnp.where(rgrp == r, sink_ref[kvh * REP + r], sink)
        s = lax.dot_general(qs, ksl, (((1,), (1,)), ((), ())), preferred_element_type=F32)
        s = jnp.where(valid, s, NEG_BIG)
        m = jnp.maximum(jnp.max(s, axis=-1, keepdims=True), sink)
        p = jnp.exp(s - m)
        den = jnp.sum(p, axis=-1, keepdims=True) + jnp.exp(sink - m)
        o = _dot(p.astype(BF16), vsl) / den
        for r in range(0, REP, 2):
            pair = jnp.concatenate([o[r * WBLK:(r + 1) * WBLK], o[(r + 1) * WBLK:(r + 2) * WBLK]], axis=1)
            c0 = (kvh * REP + r) * HEAD_DIM
            o_ref[:, c0:c0 + 2 * HEAD_DIM] = pair.astype(BF16)


def _mix(sink, gu, vn, q, k, v, kctx, vctx, ws, bs_full, batch, seq):
    t = gu.shape[0]
    nblk = seq // WBLK
    nctx = kctx.shape[0] // batch
    cur = lambda w: pl.BlockSpec((WBLK, w), lambda b, n: (b * nblk + n, 0))
    prev = pl.BlockSpec((WBLK, D_KV), lambda b, n: (b * nblk + jnp.maximum(n - 1, 0), 0))
    nxt = pl.BlockSpec((WBLK, D_KV), lambda b, n: (b * nblk + jnp.minimum(n + 1, nblk - 1), 0))
    same = pl.BlockSpec((WBLK, D_KV), lambda b, n: (b * nblk + n, 0))
    cblk = pl.BlockSpec((nctx, D_KV), lambda b, n: (b, 0))
    return pl.pallas_call(
        _mix_kernel,
        grid=(batch, nblk),
        in_specs=[pl.BlockSpec(memory_space=pltpu.SMEM),
                  cur(D_A), cur(D_A), cur(D_Q),
                  prev, same, nxt, prev, same, nxt,
                  cblk, cblk,
                  pl.BlockSpec((G_A, CHUNK, CHUNK), lambda b, n: (0, 0, 0)),
                  pl.BlockSpec((CHUNK, D_A), lambda b, n: (0, 0))],
        out_specs=[cur(D_A), cur(D_Q)],
        out_shape=[jax.ShapeDtypeStruct((t, D_A), BF16), jax.ShapeDtypeStruct((t, D_Q), BF16)],
        compiler_params=_params("arbitrary", "arbitrary"),
        name="mix",
    )(sink, gu, vn, q, k, k, k, v, v, v, kctx, vctx, ws, bs_full)


def _store_row_tiles(ref, val):
    rows = val.shape[0]
    for s in range(ROW_TILES):
        lo = val[:, (2 * s) * LANES:(2 * s + 1) * LANES]
        hi = val[:, (2 * s + 1) * LANES:(2 * s + 2) * LANES]
        ref[pl.ds(s, rows, stride=ROW_TILES), :] = pltpu.pack_elementwise([lo, hi], packed_dtype=BF16)


def _load_row_tiles(ref, s):
    words = ref[pl.ds(s, ref.shape[0] // ROW_TILES, stride=ROW_TILES), :]
    return tuple(pltpu.unpack_elementwise(words, index=i, packed_dtype=BF16, unpacked_dtype=F32) for i in (0, 1))


def _merge_kernel(a_ref, o_ref, sga_ref, sgb_ref, x_ref, mod_ref, g2_ref, wa_ref, wb_ref, wo_ref,
                  rwh_ref, rwl_ref, rb_ref,
                  x1_ref, h2t_ref, h2b_ref, idx_ref, gate_ref, rank_ref, cnt_ref, base_ref):
    i = pl.program_id(0)

    @pl.when(i == 0)
    def _():
        base_ref[...] = jnp.zeros_like(base_ref)

    mod = mod_ref[0]
    ya = _dot(a_ref[...], wa_ref[...])
    yb = _dot(o_ref[...], wb_ref[...])
    y = sga_ref[...].astype(F32) * ya + sgb_ref[...].astype(F32) * yb
    x1 = x_ref[...] + mod[2:3] * _dot(y.astype(BF16), wo_ref[...])
    x1_ref[...] = x1
    h2 = _rms_mod(x1, g2_ref[...], mod[3:4], mod[4:5])
    h2b_ref[...] = h2.astype(BF16)
    _store_row_tiles(h2t_ref, h2)

    hh = h2.astype(BF16)
    hl = (h2 - hh.astype(F32)).astype(BF16)
    dn = (((1,), (1,)), ((), ()))
    logits = (lax.dot_general(rwh_ref[...], hh, dn, preferred_element_type=F32)
              + lax.dot_general(rwl_ref[...], hh, dn, preferred_element_type=F32)
              + lax.dot_general(rwh_ref[...], hl, dn, preferred_element_type=F32))
    scores = jax.nn.sigmoid(logits)
    tm = scores.shape[1]
    eio = lax.broadcasted_iota(jnp.int32, scores.shape, 0).astype(F32)
    work = scores + rb_ref[...]
    picked = jnp.zeros(scores.shape, F32)
    idxs, vals = [], []
    for _ in range(TOP_K):
        m = jnp.max(work, axis=0, keepdims=True)
        ik = jnp.min(jnp.where(work == m, eio, float(N_EXPERTS)), axis=0, keepdims=True)
        oh = eio == ik
        vals.append(jnp.sum(jnp.where(oh, scores, 0.0), axis=0, keepdims=True))
        idxs.append(ik)
        work = jnp.where(oh, -jnp.inf, work)
        picked = picked + oh.astype(F32)
    total = vals[0]
    for vk in vals[1:]:
        total = total + vk

    tr = lax.broadcasted_iota(jnp.int32, (tm, tm), 0)
    tc = lax.broadcasted_iota(jnp.int32, (tm, tm), 1)
    before = (tr < tc).astype(BF16)
    base = base_ref[...]
    rank_e = _dot(picked.astype(BF16), before) + base
    ranks = [jnp.sum(jnp.where(eio == ik, rank_e, 0.0), axis=0, keepdims=True) for ik in idxs]
    base = base + jnp.sum(picked, axis=1, keepdims=True)
    base_ref[...] = base
    cnt_ref[...] = base

    idx_ref[...] = jnp.concatenate(idxs, axis=0).astype(jnp.int32)
    rank_ref[...] = jnp.concatenate(ranks, axis=0).astype(jnp.int32)
    gate_ref[...] = jnp.concatenate(vals, axis=0) * (ROUTE_SCALE / total)


def _merge(a, o, sga, sgb, x2d, mods, g2, wa, wb, wo, rwh, rwl, rb, seq):
    t = x2d.shape[0]
    tm = min(TM_MERGE, seq)
    spb = seq // tm
    row = lambda w: pl.BlockSpec((tm, w), lambda i: (i, 0))
    col = lambda: pl.BlockSpec((TOP_K, tm), lambda i: (0, i))
    const = lambda s: pl.BlockSpec(s, lambda i: (0,) * len(s))
    return pl.pallas_call(
        _merge_kernel,
        grid=(t // tm,),
        in_specs=[row(D_A), row(D_Q), row(D_MODEL), row(D_MODEL), row(D_MODEL),
                  pl.BlockSpec((1, 6, D_MODEL), lambda i: (i // spb, 0, 0)),
                  const((1, D_MODEL)),
                  const((D_A, D_MODEL)), const((D_Q, D_MODEL)), const((D_MODEL, D_MODEL)),
                  const((N_EXPERTS, D_MODEL)), const((N_EXPERTS, D_MODEL)), const((N_EXPERTS, 1))],
        out_specs=[row(D_MODEL), pl.BlockSpec((tm * ROW_TILES, LANES), lambda i: (i, 0)), row(D_MODEL),
                   col(), col(), col(), const((N_EXPERTS, 1))],
        out_shape=[jax.ShapeDtypeStruct((t, D_MODEL), F32),
                   jax.ShapeDtypeStruct((t * ROW_TILES, LANES), ROW_DT),
                   jax.ShapeDtypeStruct((t, D_MODEL), BF16),
                   jax.ShapeDtypeStruct((TOP_K, t), jnp.int32),
                   jax.ShapeDtypeStruct((TOP_K, t), F32),
                   jax.ShapeDtypeStruct((TOP_K, t), jnp.int32),
                   jax.ShapeDtypeStruct((N_EXPERTS, 1), F32)],
        scratch_shapes=[pltpu.VMEM((N_EXPERTS, 1), F32)],
        compiler_params=_params("arbitrary"),
        name="merge",
    )(a, o, sga, sgb, x2d, mods, g2, wa, wb, wo, rwh, rwl, rb)


def _row_copy(src, si, dst, di, sem):
    return pltpu.make_async_copy(src.at[pl.ds(pl.multiple_of(si * ROW_TILES, ROW_TILES), ROW_TILES), :],
                                 dst.at[pl.ds(pl.multiple_of(di * ROW_TILES, ROW_TILES), ROW_TILES), :], sem)


def _rows_wait(ref, nrows, sem):
    n = nrows * ROW_TILES
    pltpu.make_async_copy(ref.at[pl.ds(0, n), :], ref.at[pl.ds(0, n), :], sem).wait()


def _dispatch_kernel(nsteps, pad0_ref, padn_ref, nval_ref, dest_ref, h_ref, xs_ref, zero_ref, sem, zsem):
    i = pl.program_id(0)
    tm = dest_ref.shape[1]
    nblocks = xs_ref.shape[0] // (EXP_BLK * ROW_TILES)
    experts_per_step = -(-N_EXPERTS // nsteps)
    tail_per_step = -(-nblocks // nsteps)

    @pl.when(i == 0)
    def _():
        zero_ref[...] = jnp.zeros_like(zero_ref)

    def body(j, carry):
        for k in range(TOP_K):
            _row_copy(h_ref, j, xs_ref, dest_ref[k, j], sem).start(priority=k % 2)
        return carry

    lax.fori_loop(0, tm, body, 0)

    def zero_fill(act):
        def pad_body(r, carry):
            e = i * experts_per_step + r

            @pl.when(e < N_EXPERTS)
            def _():
                first = pad0_ref[e]
                n = padn_ref[e]
                bit = EXP_BLK // 2
                while bit:
                    off = first + (n & ~(2 * bit - 1))

                    @pl.when((n & bit) != 0)
                    def _(bit=bit, off=off):
                        act(pltpu.make_async_copy(
                            zero_ref.at[pl.ds(0, bit * ROW_TILES), :],
                            xs_ref.at[pl.ds(pl.multiple_of(off * ROW_TILES, ROW_TILES), bit * ROW_TILES), :], zsem))
                    bit //= 2
            return carry

        lax.fori_loop(0, experts_per_step, pad_body, 0)

        def tail_body(r, carry):
            blk = nval_ref[0] + i * tail_per_step + r

            @pl.when(blk < nblocks)
            def _():
                rows = EXP_BLK * ROW_TILES
                act(pltpu.make_async_copy(zero_ref, xs_ref.at[pl.ds(pl.multiple_of(blk * rows, rows), rows), :],
                                          zsem))
            return carry

        lax.fori_loop(0, tail_per_step, tail_body, 0)

    zero_fill(lambda cp: cp.start())
    _rows_wait(xs_ref, tm * TOP_K, sem)
    zero_fill(lambda cp: cp.wait())


def _dispatch(pad0, padn, nval, dest, h2t, nrows):
    t = h2t.shape[0] // ROW_TILES
    tm = min(TM_DMA, t)
    nsteps = t // tm
    grid_spec = pltpu.PrefetchScalarGridSpec(
        num_scalar_prefetch=3,
        grid=(nsteps,),
        in_specs=[pl.BlockSpec((TOP_K, tm), lambda i, *_: (0, i), memory_space=pltpu.SMEM),
                  pl.BlockSpec((tm * ROW_TILES, LANES), lambda i, *_: (i, 0))],
        out_specs=pl.BlockSpec(memory_space=pl.ANY),
        scratch_shapes=[pltpu.VMEM((EXP_BLK * ROW_TILES, LANES), ROW_DT),
                        pltpu.SemaphoreType.DMA(()), pltpu.SemaphoreType.DMA(())],
    )
    return pl.pallas_call(
        functools.partial(_dispatch_kernel, nsteps),
        grid_spec=grid_spec,
        out_shape=jax.ShapeDtypeStruct((nrows * ROW_TILES, LANES), ROW_DT),
        compiler_params=_params("arbitrary"),
        name="dispatch",
    )(pad0, padn, nval, dest, h2t)


def _expert_kernel(bsrc_ref, nval_ref, first_ref, run_ref, rexp_ref, nruns_ref,
                   xs_ref, wg_hbm, wu_hbm, wd_hbm, ys_ref,
                   xb_ref, wg_f, wu_f, wd_f, wg_b, wu_b, wd_b, wsem):
    b = pl.program_id(0)
    nruns = nruns_ref[0]

    def weights(j, act):
        e = rexp_ref[j]
        s = j % W_SLOTS
        for hbm, buf in ((wg_hbm, wg_f), (wu_hbm, wu_f), (wd_hbm, wd_f)):
            act(pltpu.make_async_copy(hbm.at[e], buf.at[s], wsem.at[s]))

    @pl.when(b == 0)
    def _():
        weights(0, lambda cp: cp.start())

        @pl.when(nruns > 1)
        def _():
            weights(1, lambda cp: cp.start())

    @pl.when(b < nval_ref[0])
    def _():
        j = run_ref[b]

        @pl.when(first_ref[b] == 1)
        def _():
            weights(j, lambda cp: cp.wait())

            @pl.when(j + 2 < nruns)
            def _():
                weights(j + 2, lambda cp: cp.start())

            s = j % W_SLOTS
            wg_b[...] = wg_f[s].astype(BF16)
            wu_b[...] = wu_f[s].astype(BF16)
            wd_b[...] = wd_f[s].astype(BF16)

        for s in range(ROW_TILES):
            for h, part in enumerate(_load_row_tiles(xs_ref, s)):
                c = 2 * s + h
                xb_ref[:, c * LANES:(c + 1) * LANES] = part.astype(BF16)
        x = xb_ref[...]
        g = _dot(x, wg_b[...])
        u = _dot(x, wu_b[...])
        act = (_silu(g) * u).astype(BF16)
        _store_row_tiles(ys_ref, _dot(act, wd_b[...]))

    @pl.when(b >= nval_ref[0])
    def _():
        ys_ref[...] = jnp.zeros_like(ys_ref)


def _experts(bexp, bsrc, nval, xs, wg, wu, wd):
    nrows = xs.shape[0] // ROW_TILES
    nb = nrows // EXP_BLK
    blk = (EXP_BLK * ROW_TILES, LANES)
    blocks = jnp.arange(nb, dtype=jnp.int32)
    first = ((blocks == 0) | (bexp != jnp.roll(bexp, 1))).astype(jnp.int32)
    run = jnp.cumsum(first) - 1
    runs = jnp.arange(N_EXPERTS, dtype=jnp.int32)
    rexp = jnp.sum(jnp.where((first[None, :] == 1) & (run[None, :] == runs[:, None]), bexp[None, :], 0), axis=1)
    nruns = (run[nb - 1] + 1).reshape(1)
    grid_spec = pltpu.PrefetchScalarGridSpec(
        num_scalar_prefetch=6,
        grid=(nb,),
        in_specs=[pl.BlockSpec(blk, lambda b, bs, *_: (bs[b], 0)),
                  pl.BlockSpec(memory_space=pl.ANY), pl.BlockSpec(memory_space=pl.ANY),
                  pl.BlockSpec(memory_space=pl.ANY)],
        out_specs=pl.BlockSpec(blk, lambda b, *_: (b, 0)),
        scratch_shapes=[pltpu.VMEM((EXP_BLK, D_MODEL), BF16),
                        pltpu.VMEM((W_SLOTS, D_MODEL, D_EXPERT), F32),
                        pltpu.VMEM((W_SLOTS, D_MODEL, D_EXPERT), F32),
                        pltpu.VMEM((W_SLOTS, D_EXPERT, D_MODEL), F32),
                        pltpu.VMEM((D_MODEL, D_EXPERT), BF16),
                        pltpu.VMEM((D_MODEL, D_EXPERT), BF16),
                        pltpu.VMEM((D_EXPERT, D_MODEL), BF16),
                        pltpu.SemaphoreType.DMA((W_SLOTS,))],
    )
    return pl.pallas_call(
        _expert_kernel,
        grid_spec=grid_spec,
        out_shape=jax.ShapeDtypeStruct((nrows * ROW_TILES, LANES), ROW_DT),
        compiler_params=_params("arbitrary"),
        name="experts",
    )(bsrc, nval, first, run, rexp, nruns, xs, wg, wu, wd)


def _combine_kernel(dest_ref, ys_ref, gate_ref, x1_ref, h2_ref, mod_ref, sg_ref, su_ref, sd_ref, fg_ref,
                    out_ref, buf_ref, x2_ref, sem):
    tm = dest_ref.shape[1]

    def body(j, carry):
        for k in range(TOP_K):
            _row_copy(ys_ref, dest_ref[k, j], buf_ref.at[k], j, sem).start(priority=k % 2)
        return carry

    lax.fori_loop(0, tm, body, 0)

    mod = mod_ref[0]
    h = h2_ref[...]
    act = (_silu(_dot(h, sg_ref[...])) * _dot(h, su_ref[...])).astype(BF16)
    moe = _dot(act, sd_ref[...])

    g = gate_ref[...]
    g0 = g.astype(BF16)
    r1 = g - g0.astype(F32)
    g1 = r1.astype(BF16)
    g2 = (r1 - g1.astype(F32)).astype(BF16)
    eye = (lax.broadcasted_iota(jnp.int32, (tm, tm), 0)
           == lax.broadcasted_iota(jnp.int32, (tm, tm), 1)).astype(BF16)
    dn = (((1,), (1,)), ((), ()))
    gcol = (lax.dot_general(eye, g0, dn, preferred_element_type=F32)
            + lax.dot_general(eye, g1, dn, preferred_element_type=F32)
            + lax.dot_general(eye, g2, dn, preferred_element_type=F32))

    gk = [jnp.broadcast_to(gcol[:, k:k + 1], (tm, LANES)) for k in range(TOP_K)]
    for k in range(TOP_K):
        _rows_wait(buf_ref.at[k], tm, sem)
    ssq = jnp.zeros((tm, 1), F32)
    for s in range(ROW_TILES):
        parts = [_load_row_tiles(buf_ref.at[k], s) for k in range(TOP_K)]
        for h in range(2):
            c = 2 * s + h
            sl = slice(c * LANES, (c + 1) * LANES)
            m = moe[:, sl]
            for k in range(TOP_K):
                m = m + gk[k] * parts[k][h]
            x2 = x1_ref[:, sl] + mod[5:6, sl] * m
            x2_ref[:, sl] = x2
            ssq = ssq + jnp.sum(x2 * x2, axis=-1, keepdims=True)
    out_ref[...] = x2_ref[...] * lax.rsqrt(ssq * (1.0 / D_MODEL) + EPS) * fg_ref[...]


def _combine(dest, ys, gate, x1, h2, mods, sg, su, sd, fg, seq):
    t = x1.shape[0]
    tm = min(TM_DMA, seq)
    spb = seq // tm
    row = lambda: pl.BlockSpec((tm, D_MODEL), lambda i: (i, 0))
    const = lambda s: pl.BlockSpec(s, lambda i: (0,) * len(s))
    return pl.pallas_call(
        _combine_kernel,
        grid=(t // tm,),
        in_specs=[pl.BlockSpec((TOP_K, tm), lambda i: (0, i), memory_space=pltpu.SMEM),
                  pl.BlockSpec(memory_space=pl.ANY),
                  pl.BlockSpec((TOP_K, tm), lambda i: (0, i)),
                  row(), row(),
                  pl.BlockSpec((1, 6, D_MODEL), lambda i: (i // spb, 0, 0)),
                  const((D_MODEL, D_SHARED)), const((D_MODEL, D_SHARED)), const((D_SHARED, D_MODEL)),
                  const((1, D_MODEL))],
        out_specs=row(),
        out_shape=jax.ShapeDtypeStruct((t, D_MODEL), F32),
        scratch_shapes=[pltpu.VMEM((TOP_K, tm * ROW_TILES, LANES), ROW_DT), pltpu.VMEM((tm, D_MODEL), F32),
                        pltpu.SemaphoreType.DMA(())],
        compiler_params=_params("arbitrary"),
        name="combine",
    )(dest, ys, gate, x1, h2, mods, sg, su, sd, fg)


def _rope_tables(seq):
    pos = jnp.arange(seq)
    n_freq = HEAD_DIM // 4
    inv = ROPE_THETA ** (-jnp.arange(n_freq, dtype=F32) / n_freq)
    ang_r = (pos // GRID_W)[:, None].astype(F32) * inv
    ang_c = (pos % GRID_W)[:, None].astype(F32) * inv
    cr, sr, cc, sc = jnp.cos(ang_r), jnp.sin(ang_r), jnp.cos(ang_c), jnp.sin(ang_c)
    cos = jnp.concatenate([cr, cr, cc, cc], axis=1)
    sin = jnp.concatenate([-sr, sr, -sc, sc], axis=1)
    reps = LANES // HEAD_DIM
    return jnp.tile(cos, (1, reps)), jnp.tile(sin, (1, reps))


def _layer(x2d, ctx2d, mods, batch, seq, norm1_g, norm2_g, w_in, ln_g, ln_b, gmlp_ws, gmlp_bs, sink,
           w_a, w_b, w_o, router_w, router_b, e_gate, e_up, e_down, s_gate, s_up, s_down, final_g):
    t = x2d.shape[0]
    g1 = norm1_g.reshape(1, D_MODEL)
    w_in_b = w_in.astype(BF16)
    cos, sin = _rope_tables(seq)

    kctx, vctx = _ctx_kv(ctx2d, mods, g1, w_in_b[:, C_K:C_GA])
    gu, vn, q, k, v, sga, sgb = _inproj(x2d, mods, g1, w_in_b, ln_g.reshape(1, D_A), ln_b.reshape(1, D_A),
                                        cos, sin, seq)

    bs_full = jnp.repeat(gmlp_bs.T, D_A // G_A, axis=1)
    a, o = _mix(sink, gu, vn, q, k, v, kctx, vctx, gmlp_ws.astype(BF16), bs_full, batch, seq)

    rwt = router_w.T
    rwh = rwt.astype(BF16)
    rwl = (rwt - rwh.astype(F32)).astype(BF16)
    x1, h2t, h2b, idx, gate, rank, counts = _merge(
        a, o, sga, sgb, x2d, mods, norm2_g.reshape(1, D_MODEL),
        w_a.astype(BF16), w_b.astype(BF16), w_o.astype(BF16), rwh, rwl, router_b.reshape(N_EXPERTS, 1), seq)

    cnt = counts[:, 0].astype(jnp.int32)
    pcnt = (cnt + EXP_BLK - 1) // EXP_BLK * EXP_BLK
    pend = jnp.cumsum(pcnt)
    pstart = pend - pcnt
    onehot = idx[:, :, None] == jnp.arange(N_EXPERTS, dtype=jnp.int32)
    dest = jnp.sum(jnp.where(onehot, pstart, 0), axis=-1) + rank
    nblocks = (t * TOP_K) // EXP_BLK + N_EXPERTS
    nval = (pend[-1] // EXP_BLK).astype(jnp.int32)
    bsrc = jnp.minimum(jnp.arange(nblocks, dtype=jnp.int32), nval - 1)
    bexp = jnp.sum((pend[None, :] <= (bsrc * EXP_BLK)[:, None]).astype(jnp.int32), axis=1)
    bexp = jnp.minimum(bexp, N_EXPERTS - 1)

    nval = nval.reshape(1)
    xs = _dispatch(pstart + cnt, pcnt - cnt, nval, dest, h2t, nblocks * EXP_BLK)
    ys = _experts(bexp, bsrc, nval, xs, e_gate, e_up, e_down)
    return _combine(dest, ys, gate, x1, h2b, mods, s_gate.astype(BF16), s_up.astype(BF16),
                    s_down.astype(BF16), final_g.reshape(1, D_MODEL), seq)


def kernel(x, c, ctx, c_ctx, ada_w, ada_b, norm1_g, norm2_g, w_in, gmlp_ln_g, gmlp_ln_b, gmlp_ws, gmlp_bs,
           attn_sink, w_branch_a, w_branch_b, w_out, router_w, router_b, exp_w_gate, exp_w_up, exp_w_down,
           sh_w_gate, sh_w_up, sh_w_down, final_g):
    batch, seq, _ = x.shape
    depth = ada_w.shape[0]
    assert depth == 1, "the context stream is only carried as keys/values of a single layer"
    assert batch + 1 <= 8 and seq % WBLK == 0
    cond = jnp.concatenate([c, c_ctx[None], jnp.zeros((8 - batch - 1, D_MODEL), F32)], axis=0)
    mods = _ada(cond, ada_w[0], ada_b[0])[:batch + 1].reshape(batch + 1, 6, D_MODEL)
    out = _layer(x.reshape(batch * seq, D_MODEL), ctx.reshape(-1, D_MODEL), mods, batch, seq,
                 norm1_g[0], norm2_g[0], w_in[0], gmlp_ln_g[0], gmlp_ln_b[0], gmlp_ws[0], gmlp_bs[0],
                 attn_sink[0], w_branch_a[0], w_branch_b[0], w_out[0], router_w[0], router_b[0],
                 exp_w_gate[0], exp_w_up[0], exp_w_down[0], sh_w_gate[0], sh_w_up[0], sh_w_down[0], final_g)
    return out.reshape(batch, seq, D_MODEL)
```

```python
import functools

import jax
import jax.numpy as jnp
from jax import lax
from jax.experimental import pallas as pl
from jax.experimental.pallas import tpu as pltpu

F32 = jnp.float32
BF16 = jnp.bfloat16

D_MODEL = 1024
EPS = 1e-6
GRID_W = 64
D_A = D_MODEL // 2
G_A = 4
CHUNK = 128
N_HEADS = 8
N_KV = 2
REP = N_HEADS // N_KV
HEAD_DIM = 64
D_Q = N_HEADS * HEAD_DIM
D_KV = N_KV * HEAD_DIM
WBLK = 128
ROPE_THETA = 10000.0
N_EXPERTS = 256
TOP_K = 8
D_EXPERT = D_MODEL // 4
D_SHARED = D_MODEL // 4
ROUTE_SCALE = 2.5

C_U = 0
C_V = D_A
C_Q = 2 * D_A
C_K = C_Q + D_Q
C_VAL = C_K + D_KV
C_GA = C_VAL + D_KV
C_GB = C_GA + D_MODEL
D_IN = C_GB + D_MODEL

LANES = 128
ROPE_HALF = HEAD_DIM // 4
NEG_BIG = -1e30

TM_PROJ = 512
TM_MERGE = 512
TM_DMA = 256
EXP_BLK = 256
W_SLOTS = 3
ROW_DT = BF16
ROW_TILES = D_MODEL // LANES
ROW_WORDS = ROW_TILES // 2
WORD_DT = jnp.uint32
VMEM_LIMIT = 56 * 1024 * 1024


def _gelu(x):
    return 0.5 * x * (1.0 + jnp.tanh(0.7978845608028654 * (x + 0.044715 * x * x * x)))


def _silu(x):
    return x * jax.nn.sigmoid(x)


def _dot(a, b):
    return jnp.dot(a, b, preferred_element_type=F32)


def _rms_mod(x, g, shift, scale):
    ms = jnp.mean(x * x, axis=-1, keepdims=True)
    return (x * lax.rsqrt(ms + EPS)) * g * (1.0 + scale) + shift


def _params(*sem):
    return pltpu.CompilerParams(dimension_semantics=sem, vmem_limit_bytes=VMEM_LIMIT)


def _ada_kernel(c_ref, w_ref, b_ref, o_ref):
    c = c_ref[...]
    s = _silu(c).astype(BF16)
    o_ref[...] = _dot(s, w_ref[...].astype(BF16)) + b_ref[...]


def _ada(cond8, ada_w, ada_b):
    n = ada_w.shape[1]
    tn = 1536
    return pl.pallas_call(
        _ada_kernel,
        grid=(n // tn,),
        in_specs=[pl.BlockSpec((8, D_MODEL), lambda j: (0, 0)),
                  pl.BlockSpec((D_MODEL, tn), lambda j: (0, j)),
                  pl.BlockSpec((1, tn), lambda j: (0, j))],
        out_specs=pl.BlockSpec((8, tn), lambda j: (0, j)),
        out_shape=jax.ShapeDtypeStruct((8, n), F32),
        compiler_params=_params("arbitrary"),
        name="ada",
    )(cond8, ada_w, ada_b.reshape(1, n))


def _ctx_kernel(x_ref, mod_ref, g_ref, w_ref, k_ref, v_ref):
    mod = mod_ref[0]
    h = _rms_mod(x_ref[...], g_ref[...], mod[0:1], mod[1:2]).astype(BF16)
    z = _dot(h, w_ref[...])
    k_ref[...] = z[:, :D_KV].astype(BF16)
    v_ref[...] = z[:, D_KV:].astype(BF16)


def _ctx_kv(ctx2d, mods, g1, w_kv):
    n = ctx2d.shape[0]
    nb = mods.shape[0] - 1
    return pl.pallas_call(
        _ctx_kernel,
        grid=(1,),
        in_specs=[pl.BlockSpec((n, D_MODEL), lambda i: (0, 0)),
                  pl.BlockSpec((1, 6, D_MODEL), lambda i: (nb, 0, 0)),
                  pl.BlockSpec((1, D_MODEL), lambda i: (0, 0)),
                  pl.BlockSpec((D_MODEL, 2 * D_KV), lambda i: (0, 0))],
        out_specs=[pl.BlockSpec((n, D_KV), lambda i: (0, 0)),
                   pl.BlockSpec((n, D_KV), lambda i: (0, 0))],
        out_shape=[jax.ShapeDtypeStruct((n, D_KV), BF16)] * 2,
        compiler_params=_params("arbitrary"),
        name="ctx_kv",
    )(ctx2d, mods, g1, w_kv)


def _rope(t, cos, sin):
    lane = lax.broadcasted_iota(jnp.int32, (t.shape[0], LANES), 1)
    first = (lane & (2 * ROPE_HALF - 1)) < ROPE_HALF
    outs = []
    for j in range(t.shape[1] // LANES):
        tj = t[:, j * LANES:(j + 1) * LANES]
        up = pltpu.roll(tj, LANES - ROPE_HALF, 1)
        dn = pltpu.roll(tj, ROPE_HALF, 1)
        outs.append(tj * cos + jnp.where(first, up, dn) * sin)
    return outs


def _inproj_kernel(x_ref, mod_ref, g_ref, w_ref, lng_ref, lnb_ref, cos_ref, sin_ref,
                   gu_ref, vn_ref, q_ref, k_ref, v_ref, sga_ref, sgb_ref):
    mod = mod_ref[0]
    h = _rms_mod(x_ref[...], g_ref[...], mod[0:1], mod[1:2]).astype(BF16)

    def proj(lo, hi):
        return _dot(h, w_ref[:, lo:hi])

    gu_ref[...] = _gelu(proj(C_U, C_V)).astype(BF16)

    v = _gelu(proj(C_V, C_Q))
    mu = jnp.mean(v, axis=-1, keepdims=True)
    vc = v - mu
    var = jnp.mean(vc * vc, axis=-1, keepdims=True)
    vn_ref[...] = (vc * lax.rsqrt(var + EPS) * lng_ref[...] + lnb_ref[...]).astype(BF16)

    cos = cos_ref[...]
    sin = sin_ref[...]
    q = _rope(proj(C_Q, C_K) * (HEAD_DIM ** -0.5), cos, sin)
    for j, qj in enumerate(q):
        q_ref[:, j * LANES:(j + 1) * LANES] = qj.astype(BF16)
    k = _rope(proj(C_K, C_VAL), cos, sin)
    k_ref[...] = k[0].astype(BF16)
    v_ref[...] = proj(C_VAL, C_GA).astype(BF16)
    sga_ref[...] = jax.nn.sigmoid(proj(C_GA, C_GB)).astype(BF16)
    sgb_ref[...] = jax.nn.sigmoid(proj(C_GB, D_IN)).astype(BF16)


def _inproj(x2d, mods, g1, w_in, lng, lnb, cos, sin, seq):
    t = x2d.shape[0]
    tm = min(TM_PROJ, seq)
    spb = seq // tm
    row = lambda w: pl.BlockSpec((tm, w), lambda i: (i, 0))
    const = lambda s: pl.BlockSpec(s, lambda i: (0,) * len(s))
    return pl.pallas_call(
        _inproj_kernel,
        grid=(t // tm,),
        in_specs=[row(D_MODEL),
                  pl.BlockSpec((1, 6, D_MODEL), lambda i: (i // spb, 0, 0)),
                  const((1, D_MODEL)),
                  const((D_MODEL, D_IN)),
                  const((1, D_A)), const((1, D_A)),
                  pl.BlockSpec((tm, LANES), lambda i: (i % spb, 0)),
                  pl.BlockSpec((tm, LANES), lambda i: (i % spb, 0))],
        out_specs=[row(D_A), row(D_A), row(D_Q), row(D_KV), row(D_KV), row(D_MODEL), row(D_MODEL)],
        out_shape=[jax.ShapeDtypeStruct((t, w), BF16)
                   for w in (D_A, D_A, D_Q, D_KV, D_KV, D_MODEL, D_MODEL)],
        compiler_params=_params("arbitrary"),
        name="inproj",
    )(x2d, mods, g1, w_in, lng, lnb, cos, sin)


def _mix_kernel(sink_ref, gu_ref, vn_ref, q_ref, kp_ref, kc_ref, kn_ref, vp_ref, vc_ref, vx_ref,
                kctx_ref, vctx_ref, ws_ref, bs_ref, a_ref, o_ref):
    n = pl.program_id(1)
    nblk = pl.num_programs(1)

    for g in range(G_A):
        sl = slice(g * CHUNK, (g + 1) * CHUNK)
        s = _dot(ws_ref[g], vn_ref[:, sl]) + bs_ref[:, sl]
        a_ref[:, sl] = (gu_ref[:, sl].astype(F32) * s).astype(BF16)

    kcat = jnp.concatenate([kp_ref[...], kc_ref[...], kn_ref[...], kctx_ref[...]], axis=0)
    vcat = jnp.concatenate([vp_ref[...], vc_ref[...], vx_ref[...], vctx_ref[...]], axis=0)
    nwin = 3 * WBLK
    nkeys = kcat.shape[0]
    rows = REP * WBLK
    qi = lax.broadcasted_iota(jnp.int32, (rows, nkeys), 0) & (WBLK - 1)
    kj = lax.broadcasted_iota(jnp.int32, (rows, nkeys), 1)
    lo = jnp.where(n == 0, WBLK, 0)
    hi = jnp.where(n == nblk - 1, 2 * WBLK, nwin)
    valid = (kj >= nwin) | ((kj >= qi) & (kj <= qi + 2 * WBLK) & (kj >= lo) & (kj < hi))
    rgrp = lax.broadcasted_iota(jnp.int32, (rows, 1), 0) // WBLK

    for kvh in range(N_KV):
        ksl = kcat[:, kvh * HEAD_DIM:(kvh + 1) * HEAD_DIM]
        vsl = vcat[:, kvh * HEAD_DIM:(kvh + 1) * HEAD_DIM]
        qs = jnp.concatenate(
            [q_ref[:, (kvh * REP + r) * HEAD_DIM:(kvh * REP + r + 1) * HEAD_DIM] for r in range(REP)],
            axis=0)
        sink = jnp.zeros((rows, 1), F32)
        for r in range(REP):
            sink = jnp.where(rgrp == r, sink_ref[kvh * REP + r], sink)
        s = lax.dot_general(qs, ksl, (((1,), (1,)), ((), ())), preferred_element_type=F32)
        s = jnp.where(valid, s, NEG_BIG)
        m = jnp.maximum(jnp.max(s, axis=-1, keepdims=True), sink)
        p = jnp.exp(s - m)
        den = jnp.sum(p, axis=-1, keepdims=True) + jnp.exp(sink - m)
        o = _dot(p.astype(BF16), vsl) / den
        for r in range(0, REP, 2):
            pair = jnp.concatenate([o[r * WBLK:(r + 1) * WBLK], o[(r + 1) * WBLK:(r + 2) * WBLK]], axis=1)
            c0 = (kvh * REP + r) * HEAD_DIM
            o_ref[:, c0:c0 + 2 * HEAD_DIM] = pair.astype(BF16)


def _mix(sink, gu, vn, q, k, v, kctx, vctx, ws, bs_full, batch, seq):
    t = gu.shape[0]
    nblk = seq // WBLK
    nctx = kctx.shape[0] // batch
    cur = lambda w: pl.BlockSpec((WBLK, w), lambda b, n: (b * nblk + n, 0))
    prev = pl.BlockSpec((WBLK, D_KV), lambda b, n: (b * nblk + jnp.maximum(n - 1, 0), 0))
    nxt = pl.BlockSpec((WBLK, D_KV), lambda b, n: (b * nblk + jnp.minimum(n + 1, nblk - 1), 0))
    same = pl.BlockSpec((WBLK, D_KV), lambda b, n: (b * nblk + n, 0))
    cblk = pl.BlockSpec((nctx, D_KV), lambda b, n: (b, 0))
    return pl.pallas_call(
        _mix_kernel,
        grid=(batch, nblk),
        in_specs=[pl.BlockSpec(memory_space=pltpu.SMEM),
                  cur(D_A), cur(D_A), cur(D_Q),
                  prev, same, nxt, prev, same, nxt,
                  cblk, cblk,
                  pl.BlockSpec((G_A, CHUNK, CHUNK), lambda b, n: (0, 0, 0)),
                  pl.BlockSpec((CHUNK, D_A), lambda b, n: (0, 0))],
        out_specs=[cur(D_A), cur(D_Q)],
        out_shape=[jax.ShapeDtypeStruct((t, D_A), BF16), jax.ShapeDtypeStruct((t, D_Q), BF16)],
        compiler_params=_params("arbitrary", "arbitrary"),
        name="mix",
    )(sink, gu, vn, q, k, k, k, v, v, v, kctx, vctx, ws, bs_full)


def _store_row_tiles(ref, val, wref):
    rows = val.shape[0]
    for s in range(ROW_WORDS):
        lo = val[:, (2 * s) * LANES:(2 * s + 1) * LANES]
        hi = val[:, (2 * s + 1) * LANES:(2 * s + 2) * LANES]
        wref[pl.ds(s, rows, stride=ROW_WORDS), :] = pltpu.pack_elementwise([lo, hi], packed_dtype=BF16)
    ref[...] = pltpu.bitcast(wref[...], ROW_DT)


def _load_row_tiles(wref, s):
    words = wref[pl.ds(s, wref.shape[0] // ROW_WORDS, stride=ROW_WORDS), :]
    return tuple(pltpu.unpack_elementwise(words, index=i, packed_dtype=BF16, unpacked_dtype=F32) for i in (0, 1))


def _merge_kernel(a_ref, o_ref, sga_ref, sgb_ref, x_ref, mod_ref, g2_ref, wa_ref, wb_ref, wo_ref,
                  rwh_ref, rwl_ref, rb_ref,
                  x1_ref, h2t_ref, h2b_ref, idx_ref, gate_ref, rank_ref, cnt_ref, base_ref, words_ref):
    i = pl.program_id(0)

    @pl.when(i == 0)
    def _():
        base_ref[...] = jnp.zeros_like(base_ref)

    mod = mod_ref[0]
    ya = _dot(a_ref[...], wa_ref[...])
    yb = _dot(o_ref[...], wb_ref[...])
    y = sga_ref[...].astype(F32) * ya + sgb_ref[...].astype(F32) * yb
    x1 = x_ref[...] + mod[2:3] * _dot(y.astype(BF16), wo_ref[...])
    x1_ref[...] = x1
    h2 = _rms_mod(x1, g2_ref[...], mod[3:4], mod[4:5])
    h2b_ref[...] = h2.astype(BF16)
    _store_row_tiles(h2t_ref, h2, words_ref)

    hh = h2.astype(BF16)
    hl = (h2 - hh.astype(F32)).astype(BF16)
    dn = (((1,), (1,)), ((), ()))
    logits = (lax.dot_general(rwh_ref[...], hh, dn, preferred_element_type=F32)
              + lax.dot_general(rwl_ref[...], hh, dn, preferred_element_type=F32)
              + lax.dot_general(rwh_ref[...], hl, dn, preferred_element_type=F32))
    scores = jax.nn.sigmoid(logits)
    tm = scores.shape[1]
    eio = lax.broadcasted_iota(jnp.int32, scores.shape, 0).astype(F32)
    work = scores + rb_ref[...]
    picked = jnp.zeros(scores.shape, F32)
    idxs, vals = [], []
    for _ in range(TOP_K):
        m = jnp.max(work, axis=0, keepdims=True)
        ik = jnp.min(jnp.where(work == m, eio, float(N_EXPERTS)), axis=0, keepdims=True)
        oh = eio == ik
        vals.append(jnp.sum(jnp.where(oh, scores, 0.0), axis=0, keepdims=True))
        idxs.append(ik)
        work = jnp.where(oh, -jnp.inf, work)
        picked = picked + oh.astype(F32)
    total = vals[0]
    for vk in vals[1:]:
        total = total + vk

    tr = lax.broadcasted_iota(jnp.int32, (tm, tm), 0)
    tc = lax.broadcasted_iota(jnp.int32, (tm, tm), 1)
    before = (tr < tc).astype(BF16)
    base = base_ref[...]
    rank_e = _dot(picked.astype(BF16), before) + base
    ranks = [jnp.sum(jnp.where(eio == ik, rank_e, 0.0), axis=0, keepdims=True) for ik in idxs]
    base = base + jnp.sum(picked, axis=1, keepdims=True)
    base_ref[...] = base
    cnt_ref[...] = base

    idx_ref[...] = jnp.concatenate(idxs, axis=0).astype(jnp.int32)
    rank_ref[...] = jnp.concatenate(ranks, axis=0).astype(jnp.int32)
    gate_ref[...] = jnp.concatenate(vals, axis=0) * (ROUTE_SCALE / total)


def _merge(a, o, sga, sgb, x2d, mods, g2, wa, wb, wo, rwh, rwl, rb, seq):
    t = x2d.shape[0]
    tm = min(TM_MERGE, seq)
    spb = seq // tm
    row = lambda w: pl.BlockSpec((tm, w), lambda i: (i, 0))
    col = lambda: pl.BlockSpec((TOP_K, tm), lambda i: (0, i))
    const = lambda s: pl.BlockSpec(s, lambda i: (0,) * len(s))
    return pl.pallas_call(
        _merge_kernel,
        grid=(t // tm,),
        in_specs=[row(D_A), row(D_Q), row(D_MODEL), row(D_MODEL), row(D_MODEL),
                  pl.BlockSpec((1, 6, D_MODEL), lambda i: (i // spb, 0, 0)),
                  const((1, D_MODEL)),
                  const((D_A, D_MODEL)), const((D_Q, D_MODEL)), const((D_MODEL, D_MODEL)),
                  const((N_EXPERTS, D_MODEL)), const((N_EXPERTS, D_MODEL)), const((N_EXPERTS, 1))],
        out_specs=[row(D_MODEL), pl.BlockSpec((tm * ROW_TILES, LANES), lambda i: (i, 0)), row(D_MODEL),
                   col(), col(), col(), const((N_EXPERTS, 1))],
        out_shape=[jax.ShapeDtypeStruct((t, D_MODEL), F32),
                   jax.ShapeDtypeStruct((t * ROW_TILES, LANES), ROW_DT),
                   jax.ShapeDtypeStruct((t, D_MODEL), BF16),
                   jax.ShapeDtypeStruct((TOP_K, t), jnp.int32),
                   jax.ShapeDtypeStruct((TOP_K, t), F32),
                   jax.ShapeDtypeStruct((TOP_K, t), jnp.int32),
                   jax.ShapeDtypeStruct((N_EXPERTS, 1), F32)],
        scratch_shapes=[pltpu.VMEM((N_EXPERTS, 1), F32), pltpu.VMEM((tm * ROW_WORDS, LANES), WORD_DT)],
        compiler_params=_params("arbitrary"),
        name="merge",
    )(a, o, sga, sgb, x2d, mods, g2, wa, wb, wo, rwh, rwl, rb)


def _row_copy(src, si, dst, di, sem):
    return pltpu.make_async_copy(src.at[pl.ds(pl.multiple_of(si * ROW_TILES, ROW_TILES), ROW_TILES), :],
                                 dst.at[pl.ds(pl.multiple_of(di * ROW_TILES, ROW_TILES), ROW_TILES), :], sem)


def _rows_wait(ref, nrows, sem):
    n = nrows * ROW_TILES
    pltpu.make_async_copy(ref.at[pl.ds(0, n), :], ref.at[pl.ds(0, n), :], sem).wait()


def _dispatch_kernel(nsteps, pad0_ref, padn_ref, nval_ref, dest_ref, h_ref, xs_ref, zero_ref, sem, zsem):
    i = pl.program_id(0)
    tm = dest_ref.shape[1]
    nblocks = xs_ref.shape[0] // (EXP_BLK * ROW_TILES)
    experts_per_step = -(-N_EXPERTS // nsteps)
    tail_per_step = -(-nblocks // nsteps)

    @pl.when(i == 0)
    def _():
        zero_ref[...] = jnp.zeros_like(zero_ref)

    def body(j, carry):
        for k in range(TOP_K):
            _row_copy(h_ref, j, xs_ref, dest_ref[k, j], sem).start(priority=k % 2)
        return carry

    lax.fori_loop(0, tm, body, 0)

    def zero_fill(act):
        def pad_body(r, carry):
            e = i * experts_per_step + r

            @pl.when(e < N_EXPERTS)
            def _():
                first = pad0_ref[e]
                n = padn_ref[e]
                bit = EXP_BLK // 2
                while bit:
                    off = first + (n & ~(2 * bit - 1))

                    @pl.when((n & bit) != 0)
                    def _(bit=bit, off=off):
                        act(pltpu.make_async_copy(
                            zero_ref.at[pl.ds(0, bit * ROW_TILES), :],
                            xs_ref.at[pl.ds(pl.multiple_of(off * ROW_TILES, ROW_TILES), bit * ROW_TILES), :], zsem))
                    bit //= 2
            return carry

        lax.fori_loop(0, experts_per_step, pad_body, 0)

        def tail_body(r, carry):
            blk = nval_ref[0] + i * tail_per_step + r

            @pl.when(blk < nblocks)
            def _():
                rows = EXP_BLK * ROW_TILES
                act(pltpu.make_async_copy(zero_ref, xs_ref.at[pl.ds(pl.multiple_of(blk * rows, rows), rows), :],
                                          zsem))
            return carry

        lax.fori_loop(0, tail_per_step, tail_body, 0)

    zero_fill(lambda cp: cp.start())
    _rows_wait(xs_ref, tm * TOP_K, sem)
    zero_fill(lambda cp: cp.wait())


def _dispatch(pad0, padn, nval, dest, h2t, nrows):
    t = h2t.shape[0] // ROW_TILES
    tm = min(TM_DMA, t)
    nsteps = t // tm
    grid_spec = pltpu.PrefetchScalarGridSpec(
        num_scalar_prefetch=3,
        grid=(nsteps,),
        in_specs=[pl.BlockSpec((TOP_K, tm), lambda i, *_: (0, i), memory_space=pltpu.SMEM),
                  pl.BlockSpec((tm * ROW_TILES, LANES), lambda i, *_: (i, 0))],
        out_specs=pl.BlockSpec(memory_space=pl.ANY),
        scratch_shapes=[pltpu.VMEM((EXP_BLK * ROW_TILES, LANES), ROW_DT),
                        pltpu.SemaphoreType.DMA(()), pltpu.SemaphoreType.DMA(())],
    )
    return pl.pallas_call(
        functools.partial(_dispatch_kernel, nsteps),
        grid_spec=grid_spec,
        out_shape=jax.ShapeDtypeStruct((nrows * ROW_TILES, LANES), ROW_DT),
        compiler_params=_params("arbitrary"),
        name="dispatch",
    )(pad0, padn, nval, dest, h2t)


def _expert_kernel(bsrc_ref, nval_ref, first_ref, run_ref, rexp_ref, nruns_ref,
                   xs_ref, wg_hbm, wu_hbm, wd_hbm, ys_ref,
                   xb_ref, xw_ref, wg_f, wu_f, wd_f, wg_b, wu_b, wd_b, wsem):
    b = pl.program_id(0)
    nruns = nruns_ref[0]

    def weights(j, act):
        e = rexp_ref[j]
        s = j % W_SLOTS
        for hbm, buf in ((wg_hbm, wg_f), (wu_hbm, wu_f), (wd_hbm, wd_f)):
            act(pltpu.make_async_copy(hbm.at[e], buf.at[s], wsem.at[s]))

    @pl.when(b == 0)
    def _():
        weights(0, lambda cp: cp.start())

        @pl.when(nruns > 1)
        def _():
            weights(1, lambda cp: cp.start())

    @pl.when(b < nval_ref[0])
    def _():
        j = run_ref[b]

        @pl.when(first_ref[b] == 1)
        def _():
            weights(j, lambda cp: cp.wait())

            @pl.when(j + 2 < nruns)
            def _():
                weights(j + 2, lambda cp: cp.start())

            s = j % W_SLOTS
            wg_b[...] = wg_f[s].astype(BF16)
            wu_b[...] = wu_f[s].astype(BF16)
            wd_b[...] = wd_f[s].astype(BF16)

        xw_ref[...] = pltpu.bitcast(xs_ref[...], WORD_DT)
        for s in range(ROW_WORDS):
            for h, part in enumerate(_load_row_tiles(xw_ref, s)):
                c = 2 * s + h
                xb_ref[:, c * LANES:(c + 1) * LANES] = part.astype(BF16)
        x = xb_ref[...]
        g = _dot(x, wg_b[...])
        u = _dot(x, wu_b[...])
        act = (_silu(g) * u).astype(BF16)
        _store_row_tiles(ys_ref, _dot(act, wd_b[...]), xw_ref)

    @pl.when(b >= nval_ref[0])
    def _():
        ys_ref[...] = jnp.zeros_like(ys_ref)


def _experts(bexp, bsrc, nval, xs, wg, wu, wd):
    nrows = xs.shape[0] // ROW_TILES
    nb = nrows // EXP_BLK
    blk = (EXP_BLK * ROW_TILES, LANES)
    blocks = jnp.arange(nb, dtype=jnp.int32)
    first = ((blocks == 0) | (bexp != jnp.roll(bexp, 1))).astype(jnp.int32)
    run = jnp.cumsum(first) - 1
    runs = jnp.arange(N_EXPERTS, dtype=jnp.int32)
    rexp = jnp.sum(jnp.where((first[None, :] == 1) & (run[None, :] == runs[:, None]), bexp[None, :], 0), axis=1)
    nruns = (run[nb - 1] + 1).reshape(1)
    grid_spec = pltpu.PrefetchScalarGridSpec(
        num_scalar_prefetch=6,
        grid=(nb,),
        in_specs=[pl.BlockSpec(blk, lambda b, bs, *_: (bs[b], 0)),
                  pl.BlockSpec(memory_space=pl.ANY), pl.BlockSpec(memory_space=pl.ANY),
                  pl.BlockSpec(memory_space=pl.ANY)],
        out_specs=pl.BlockSpec(blk, lambda b, *_: (b, 0)),
        scratch_shapes=[pltpu.VMEM((EXP_BLK, D_MODEL), BF16),
                        pltpu.VMEM((EXP_BLK * ROW_WORDS, LANES), WORD_DT),
                        pltpu.VMEM((W_SLOTS, D_MODEL, D_EXPERT), F32),
                        pltpu.VMEM((W_SLOTS, D_MODEL, D_EXPERT), F32),
                        pltpu.VMEM((W_SLOTS, D_EXPERT, D_MODEL), F32),
                        pltpu.VMEM((D_MODEL, D_EXPERT), BF16),
                        pltpu.VMEM((D_MODEL, D_EXPERT), BF16),
                        pltpu.VMEM((D_EXPERT, D_MODEL), BF16),
                        pltpu.SemaphoreType.DMA((W_SLOTS,))],
    )
    return pl.pallas_call(
        _expert_kernel,
        grid_spec=grid_spec,
        out_shape=jax.ShapeDtypeStruct((nrows * ROW_TILES, LANES), ROW_DT),
        compiler_params=_params("arbitrary"),
        name="experts",
    )(bsrc, nval, first, run, rexp, nruns, xs, wg, wu, wd)


def _combine_kernel(dest_ref, ys_ref, gate_ref, x1_ref, h2_ref, mod_ref, sg_ref, su_ref, sd_ref, fg_ref,
                    out_ref, buf_ref, bufw_ref, x2_ref, sem):
    tm = dest_ref.shape[1]

    def body(j, carry):
        for k in range(TOP_K):
            _row_copy(ys_ref, dest_ref[k, j], buf_ref.at[k], j, sem).start(priority=k % 2)
        return carry

    lax.fori_loop(0, tm, body, 0)

    mod = mod_ref[0]
    h = h2_ref[...]
    act = (_silu(_dot(h, sg_ref[...])) * _dot(h, su_ref[...])).astype(BF16)
    moe = _dot(act, sd_ref[...])

    g = gate_ref[...]
    g0 = g.astype(BF16)
    r1 = g - g0.astype(F32)
    g1 = r1.astype(BF16)
    g2 = (r1 - g1.astype(F32)).astype(BF16)
    eye = (lax.broadcasted_iota(jnp.int32, (tm, tm), 0)
           == lax.broadcasted_iota(jnp.int32, (tm, tm), 1)).astype(BF16)
    dn = (((1,), (1,)), ((), ()))
    gcol = (lax.dot_general(eye, g0, dn, preferred_element_type=F32)
            + lax.dot_general(eye, g1, dn, preferred_element_type=F32)
            + lax.dot_general(eye, g2, dn, preferred_element_type=F32))

    gk = [jnp.broadcast_to(gcol[:, k:k + 1], (tm, LANES)) for k in range(TOP_K)]
    for k in range(TOP_K):
        _rows_wait(buf_ref.at[k], tm, sem)
    ssq = jnp.zeros((tm, 1), F32)
    for k in range(TOP_K):
        bufw_ref[k] = pltpu.bitcast(buf_ref[k], WORD_DT)
    for s in range(ROW_WORDS):
        parts = [_load_row_tiles(bufw_ref.at[k], s) for k in range(TOP_K)]
        for h in range(2):
            c = 2 * s + h
            sl = slice(c * LANES, (c + 1) * LANES)
            m = moe[:, sl]
            for k in range(TOP_K):
                m = m + gk[k] * parts[k][h]
            x2 = x1_ref[:, sl] + mod[5:6, sl] * m
            x2_ref[:, sl] = x2
            ssq = ssq + jnp.sum(x2 * x2, axis=-1, keepdims=True)
    out_ref[...] = x2_ref[...] * lax.rsqrt(ssq * (1.0 / D_MODEL) + EPS) * fg_ref[...]


def _combine(dest, ys, gate, x1, h2, mods, sg, su, sd, fg, seq):
    t = x1.shape[0]
    tm = min(TM_DMA, seq)
    spb = seq // tm
    row = lambda: pl.BlockSpec((tm, D_MODEL), lambda i: (i, 0))
    const = lambda s: pl.BlockSpec(s, lambda i: (0,) * len(s))
    return pl.pallas_call(
        _combine_kernel,
        grid=(t // tm,),
        in_specs=[pl.BlockSpec((TOP_K, tm), lambda i: (0, i), memory_space=pltpu.SMEM),
                  pl.BlockSpec(memory_space=pl.ANY),
                  pl.BlockSpec((TOP_K, tm), lambda i: (0, i)),
                  row(), row(),
                  pl.BlockSpec((1, 6, D_MODEL), lambda i: (i // spb, 0, 0)),
                  const((D_MODEL, D_SHARED)), const((D_MODEL, D_SHARED)), const((D_SHARED, D_MODEL)),
                  const((1, D_MODEL))],
        out_specs=row(),
        out_shape=jax.ShapeDtypeStruct((t, D_MODEL), F32),
        scratch_shapes=[pltpu.VMEM((TOP_K, tm * ROW_TILES, LANES), ROW_DT),
                        pltpu.VMEM((TOP_K, tm * ROW_WORDS, LANES), WORD_DT), pltpu.VMEM((tm, D_MODEL), F32),
                        pltpu.SemaphoreType.DMA(())],
        compiler_params=_params("arbitrary"),
        name="combine",
    )(dest, ys, gate, x1, h2, mods, sg, su, sd, fg)


def _rope_tables(seq):
    pos = jnp.arange(seq)
    n_freq = HEAD_DIM // 4
    inv = ROPE_THETA ** (-jnp.arange(n_freq, dtype=F32) / n_freq)
    ang_r = (pos // GRID_W)[:, None].astype(F32) * inv
    ang_c = (pos % GRID_W)[:, None].astype(F32) * inv
    cr, sr, cc, sc = jnp.cos(ang_r), jnp.sin(ang_r), jnp.cos(ang_c), jnp.sin(ang_c)
    cos = jnp.concatenate([cr, cr, cc, cc], axis=1)
    sin = jnp.concatenate([-sr, sr, -sc, sc], axis=1)
    reps = LANES // HEAD_DIM
    return jnp.tile(cos, (1, reps)), jnp.tile(sin, (1, reps))


def _layer(x2d, ctx2d, mods, batch, seq, norm1_g, norm2_g, w_in, ln_g, ln_b, gmlp_ws, gmlp_bs, sink,
           w_a, w_b, w_o, router_w, router_b, e_gate, e_up, e_down, s_gate, s_up, s_down, final_g):
    t = x2d.shape[0]
    g1 = norm1_g.reshape(1, D_MODEL)
    w_in_b = w_in.astype(BF16)
    cos, sin = _rope_tables(seq)

    kctx, vctx = _ctx_kv(ctx2d, mods, g1, w_in_b[:, C_K:C_GA])
    gu, vn, q, k, v, sga, sgb = _inproj(x2d, mods, g1, w_in_b, ln_g.reshape(1, D_A), ln_b.reshape(1, D_A),
                                        cos, sin, seq)

    bs_full = jnp.repeat(gmlp_bs.T, D_A // G_A, axis=1)
    a, o = _mix(sink, gu, vn, q, k, v, kctx, vctx, gmlp_ws.astype(BF16), bs_full, batch, seq)

    rwt = router_w.T
    rwh = rwt.astype(BF16)
    rwl = (rwt - rwh.astype(F32)).astype(BF16)
    x1, h2t, h2b, idx, gate, rank, counts = _merge(
        a, o, sga, sgb, x2d, mods, norm2_g.reshape(1, D_MODEL),
        w_a.astype(BF16), w_b.astype(BF16), w_o.astype(BF16), rwh, rwl, router_b.reshape(N_EXPERTS, 1), seq)

    cnt = counts[:, 0].astype(jnp.int32)
    pcnt = (cnt + EXP_BLK - 1) // EXP_BLK * EXP_BLK
    pend = jnp.cumsum(pcnt)
    pstart = pend - pcnt
    onehot = idx[:, :, None] == jnp.arange(N_EXPERTS, dtype=jnp.int32)
    dest = jnp.sum(jnp.where(onehot, pstart, 0), axis=-1) + rank
    nblocks = (t * TOP_K) // EXP_BLK + N_EXPERTS
    nval = (pend[-1] // EXP_BLK).astype(jnp.int32)
    bsrc = jnp.minimum(jnp.arange(nblocks, dtype=jnp.int32), nval - 1)
    bexp = jnp.sum((pend[None, :] <= (bsrc * EXP_BLK)[:, None]).astype(jnp.int32), axis=1)
    bexp = jnp.minimum(bexp, N_EXPERTS - 1)

    nval = nval.reshape(1)
    xs = _dispatch(pstart + cnt, pcnt - cnt, nval, dest, h2t, nblocks * EXP_BLK)
    ys = _experts(bexp, bsrc, nval, xs, e_gate, e_up, e_down)
    return _combine(dest, ys, gate, x1, h2b, mods, s_gate.astype(BF16), s_up.astype(BF16),
                    s_down.astype(BF16), final_g.reshape(1, D_MODEL), seq)


def kernel(x, c, ctx, c_ctx, ada_w, ada_b, norm1_g, norm2_g, w_in, gmlp_ln_g, gmlp_ln_b, gmlp_ws, gmlp_bs,
           attn_sink, w_branch_a, w_branch_b, w_out, router_w, router_b, exp_w_gate, exp_w_up, exp_w_down,
           sh_w_gate, sh_w_up, sh_w_down, final_g):
    batch, seq, _ = x.shape
    depth = ada_w.shape[0]
    assert depth == 1, "the context stream is only carried as keys/values of a single layer"
    assert batch + 1 <= 8 and seq % WBLK == 0
    cond = jnp.concatenate([c, c_ctx[None], jnp.zeros((8 - batch - 1, D_MODEL), F32)], axis=0)
    mods = _ada(cond, ada_w[0], ada_b[0])[:batch + 1].reshape(batch + 1, 6, D_MODEL)
    out = _layer(x.reshape(batch * seq, D_MODEL), ctx.reshape(-1, D_MODEL), mods, batch, seq,
                 norm1_g[0], norm2_g[0], w_in[0], gmlp_ln_g[0], gmlp_ln_b[0], gmlp_ws[0], gmlp_bs[0],
                 attn_sink[0], w_branch_a[0], w_branch_b[0], w_out[0], router_w[0], router_b[0],
                 exp_w_gate[0], exp_w_up[0], exp_w_down[0], sh_w_gate[0], sh_w_up[0], sh_w_down[0], final_g)
    return out.reshape(batch, seq, D_MODEL)
```

```python
import functools

import jax
import jax.numpy as jnp
from jax import lax
from jax.experimental import pallas as pl
from jax.experimental.pallas import tpu as pltpu

F32 = jnp.float32
BF16 = jnp.bfloat16

D_MODEL = 1024
EPS = 1e-6
GRID_W = 64
D_A = D_MODEL // 2
G_A = 4
CHUNK = 128
N_HEADS = 8
N_KV = 2
REP = N_HEADS // N_KV
HEAD_DIM = 64
D_Q = N_HEADS * HEAD_DIM
D_KV = N_KV * HEAD_DIM
WBLK = 128
ROPE_THETA = 10000.0
N_EXPERTS = 256
TOP_K = 8
D_EXPERT = D_MODEL // 4
D_SHARED = D_MODEL // 4
ROUTE_SCALE = 2.5

C_U = 0
C_V = D_A
C_Q = 2 * D_A
C_K = C_Q + D_Q
C_VAL = C_K + D_KV
C_GA = C_VAL + D_KV
C_GB = C_GA + D_MODEL
D_IN = C_GB + D_MODEL

LANES = 128
ROPE_HALF = HEAD_DIM // 4
NEG_BIG = -1e30

TM_PROJ = 512
TM_MERGE = 512
TM_DMA = 256
EXP_BLK = 256
W_SLOTS = 3
ROW_DT = BF16
ROW_TILES = D_MODEL // LANES
ROW_WORDS = ROW_TILES // 2
WORD_DT = jnp.uint32
VMEM_LIMIT = 56 * 1024 * 1024


def _gelu(x):
    return 0.5 * x * (1.0 + jnp.tanh(0.7978845608028654 * (x + 0.044715 * x * x * x)))


def _silu(x):
    return x * jax.nn.sigmoid(x)


def _dot(a, b):
    return jnp.dot(a, b, preferred_element_type=F32)


def _rms_mod(x, g, shift, scale):
    ms = jnp.mean(x * x, axis=-1, keepdims=True)
    return (x * lax.rsqrt(ms + EPS)) * g * (1.0 + scale) + shift


def _params(*sem):
    return pltpu.CompilerParams(dimension_semantics=sem, vmem_limit_bytes=VMEM_LIMIT)


def _ada_kernel(c_ref, w_ref, b_ref, o_ref):
    c = c_ref[...]
    s = _silu(c).astype(BF16)
    o_ref[...] = _dot(s, w_ref[...].astype(BF16)) + b_ref[...]


def _ada(cond8, ada_w, ada_b):
    n = ada_w.shape[1]
    tn = 1536
    return pl.pallas_call(
        _ada_kernel,
        grid=(n // tn,),
        in_specs=[pl.BlockSpec((8, D_MODEL), lambda j: (0, 0)),
                  pl.BlockSpec((D_MODEL, tn), lambda j: (0, j)),
                  pl.BlockSpec((1, tn), lambda j: (0, j))],
        out_specs=pl.BlockSpec((8, tn), lambda j: (0, j)),
        out_shape=jax.ShapeDtypeStruct((8, n), F32),
        compiler_params=_params("arbitrary"),
        name="ada",
    )(cond8, ada_w, ada_b.reshape(1, n))


def _ctx_kernel(x_ref, mod_ref, g_ref, w_ref, k_ref, v_ref):
    mod = mod_ref[0]
    h = _rms_mod(x_ref[...], g_ref[...], mod[0:1], mod[1:2]).astype(BF16)
    z = _dot(h, w_ref[...])
    k_ref[...] = z[:, :D_KV].astype(BF16)
    v_ref[...] = z[:, D_KV:].astype(BF16)


def _ctx_kv(ctx2d, mods, g1, w_kv):
    n = ctx2d.shape[0]
    nb = mods.shape[0] - 1
    return pl.pallas_call(
        _ctx_kernel,
        grid=(1,),
        in_specs=[pl.BlockSpec((n, D_MODEL), lambda i: (0, 0)),
                  pl.BlockSpec((1, 6, D_MODEL), lambda i: (nb, 0, 0)),
                  pl.BlockSpec((1, D_MODEL), lambda i: (0, 0)),
                  pl.BlockSpec((D_MODEL, 2 * D_KV), lambda i: (0, 0))],
        out_specs=[pl.BlockSpec((n, D_KV), lambda i: (0, 0)),
                   pl.BlockSpec((n, D_KV), lambda i: (0, 0))],
        out_shape=[jax.ShapeDtypeStruct((n, D_KV), BF16)] * 2,
        compiler_params=_params("arbitrary"),
        name="ctx_kv",
    )(ctx2d, mods, g1, w_kv)


def _rope(t, cos, sin):
    lane = lax.broadcasted_iota(jnp.int32, (t.shape[0], LANES), 1)
    first = (lane & (2 * ROPE_HALF - 1)) < ROPE_HALF
    outs = []
    for j in range(t.shape[1] // LANES):
        tj = t[:, j * LANES:(j + 1) * LANES]
        up = pltpu.roll(tj, LANES - ROPE_HALF, 1)
        dn = pltpu.roll(tj, ROPE_HALF, 1)
        outs.append(tj * cos + jnp.where(first, up, dn) * sin)
    return outs


def _inproj_kernel(x_ref, mod_ref, g_ref, w_ref, lng_ref, lnb_ref, cos_ref, sin_ref,
                   gu_ref, vn_ref, q_ref, k_ref, v_ref, sga_ref, sgb_ref):
    mod = mod_ref[0]
    h = _rms_mod(x_ref[...], g_ref[...], mod[0:1], mod[1:2]).astype(BF16)

    def proj(lo, hi):
        return _dot(h, w_ref[:, lo:hi])

    gu_ref[...] = _gelu(proj(C_U, C_V)).astype(BF16)

    v = _gelu(proj(C_V, C_Q))
    mu = jnp.mean(v, axis=-1, keepdims=True)
    vc = v - mu
    var = jnp.mean(vc * vc, axis=-1, keepdims=True)
    vn_ref[...] = (vc * lax.rsqrt(var + EPS) * lng_ref[...] + lnb_ref[...]).astype(BF16)

    cos = cos_ref[...]
    sin = sin_ref[...]
    q = _rope(proj(C_Q, C_K) * (HEAD_DIM ** -0.5), cos, sin)
    for j, qj in enumerate(q):
        q_ref[:, j * LANES:(j + 1) * LANES] = qj.astype(BF16)
    k = _rope(proj(C_K, C_VAL), cos, sin)
    k_ref[...] = k[0].astype(BF16)
    v_ref[...] = proj(C_VAL, C_GA).astype(BF16)
    sga_ref[...] = jax.nn.sigmoid(proj(C_GA, C_GB)).astype(BF16)
    sgb_ref[...] = jax.nn.sigmoid(proj(C_GB, D_IN)).astype(BF16)


def _inproj(x2d, mods, g1, w_in, lng, lnb, cos, sin, seq):
    t = x2d.shape[0]
    tm = min(TM_PROJ, seq)
    spb = seq // tm
    row = lambda w: pl.BlockSpec((tm, w), lambda i: (i, 0))
    const = lambda s: pl.BlockSpec(s, lambda i: (0,) * len(s))
    return pl.pallas_call(
        _inproj_kernel,
        grid=(t // tm,),
        in_specs=[row(D_MODEL),
                  pl.BlockSpec((1, 6, D_MODEL), lambda i: (i // spb, 0, 0)),
                  const((1, D_MODEL)),
                  const((D_MODEL, D_IN)),
                  const((1, D_A)), const((1, D_A)),
                  pl.BlockSpec((tm, LANES), lambda i: (i % spb, 0)),
                  pl.BlockSpec((tm, LANES), lambda i: (i % spb, 0))],
        out_specs=[row(D_A), row(D_A), row(D_Q), row(D_KV), row(D_KV), row(D_MODEL), row(D_MODEL)],
        out_shape=[jax.ShapeDtypeStruct((t, w), BF16)
                   for w in (D_A, D_A, D_Q, D_KV, D_KV, D_MODEL, D_MODEL)],
        compiler_params=_params("arbitrary"),
        name="inproj",
    )(x2d, mods, g1, w_in, lng, lnb, cos, sin)


def _mix_kernel(sink_ref, gu_ref, vn_ref, q_ref, kp_ref, kc_ref, kn_ref, vp_ref, vc_ref, vx_ref,
                kctx_ref, vctx_ref, ws_ref, bs_ref, mask_ref, a_ref, o_ref):
    for g in range(G_A):
        sl = slice(g * CHUNK, (g + 1) * CHUNK)
        s = _dot(ws_ref[g], vn_ref[:, sl]) + bs_ref[:, sl]
        a_ref[:, sl] = (gu_ref[:, sl].astype(F32) * s).astype(BF16)

    kcat = jnp.concatenate([kp_ref[...], kc_ref[...], kn_ref[...], kctx_ref[...]], axis=0)
    vcat = jnp.concatenate([vp_ref[...], vc_ref[...], vx_ref[...], vctx_ref[...]], axis=0)
    rows = REP * WBLK
    mask = mask_ref[0]
    rgrp = lax.broadcasted_iota(jnp.int32, (rows, 1), 0) // WBLK

    for kvh in range(N_KV):
        ksl = kcat[:, kvh * HEAD_DIM:(kvh + 1) * HEAD_DIM]
        vsl = vcat[:, kvh * HEAD_DIM:(kvh + 1) * HEAD_DIM]
        qs = jnp.concatenate(
            [q_ref[:, (kvh * REP + r) * HEAD_DIM:(kvh * REP + r + 1) * HEAD_DIM] for r in range(REP)],
            axis=0)
        sink = jnp.zeros((rows, 1), F32)
        for r in range(REP):
            sink = jnp.where(rgrp == r, sink_ref[kvh * REP + r], sink)
        s = lax.dot_general(qs, ksl, (((1,), (1,)), ((), ())), preferred_element_type=F32)
        s = s + mask
        m = jnp.maximum(jnp.max(s, axis=-1, keepdims=True), sink)
        p = jnp.exp(s - m)
        den = jnp.sum(p, axis=-1, keepdims=True) + jnp.exp(sink - m)
        o = _dot(p.astype(BF16), vsl) / den
        for r in range(0, REP, 2):
            pair = jnp.concatenate([o[r * WBLK:(r + 1) * WBLK], o[(r + 1) * WBLK:(r + 2) * WBLK]], axis=1)
            c0 = (kvh * REP + r) * HEAD_DIM
            o_ref[:, c0:c0 + 2 * HEAD_DIM] = pair.astype(BF16)


def _attn_masks(nctx):
    nwin = 3 * WBLK
    qi = (jnp.arange(REP * WBLK) & (WBLK - 1))[:, None]
    kj = jnp.arange(nwin + nctx)[None, :]
    band = (kj >= qi) & (kj <= qi + 2 * WBLK)
    masks = []
    for first in (False, True):
        for last in (False, True):
            lo = WBLK if first else 0
            hi = 2 * WBLK if last else nwin
            valid = (kj >= nwin) | (band & (kj >= lo) & (kj < hi))
            masks.append(jnp.where(valid, 0.0, NEG_BIG))
    return jnp.stack(masks).astype(F32)


def _mix(sink, gu, vn, q, k, v, kctx, vctx, ws, bs_full, batch, seq):
    t = gu.shape[0]
    nblk = seq // WBLK
    nctx = kctx.shape[0] // batch
    masks = _attn_masks(nctx)
    mspec = pl.BlockSpec((1,) + masks.shape[1:],
                         lambda b, n: (jnp.where(n == 0, 2, 0) + jnp.where(n == nblk - 1, 1, 0), 0, 0))
    cur = lambda w: pl.BlockSpec((WBLK, w), lambda b, n: (b * nblk + n, 0))
    prev = pl.BlockSpec((WBLK, D_KV), lambda b, n: (b * nblk + jnp.maximum(n - 1, 0), 0))
    nxt = pl.BlockSpec((WBLK, D_KV), lambda b, n: (b * nblk + jnp.minimum(n + 1, nblk - 1), 0))
    same = pl.BlockSpec((WBLK, D_KV), lambda b, n: (b * nblk + n, 0))
    cblk = pl.BlockSpec((nctx, D_KV), lambda b, n: (b, 0))
    return pl.pallas_call(
        _mix_kernel,
        grid=(batch, nblk),
        in_specs=[pl.BlockSpec(memory_space=pltpu.SMEM),
                  cur(D_A), cur(D_A), cur(D_Q),
                  prev, same, nxt, prev, same, nxt,
                  cblk, cblk,
                  pl.BlockSpec((G_A, CHUNK, CHUNK), lambda b, n: (0, 0, 0)),
                  pl.BlockSpec((CHUNK, D_A), lambda b, n: (0, 0)),
                  mspec],
        out_specs=[cur(D_A), cur(D_Q)],
        out_shape=[jax.ShapeDtypeStruct((t, D_A), BF16), jax.ShapeDtypeStruct((t, D_Q), BF16)],
        compiler_params=_params("arbitrary", "arbitrary"),
        name="mix",
    )(sink, gu, vn, q, k, k, k, v, v, v, kctx, vctx, ws, bs_full, masks)


def _store_row_tiles(ref, val, wref):
    rows = val.shape[0]
    for s in range(ROW_WORDS):
        lo = val[:, (2 * s) * LANES:(2 * s + 1) * LANES]
        hi = val[:, (2 * s + 1) * LANES:(2 * s + 2) * LANES]
        wref[pl.ds(s, rows, stride=ROW_WORDS), :] = pltpu.pack_elementwise([lo, hi], packed_dtype=BF16)
    ref[...] = pltpu.bitcast(wref[...], ROW_DT)


def _load_row_tiles(wref, s):
    words = wref[pl.ds(s, wref.shape[0] // ROW_WORDS, stride=ROW_WORDS), :]
    return tuple(pltpu.unpack_elementwise(words, index=i, packed_dtype=BF16, unpacked_dtype=F32) for i in (0, 1))


def _merge_kernel(a_ref, o_ref, sga_ref, sgb_ref, x_ref, mod_ref, g2_ref, wa_ref, wb_ref, wo_ref,
                  rwh_ref, rwl_ref, rb_ref,
                  x1_ref, h2t_ref, h2b_ref, idx_ref, gate_ref, rank_ref, cnt_ref, base_ref, words_ref):
    i = pl.program_id(0)

    @pl.when(i == 0)
    def _():
        base_ref[...] = jnp.zeros_like(base_ref)

    mod = mod_ref[0]
    ya = _dot(a_ref[...], wa_ref[...])
    yb = _dot(o_ref[...], wb_ref[...])
    y = sga_ref[...].astype(F32) * ya + sgb_ref[...].astype(F32) * yb
    x1 = x_ref[...] + mod[2:3] * _dot(y.astype(BF16), wo_ref[...])
    x1_ref[...] = x1
    h2 = _rms_mod(x1, g2_ref[...], mod[3:4], mod[4:5])
    h2b_ref[...] = h2.astype(BF16)
    _store_row_tiles(h2t_ref, h2, words_ref)

    hh = h2.astype(BF16)
    hl = (h2 - hh.astype(F32)).astype(BF16)
    dn = (((1,), (1,)), ((), ()))
    logits = (lax.dot_general(rwh_ref[...], hh, dn, preferred_element_type=F32)
              + lax.dot_general(rwl_ref[...], hh, dn, preferred_element_type=F32)
              + lax.dot_general(rwh_ref[...], hl, dn, preferred_element_type=F32))
    scores = jax.nn.sigmoid(logits)
    tm = scores.shape[1]
    eio = lax.broadcasted_iota(jnp.int32, scores.shape, 0).astype(F32)
    work = scores + rb_ref[...]
    picked = jnp.zeros(scores.shape, F32)
    idxs, vals = [], []
    for _ in range(TOP_K):
        m = jnp.max(work, axis=0, keepdims=True)
        ik = jnp.min(jnp.where(work == m, eio, float(N_EXPERTS)), axis=0, keepdims=True)
        oh = eio == ik
        vals.append(jnp.sum(jnp.where(oh, scores, 0.0), axis=0, keepdims=True))
        idxs.append(ik)
        work = jnp.where(oh, -jnp.inf, work)
        picked = picked + oh.astype(F32)
    total = vals[0]
    for vk in vals[1:]:
        total = total + vk

    tr = lax.broadcasted_iota(jnp.int32, (tm, tm), 0)
    tc = lax.broadcasted_iota(jnp.int32, (tm, tm), 1)
    before = (tr < tc).astype(BF16)
    base = base_ref[...]
    rank_e = _dot(picked.astype(BF16), before) + base
    ranks = [jnp.sum(jnp.where(eio == ik, rank_e, 0.0), axis=0, keepdims=True) for ik in idxs]
    base = base + jnp.sum(picked, axis=1, keepdims=True)
    base_ref[...] = base
    cnt_ref[...] = base

    idx_ref[...] = jnp.concatenate(idxs, axis=0).astype(jnp.int32)
    rank_ref[...] = jnp.concatenate(ranks, axis=0).astype(jnp.int32)
    gate_ref[...] = jnp.concatenate(vals, axis=0) * (ROUTE_SCALE / total)


def _merge(a, o, sga, sgb, x2d, mods, g2, wa, wb, wo, rwh, rwl, rb, seq):
    t = x2d.shape[0]
    tm = min(TM_MERGE, seq)
    spb = seq // tm
    row = lambda w: pl.BlockSpec((tm, w), lambda i: (i, 0))
    col = lambda: pl.BlockSpec((TOP_K, tm), lambda i: (0, i))
    const = lambda s: pl.BlockSpec(s, lambda i: (0,) * len(s))
    return pl.pallas_call(
        _merge_kernel,
        grid=(t // tm,),
        in_specs=[row(D_A), row(D_Q), row(D_MODEL), row(D_MODEL), row(D_MODEL),
                  pl.BlockSpec((1, 6, D_MODEL), lambda i: (i // spb, 0, 0)),
                  const((1, D_MODEL)),
                  const((D_A, D_MODEL)), const((D_Q, D_MODEL)), const((D_MODEL, D_MODEL)),
                  const((N_EXPERTS, D_MODEL)), const((N_EXPERTS, D_MODEL)), const((N_EXPERTS, 1))],
        out_specs=[row(D_MODEL), pl.BlockSpec((tm * ROW_TILES, LANES), lambda i: (i, 0)), row(D_MODEL),
                   col(), col(), col(), const((N_EXPERTS, 1))],
        out_shape=[jax.ShapeDtypeStruct((t, D_MODEL), F32),
                   jax.ShapeDtypeStruct((t * ROW_TILES, LANES), ROW_DT),
                   jax.ShapeDtypeStruct((t, D_MODEL), BF16),
                   jax.ShapeDtypeStruct((TOP_K, t), jnp.int32),
                   jax.ShapeDtypeStruct((TOP_K, t), F32),
                   jax.ShapeDtypeStruct((TOP_K, t), jnp.int32),
                   jax.ShapeDtypeStruct((N_EXPERTS, 1), F32)],
        scratch_shapes=[pltpu.VMEM((N_EXPERTS, 1), F32), pltpu.VMEM((tm * ROW_WORDS, LANES), WORD_DT)],
        compiler_params=_params("arbitrary"),
        name="merge",
    )(a, o, sga, sgb, x2d, mods, g2, wa, wb, wo, rwh, rwl, rb)


def _row_copy(src, s_off, dst, d_off, sem):
    return pltpu.make_async_copy(src.at[pl.ds(pl.multiple_of(s_off, ROW_TILES), ROW_TILES), :],
                                 dst.at[pl.ds(pl.multiple_of(d_off, ROW_TILES), ROW_TILES), :], sem)


def _rows_wait(ref, nrows, sem):
    n = nrows * ROW_TILES
    pltpu.make_async_copy(ref.at[pl.ds(0, n), :], ref.at[pl.ds(0, n), :], sem).wait()


def _dispatch_kernel(nsteps, pad0_ref, padn_ref, nval_ref, dest_ref, h_ref, xs_ref, zero_ref, sem, zsem):
    i = pl.program_id(0)
    tm = h_ref.shape[0] // ROW_TILES
    nblocks = xs_ref.shape[0] // (EXP_BLK * ROW_TILES)
    experts_per_step = -(-N_EXPERTS // nsteps)
    tail_per_step = -(-nblocks // nsteps)

    @pl.when(i == 0)
    def _():
        zero_ref[...] = jnp.zeros_like(zero_ref)

    def body(j, carry):
        for k in range(TOP_K):
            _row_copy(h_ref, j * ROW_TILES, xs_ref, dest_ref[0, 0, j * TOP_K + k], sem).start(priority=k % 2)
        return carry

    lax.fori_loop(0, tm, body, 0)

    def zero_fill(act):
        def pad_body(r, carry):
            e = i * experts_per_step + r

            @pl.when(e < N_EXPERTS)
            def _():
                first = pad0_ref[e]
                n = padn_ref[e]
                bit = EXP_BLK // 2
                while bit:
                    off = first + (n & ~(2 * bit - 1))

                    @pl.when((n & bit) != 0)
                    def _(bit=bit, off=off):
                        act(pltpu.make_async_copy(
                            zero_ref.at[pl.ds(0, bit * ROW_TILES), :],
                            xs_ref.at[pl.ds(pl.multiple_of(off * ROW_TILES, ROW_TILES), bit * ROW_TILES), :], zsem))
                    bit //= 2
            return carry

        lax.fori_loop(0, experts_per_step, pad_body, 0)

        def tail_body(r, carry):
            blk = nval_ref[0] + i * tail_per_step + r

            @pl.when(blk < nblocks)
            def _():
                rows = EXP_BLK * ROW_TILES
                act(pltpu.make_async_copy(zero_ref, xs_ref.at[pl.ds(pl.multiple_of(blk * rows, rows), rows), :],
                                          zsem))
            return carry

        lax.fori_loop(0, tail_per_step, tail_body, 0)

    zero_fill(lambda cp: cp.start())
    _rows_wait(xs_ref, tm * TOP_K, sem)
    zero_fill(lambda cp: cp.wait())


def _dispatch(pad0, padn, nval, dest, h2t, nrows):
    t = h2t.shape[0] // ROW_TILES
    tm = dest.shape[2] // TOP_K
    nsteps = t // tm
    grid_spec = pltpu.PrefetchScalarGridSpec(
        num_scalar_prefetch=3,
        grid=(nsteps,),
        in_specs=[pl.BlockSpec((1, 1, tm * TOP_K), lambda i, *_: (i, 0, 0), memory_space=pltpu.SMEM),
                  pl.BlockSpec((tm * ROW_TILES, LANES), lambda i, *_: (i, 0))],
        out_specs=pl.BlockSpec(memory_space=pl.ANY),
        scratch_shapes=[pltpu.VMEM((EXP_BLK * ROW_TILES, LANES), ROW_DT),
                        pltpu.SemaphoreType.DMA(()), pltpu.SemaphoreType.DMA(())],
    )
    return pl.pallas_call(
        functools.partial(_dispatch_kernel, nsteps),
        grid_spec=grid_spec,
        out_shape=jax.ShapeDtypeStruct((nrows * ROW_TILES, LANES), ROW_DT),
        compiler_params=_params("arbitrary"),
        name="dispatch",
    )(pad0, padn, nval, dest, h2t)


def _expert_kernel(bsrc_ref, nval_ref, first_ref, run_ref, rexp_ref, nruns_ref,
                   xs_ref, wg_hbm, wu_hbm, wd_hbm, ys_ref,
                   xb_ref, xw_ref, wg_f, wu_f, wd_f, wg_b, wu_b, wd_b, wsem):
    b = pl.program_id(0)
    nruns = nruns_ref[0]

    def weights(j, act):
        e = rexp_ref[j]
        s = j % W_SLOTS
        for hbm, buf in ((wg_hbm, wg_f), (wu_hbm, wu_f), (wd_hbm, wd_f)):
            act(pltpu.make_async_copy(hbm.at[e], buf.at[s], wsem.at[s]))

    @pl.when(b == 0)
    def _():
        weights(0, lambda cp: cp.start())

        @pl.when(nruns > 1)
        def _():
            weights(1, lambda cp: cp.start())

    @pl.when(b < nval_ref[0])
    def _():
        j = run_ref[b]

        @pl.when(first_ref[b] == 1)
        def _():
            weights(j, lambda cp: cp.wait())

            @pl.when(j + 2 < nruns)
            def _():
                weights(j + 2, lambda cp: cp.start())

            s = j % W_SLOTS
            wg_b[...] = wg_f[s].astype(BF16)
            wu_b[...] = wu_f[s].astype(BF16)
            wd_b[...] = wd_f[s].astype(BF16)

        xw_ref[...] = pltpu.bitcast(xs_ref[...], WORD_DT)
        for s in range(ROW_WORDS):
            for h, part in enumerate(_load_row_tiles(xw_ref, s)):
                c = 2 * s + h
                xb_ref[:, c * LANES:(c + 1) * LANES] = part.astype(BF16)
        x = xb_ref[...]
        g = _dot(x, wg_b[...])
        u = _dot(x, wu_b[...])
        act = (_silu(g) * u).astype(BF16)
        _store_row_tiles(ys_ref, _dot(act, wd_b[...]), xw_ref)

    @pl.when(b >= nval_ref[0])
    def _():
        ys_ref[...] = jnp.zeros_like(ys_ref)


def _experts(bexp, bsrc, nval, xs, wg, wu, wd):
    nrows = xs.shape[0] // ROW_TILES
    nb = nrows // EXP_BLK
    blk = (EXP_BLK * ROW_TILES, LANES)
    blocks = jnp.arange(nb, dtype=jnp.int32)
    first = ((blocks == 0) | (bexp != jnp.roll(bexp, 1))).astype(jnp.int32)
    run = jnp.cumsum(first) - 1
    runs = jnp.arange(N_EXPERTS, dtype=jnp.int32)
    rexp = jnp.sum(jnp.where((first[None, :] == 1) & (run[None, :] == runs[:, None]), bexp[None, :], 0), axis=1)
    nruns = (run[nb - 1] + 1).reshape(1)
    grid_spec = pltpu.PrefetchScalarGridSpec(
        num_scalar_prefetch=6,
        grid=(nb,),
        in_specs=[pl.BlockSpec(blk, lambda b, bs, *_: (bs[b], 0)),
                  pl.BlockSpec(memory_space=pl.ANY), pl.BlockSpec(memory_space=pl.ANY),
                  pl.BlockSpec(memory_space=pl.ANY)],
        out_specs=pl.BlockSpec(blk, lambda b, *_: (b, 0)),
        scratch_shapes=[pltpu.VMEM((EXP_BLK, D_MODEL), BF16),
                        pltpu.VMEM((EXP_BLK * ROW_WORDS, LANES), WORD_DT),
                        pltpu.VMEM((W_SLOTS, D_MODEL, D_EXPERT), F32),
                        pltpu.VMEM((W_SLOTS, D_MODEL, D_EXPERT), F32),
                        pltpu.VMEM((W_SLOTS, D_EXPERT, D_MODEL), F32),
                        pltpu.VMEM((D_MODEL, D_EXPERT), BF16),
                        pltpu.VMEM((D_MODEL, D_EXPERT), BF16),
                        pltpu.VMEM((D_EXPERT, D_MODEL), BF16),
                        pltpu.SemaphoreType.DMA((W_SLOTS,))],
    )
    return pl.pallas_call(
        _expert_kernel,
        grid_spec=grid_spec,
        out_shape=jax.ShapeDtypeStruct((nrows * ROW_TILES, LANES), ROW_DT),
        compiler_params=_params("arbitrary"),
        name="experts",
    )(bsrc, nval, first, run, rexp, nruns, xs, wg, wu, wd)


def _combine_kernel(dest_ref, next_ref, ys_ref, gate_ref, x1_ref, h2_ref, mod_ref, sg_ref, su_ref, sd_ref, fg_ref,
                    out_ref, buf_ref, bufw_ref, x2_ref, sem):
    i = pl.program_id(0)
    tm = x1_ref.shape[0]
    slot = i % 2

    def gather(ids_ref, s):
        def body(j, carry):
            for k in range(TOP_K):
                _row_copy(ys_ref, ids_ref[0, 0, j * TOP_K + k], buf_ref.at[s, k], j * ROW_TILES,
                          sem.at[s]).start(priority=k % 2)
            return carry
        lax.fori_loop(0, tm, body, 0)

    @pl.when(i == 0)
    def _():
        gather(dest_ref, 0)

    @pl.when(i + 1 < pl.num_programs(0))
    def _():
        gather(next_ref, 1 - slot)

    mod = mod_ref[0]
    h = h2_ref[...]
    act = (_silu(_dot(h, sg_ref[...])) * _dot(h, su_ref[...])).astype(BF16)
    moe = _dot(act, sd_ref[...])

    g = gate_ref[...]
    g0 = g.astype(BF16)
    r1 = g - g0.astype(F32)
    g1 = r1.astype(BF16)
    g2 = (r1 - g1.astype(F32)).astype(BF16)
    eye = (lax.broadcasted_iota(jnp.int32, (tm, tm), 0)
           == lax.broadcasted_iota(jnp.int32, (tm, tm), 1)).astype(BF16)
    dn = (((1,), (1,)), ((), ()))
    gcol = (lax.dot_general(eye, g0, dn, preferred_element_type=F32)
            + lax.dot_general(eye, g1, dn, preferred_element_type=F32)
            + lax.dot_general(eye, g2, dn, preferred_element_type=F32))

    gk = [jnp.broadcast_to(gcol[:, k:k + 1], (tm, LANES)) for k in range(TOP_K)]
    for k in range(TOP_K):
        _rows_wait(buf_ref.at[slot, k], tm, sem.at[slot])
    ssq = jnp.zeros((tm, 1), F32)
    for k in range(TOP_K):
        bufw_ref[k] = pltpu.bitcast(buf_ref[slot, k], WORD_DT)
    for s in range(ROW_WORDS):
        parts = [_load_row_tiles(bufw_ref.at[k], s) for k in range(TOP_K)]
        for h in range(2):
            c = 2 * s + h
            sl = slice(c * LANES, (c + 1) * LANES)
            m = moe[:, sl]
            for k in range(TOP_K):
                m = m + gk[k] * parts[k][h]
            x2 = x1_ref[:, sl] + mod[5:6, sl] * m
            x2_ref[:, sl] = x2
            ssq = ssq + jnp.sum(x2 * x2, axis=-1, keepdims=True)
    out_ref[...] = x2_ref[...] * lax.rsqrt(ssq * (1.0 / D_MODEL) + EPS) * fg_ref[...]


def _combine(dest, ys, gate, x1, h2, mods, sg, su, sd, fg, seq):
    t = x1.shape[0]
    tm = dest.shape[2] // TOP_K
    spb = seq // tm
    nt = t // tm
    row = lambda: pl.BlockSpec((tm, D_MODEL), lambda i: (i, 0))
    const = lambda s: pl.BlockSpec(s, lambda i: (0,) * len(s))
    ids = lambda f: pl.BlockSpec((1, 1, tm * TOP_K), f, memory_space=pltpu.SMEM)
    return pl.pallas_call(
        _combine_kernel,
        grid=(nt,),
        in_specs=[ids(lambda i: (i, 0, 0)), ids(lambda i: (jnp.minimum(i + 1, nt - 1), 0, 0)),
                  pl.BlockSpec(memory_space=pl.ANY),
                  pl.BlockSpec((TOP_K, tm), lambda i: (0, i)),
                  row(), row(),
                  pl.BlockSpec((1, 6, D_MODEL), lambda i: (i // spb, 0, 0)),
                  const((D_MODEL, D_SHARED)), const((D_MODEL, D_SHARED)), const((D_SHARED, D_MODEL)),
                  const((1, D_MODEL))],
        out_specs=row(),
        out_shape=jax.ShapeDtypeStruct((t, D_MODEL), F32),
        scratch_shapes=[pltpu.VMEM((2, TOP_K, tm * ROW_TILES, LANES), ROW_DT),
                        pltpu.VMEM((TOP_K, tm * ROW_WORDS, LANES), WORD_DT), pltpu.VMEM((tm, D_MODEL), F32),
                        pltpu.SemaphoreType.DMA((2,))],
        compiler_params=_params("arbitrary"),
        name="combine",
    )(dest, dest, ys, gate, x1, h2, mods, sg, su, sd, fg)


def _rope_tables(seq):
    pos = jnp.arange(seq)
    n_freq = HEAD_DIM // 4
    inv = ROPE_THETA ** (-jnp.arange(n_freq, dtype=F32) / n_freq)
    ang_r = (pos // GRID_W)[:, None].astype(F32) * inv
    ang_c = (pos % GRID_W)[:, None].astype(F32) * inv
    cr, sr, cc, sc = jnp.cos(ang_r), jnp.sin(ang_r), jnp.cos(ang_c), jnp.sin(ang_c)
    cos = jnp.concatenate([cr, cr, cc, cc], axis=1)
    sin = jnp.concatenate([-sr, sr, -sc, sc], axis=1)
    reps = LANES // HEAD_DIM
    return jnp.tile(cos, (1, reps)), jnp.tile(sin, (1, reps))


def _layer(x2d, ctx2d, mods, batch, seq, norm1_g, norm2_g, w_in, ln_g, ln_b, gmlp_ws, gmlp_bs, sink,
           w_a, w_b, w_o, router_w, router_b, e_gate, e_up, e_down, s_gate, s_up, s_down, final_g):
    t = x2d.shape[0]
    g1 = norm1_g.reshape(1, D_MODEL)
    w_in_b = w_in.astype(BF16)
    cos, sin = _rope_tables(seq)

    kctx, vctx = _ctx_kv(ctx2d, mods, g1, w_in_b[:, C_K:C_GA])
    gu, vn, q, k, v, sga, sgb = _inproj(x2d, mods, g1, w_in_b, ln_g.reshape(1, D_A), ln_b.reshape(1, D_A),
                                        cos, sin, seq)

    bs_full = jnp.repeat(gmlp_bs.T, D_A // G_A, axis=1)
    a, o = _mix(sink, gu, vn, q, k, v, kctx, vctx, gmlp_ws.astype(BF16), bs_full, batch, seq)

    rwt = router_w.T
    rwh = rwt.astype(BF16)
    rwl = (rwt - rwh.astype(F32)).astype(BF16)
    x1, h2t, h2b, idx, gate, rank, counts = _merge(
        a, o, sga, sgb, x2d, mods, norm2_g.reshape(1, D_MODEL),
        w_a.astype(BF16), w_b.astype(BF16), w_o.astype(BF16), rwh, rwl, router_b.reshape(N_EXPERTS, 1), seq)

    cnt = counts[:, 0].astype(jnp.int32)
    pcnt = (cnt + EXP_BLK - 1) // EXP_BLK * EXP_BLK
    pend = jnp.cumsum(pcnt)
    pstart = pend - pcnt
    onehot = idx[:, :, None] == jnp.arange(N_EXPERTS, dtype=jnp.int32)
    dest = jnp.sum(jnp.where(onehot, pstart, 0), axis=-1) + rank
    nblocks = (t * TOP_K) // EXP_BLK + N_EXPERTS
    nval = (pend[-1] // EXP_BLK).astype(jnp.int32)
    bsrc = jnp.minimum(jnp.arange(nblocks, dtype=jnp.int32), nval - 1)
    bexp = jnp.sum((pend[None, :] <= (bsrc * EXP_BLK)[:, None]).astype(jnp.int32), axis=1)
    bexp = jnp.minimum(bexp, N_EXPERTS - 1)

    nval = nval.reshape(1)
    tm = min(TM_DMA, seq)
    dest = (dest * ROW_TILES).T.reshape(t // tm, 1, tm * TOP_K)
    xs = _dispatch(pstart + cnt, pcnt - cnt, nval, dest, h2t, nblocks * EXP_BLK)
    ys = _experts(bexp, bsrc, nval, xs, e_gate, e_up, e_down)
    return _combine(dest, ys, gate, x1, h2b, mods, s_gate.astype(BF16), s_up.astype(BF16),
                    s_down.astype(BF16), final_g.reshape(1, D_MODEL), seq)


def kernel(x, c, ctx, c_ctx, ada_w, ada_b, norm1_g, norm2_g, w_in, gmlp_ln_g, gmlp_ln_b, gmlp_ws, gmlp_bs,
           attn_sink, w_branch_a, w_branch_b, w_out, router_w, router_b, exp_w_gate, exp_w_up, exp_w_down,
           sh_w_gate, sh_w_up, sh_w_down, final_g):
    batch, seq, _ = x.shape
    depth = ada_w.shape[0]
    assert depth == 1, "the context stream is only carried as keys/values of a single layer"
    assert batch + 1 <= 8 and seq % WBLK == 0
    cond = jnp.concatenate([c, c_ctx[None], jnp.zeros((8 - batch - 1, D_MODEL), F32)], axis=0)
    mods = _ada(cond, ada_w[0], ada_b[0])[:batch + 1].reshape(batch + 1, 6, D_MODEL)
    out = _layer(x.reshape(batch * seq, D_MODEL), ctx.reshape(-1, D_MODEL), mods, batch, seq,
                 norm1_g[0], norm2_g[0], w_in[0], gmlp_ln_g[0], gmlp_ln_b[0], gmlp_ws[0], gmlp_bs[0],
                 attn_sink[0], w_branch_a[0], w_branch_b[0], w_out[0], router_w[0], router_b[0],
                 exp_w_gate[0], exp_w_up[0], exp_w_down[0], sh_w_gate[0], sh_w_up[0], sh_w_down[0], final_g)
    return out.reshape(batch, seq, D_MODEL)
```

```python
import functools

import jax
import jax.numpy as jnp
from jax import lax
from jax.experimental import pallas as pl
from jax.experimental.pallas import tpu as pltpu

F32 = jnp.float32
BF16 = jnp.bfloat16

D_MODEL = 1024
EPS = 1e-6
GRID_W = 64
D_A = D_MODEL // 2
G_A = 4
CHUNK = 128
N_HEADS = 8
N_KV = 2
REP = N_HEADS // N_KV
HEAD_DIM = 64
D_Q = N_HEADS * HEAD_DIM
D_KV = N_KV * HEAD_DIM
WBLK = 128
ROPE_THETA = 10000.0
N_EXPERTS = 256
TOP_K = 8
D_EXPERT = D_MODEL // 4
D_SHARED = D_MODEL // 4
ROUTE_SCALE = 2.5

C_U = 0
C_V = D_A
C_Q = 2 * D_A
C_K = C_Q + D_Q
C_VAL = C_K + D_KV
C_GA = C_VAL + D_KV
C_GB = C_GA + D_MODEL
D_IN = C_GB + D_MODEL

LANES = 128
ROPE_HALF = HEAD_DIM // 4
NEG_BIG = -1e30

TM_PROJ = 512
TM_MERGE = 512
TM_DMA = 256
EXP_BLK = 256
W_SLOTS = 3
ROW_DT = BF16
ROW_TILES = D_MODEL // LANES
ROW_WORDS = ROW_TILES // 2
WORD_DT = jnp.uint32
VMEM_LIMIT = 56 * 1024 * 1024


def _gelu(x):
    return 0.5 * x * (1.0 + jnp.tanh(0.7978845608028654 * (x + 0.044715 * x * x * x)))


def _silu(x):
    return x * jax.nn.sigmoid(x)


def _dot(a, b):
    return jnp.dot(a, b, preferred_element_type=F32)


def _rms_mod(x, g, shift, scale):
    ms = jnp.mean(x * x, axis=-1, keepdims=True)
    return (x * lax.rsqrt(ms + EPS)) * g * (1.0 + scale) + shift


def _params(*sem):
    return pltpu.CompilerParams(dimension_semantics=sem, vmem_limit_bytes=VMEM_LIMIT)


def _ada_kernel(c_ref, w_ref, b_ref, o_ref):
    c = c_ref[...]
    s = _silu(c).astype(BF16)
    o_ref[...] = _dot(s, w_ref[...].astype(BF16)) + b_ref[...]


def _ada(cond8, ada_w, ada_b):
    n = ada_w.shape[1]
    tn = 1536
    return pl.pallas_call(
        _ada_kernel,
        grid=(n // tn,),
        in_specs=[pl.BlockSpec((8, D_MODEL), lambda j: (0, 0)),
                  pl.BlockSpec((D_MODEL, tn), lambda j: (0, j)),
                  pl.BlockSpec((1, tn), lambda j: (0, j))],
        out_specs=pl.BlockSpec((8, tn), lambda j: (0, j)),
        out_shape=jax.ShapeDtypeStruct((8, n), F32),
        compiler_params=_params("arbitrary"),
        name="ada",
    )(cond8, ada_w, ada_b.reshape(1, n))


def _ctx_kernel(x_ref, mod_ref, g_ref, w_ref, k_ref, v_ref):
    mod = mod_ref[0]
    h = _rms_mod(x_ref[...], g_ref[...], mod[0:1], mod[1:2]).astype(BF16)
    z = _dot(h, w_ref[...])
    k_ref[...] = z[:, :D_KV].astype(BF16)
    v_ref[...] = z[:, D_KV:].astype(BF16)


def _ctx_kv(ctx2d, mods, g1, w_kv):
    n = ctx2d.shape[0]
    nb = mods.shape[0] - 1
    return pl.pallas_call(
        _ctx_kernel,
        grid=(1,),
        in_specs=[pl.BlockSpec((n, D_MODEL), lambda i: (0, 0)),
                  pl.BlockSpec((1, 6, D_MODEL), lambda i: (nb, 0, 0)),
                  pl.BlockSpec((1, D_MODEL), lambda i: (0, 0)),
                  pl.BlockSpec((D_MODEL, 2 * D_KV), lambda i: (0, 0))],
        out_specs=[pl.BlockSpec((n, D_KV), lambda i: (0, 0)),
                   pl.BlockSpec((n, D_KV), lambda i: (0, 0))],
        out_shape=[jax.ShapeDtypeStruct((n, D_KV), BF16)] * 2,
        compiler_params=_params("arbitrary"),
        name="ctx_kv",
    )(ctx2d, mods, g1, w_kv)


def _rope(t, cos, sin):
    lane = lax.broadcasted_iota(jnp.int32, (t.shape[0], LANES), 1)
    first = (lane & (2 * ROPE_HALF - 1)) < ROPE_HALF
    outs = []
    for j in range(t.shape[1] // LANES):
        tj = t[:, j * LANES:(j + 1) * LANES]
        up = pltpu.roll(tj, LANES - ROPE_HALF, 1)
        dn = pltpu.roll(tj, ROPE_HALF, 1)
        outs.append(tj * cos + jnp.where(first, up, dn) * sin)
    return outs


def _inproj_kernel(x_ref, mod_ref, g_ref, w_ref, lng_ref, lnb_ref, cos_ref, sin_ref,
                   gu_ref, vn_ref, q_ref, k_ref, v_ref, sga_ref, sgb_ref):
    mod = mod_ref[0]
    h = _rms_mod(x_ref[...], g_ref[...], mod[0:1], mod[1:2]).astype(BF16)

    def proj(lo, hi):
        return _dot(h, w_ref[:, lo:hi])

    gu_ref[...] = _gelu(proj(C_U, C_V)).astype(BF16)

    v = _gelu(proj(C_V, C_Q))
    mu = jnp.mean(v, axis=-1, keepdims=True)
    vc = v - mu
    var = jnp.mean(vc * vc, axis=-1, keepdims=True)
    vn_ref[...] = (vc * lax.rsqrt(var + EPS) * lng_ref[...] + lnb_ref[...]).astype(BF16)

    cos = cos_ref[...]
    sin = sin_ref[...]
    q = _rope(proj(C_Q, C_K) * (HEAD_DIM ** -0.5), cos, sin)
    for j, qj in enumerate(q):
        q_ref[:, j * LANES:(j + 1) * LANES] = qj.astype(BF16)
    k = _rope(proj(C_K, C_VAL), cos, sin)
    k_ref[...] = k[0].astype(BF16)
    v_ref[...] = proj(C_VAL, C_GA).astype(BF16)
    sga_ref[...] = jax.nn.sigmoid(proj(C_GA, C_GB)).astype(BF16)
    sgb_ref[...] = jax.nn.sigmoid(proj(C_GB, D_IN)).astype(BF16)


def _inproj(x2d, mods, g1, w_in, lng, lnb, cos, sin, seq):
    t = x2d.shape[0]
    tm = min(TM_PROJ, seq)
    spb = seq // tm
    row = lambda w: pl.BlockSpec((tm, w), lambda i: (i, 0))
    const = lambda s: pl.BlockSpec(s, lambda i: (0,) * len(s))
    return pl.pallas_call(
        _inproj_kernel,
        grid=(t // tm,),
        in_specs=[row(D_MODEL),
                  pl.BlockSpec((1, 6, D_MODEL), lambda i: (i // spb, 0, 0)),
                  const((1, D_MODEL)),
                  const((D_MODEL, D_IN)),
                  const((1, D_A)), const((1, D_A)),
                  pl.BlockSpec((tm, LANES), lambda i: (i % spb, 0)),
                  pl.BlockSpec((tm, LANES), lambda i: (i % spb, 0))],
        out_specs=[row(D_A), row(D_A), row(D_Q), row(D_KV), row(D_KV), row(D_MODEL), row(D_MODEL)],
        out_shape=[jax.ShapeDtypeStruct((t, w), BF16)
                   for w in (D_A, D_A, D_Q, D_KV, D_KV, D_MODEL, D_MODEL)],
        compiler_params=_params("arbitrary"),
        name="inproj",
    )(x2d, mods, g1, w_in, lng, lnb, cos, sin)


def _mix_kernel(sink_ref, *refs):
    seq_refs, (ws_ref, bs_ref, mask_ref), out_refs = refs[:11], refs[11:14], refs[14:]
    for b in range(seq_refs[0].shape[0]):
        _mix_block(sink_ref, *[r.at[b] for r in seq_refs], ws_ref, bs_ref, mask_ref, *[r.at[b] for r in out_refs])


def _mix_block(sink_ref, gu_ref, vn_ref, q_ref, kp_ref, kc_ref, kn_ref, vp_ref, vc_ref, vx_ref,
               kctx_ref, vctx_ref, ws_ref, bs_ref, mask_ref, a_ref, o_ref):
    for g in range(G_A):
        sl = slice(g * CHUNK, (g + 1) * CHUNK)
        s = _dot(ws_ref[g], vn_ref[:, sl]) + bs_ref[:, sl]
        a_ref[:, sl] = (gu_ref[:, sl].astype(F32) * s).astype(BF16)

    kcat = jnp.concatenate([kp_ref[...], kc_ref[...], kn_ref[...], kctx_ref[...]], axis=0)
    vcat = jnp.concatenate([vp_ref[...], vc_ref[...], vx_ref[...], vctx_ref[...]], axis=0)
    rows = REP * WBLK
    mask = mask_ref[0]
    rgrp = lax.broadcasted_iota(jnp.int32, (rows, 1), 0) // WBLK

    for kvh in range(N_KV):
        ksl = kcat[:, kvh * HEAD_DIM:(kvh + 1) * HEAD_DIM]
        vsl = vcat[:, kvh * HEAD_DIM:(kvh + 1) * HEAD_DIM]
        qs = jnp.concatenate(
            [q_ref[:, (kvh * REP + r) * HEAD_DIM:(kvh * REP + r + 1) * HEAD_DIM] for r in range(REP)],
            axis=0)
        sink = jnp.zeros((rows, 1), F32)
        for r in range(REP):
            sink = jnp.where(rgrp == r, sink_ref[kvh * REP + r], sink)
        s = lax.dot_general(qs, ksl, (((1,), (1,)), ((), ())), preferred_element_type=F32)
        s = s + mask
        m = jnp.maximum(jnp.max(s, axis=-1, keepdims=True), sink)
        p = jnp.exp(s - m)
        den = jnp.sum(p, axis=-1, keepdims=True) + jnp.exp(sink - m)
        o = _dot(p.astype(BF16), vsl) / den
        for r in range(0, REP, 2):
            pair = jnp.concatenate([o[r * WBLK:(r + 1) * WBLK], o[(r + 1) * WBLK:(r + 2) * WBLK]], axis=1)
            c0 = (kvh * REP + r) * HEAD_DIM
            o_ref[:, c0:c0 + 2 * HEAD_DIM] = pair.astype(BF16)


def _attn_masks(nctx):
    nwin = 3 * WBLK
    qi = (jnp.arange(REP * WBLK) & (WBLK - 1))[:, None]
    kj = jnp.arange(nwin + nctx)[None, :]
    band = (kj >= qi) & (kj <= qi + 2 * WBLK)
    masks = []
    for first in (False, True):
        for last in (False, True):
            lo = WBLK if first else 0
            hi = 2 * WBLK if last else nwin
            valid = (kj >= nwin) | (band & (kj >= lo) & (kj < hi))
            masks.append(jnp.where(valid, 0.0, NEG_BIG))
    return jnp.stack(masks).astype(F32)


def _mix(sink, gu, vn, q, k, v, kctx, vctx, ws, bs_full, batch, seq):
    t = gu.shape[0]
    nblk = seq // WBLK
    nctx = kctx.shape[0] // batch
    masks = _attn_masks(nctx)
    mspec = pl.BlockSpec((1,) + masks.shape[1:],
                         lambda n: (jnp.where(n == 0, 2, 0) + jnp.where(n == nblk - 1, 1, 0), 0, 0))

    cur = lambda w: pl.BlockSpec((batch, WBLK, w), lambda n: (0, n, 0))
    prev = pl.BlockSpec((batch, WBLK, D_KV), lambda n: (0, jnp.maximum(n - 1, 0), 0))
    nxt = pl.BlockSpec((batch, WBLK, D_KV), lambda n: (0, jnp.minimum(n + 1, nblk - 1), 0))
    cblk = pl.BlockSpec((batch, nctx, D_KV), lambda n: (0, 0, 0))
    per_seq = lambda x: x.reshape(batch, -1, x.shape[-1])
    gu, vn, q, k, v, kctx, vctx = map(per_seq, (gu, vn, q, k, v, kctx, vctx))
    a, o = pl.pallas_call(
        _mix_kernel,
        grid=(nblk,),
        in_specs=[pl.BlockSpec(memory_space=pltpu.SMEM),
                  cur(D_A), cur(D_A), cur(D_Q),
                  prev, cur(D_KV), nxt, prev, cur(D_KV), nxt,
                  cblk, cblk,
                  pl.BlockSpec((G_A, CHUNK, CHUNK), lambda n: (0, 0, 0)),
                  pl.BlockSpec((CHUNK, D_A), lambda n: (0, 0)),
                  mspec],
        out_specs=[cur(D_A), cur(D_Q)],
        out_shape=[jax.ShapeDtypeStruct((batch, seq, D_A), BF16), jax.ShapeDtypeStruct((batch, seq, D_Q), BF16)],
        compiler_params=_params("arbitrary"),
        name="mix",
    )(sink, gu, vn, q, k, k, k, v, v, v, kctx, vctx, ws, bs_full, masks)
    return a.reshape(t, D_A), o.reshape(t, D_Q)


def _store_row_tiles(ref, val, wref):
    rows = val.shape[0]
    for s in range(ROW_WORDS):
        lo = val[:, (2 * s) * LANES:(2 * s + 1) * LANES]
        hi = val[:, (2 * s + 1) * LANES:(2 * s + 2) * LANES]
        wref[pl.ds(s, rows, stride=ROW_WORDS), :] = pltpu.pack_elementwise([lo, hi], packed_dtype=BF16)
    ref[...] = pltpu.bitcast(wref[...], ROW_DT)


def _load_row_tiles(wref, s):
    words = wref[pl.ds(s, wref.shape[0] // ROW_WORDS, stride=ROW_WORDS), :]
    return tuple(pltpu.unpack_elementwise(words, index=i, packed_dtype=BF16, unpacked_dtype=F32) for i in (0, 1))


def _merge_kernel(a_ref, o_ref, sga_ref, sgb_ref, x_ref, mod_ref, g2_ref, wa_ref, wb_ref, wo_ref,
                  rwh_ref, rwl_ref, rb_ref,
                  x1_ref, h2t_ref, h2b_ref, idx_ref, gate_ref, rank_ref, cnt_ref, base_ref, words_ref):
    i = pl.program_id(0)

    @pl.when(i == 0)
    def _():
        base_ref[...] = jnp.zeros_like(base_ref)

    mod = mod_ref[0]
    ya = _dot(a_ref[...], wa_ref[...])
    yb = _dot(o_ref[...], wb_ref[...])
    y = sga_ref[...].astype(F32) * ya + sgb_ref[...].astype(F32) * yb
    x1 = x_ref[...] + mod[2:3] * _dot(y.astype(BF16), wo_ref[...])
    x1_ref[...] = x1
    h2 = _rms_mod(x1, g2_ref[...], mod[3:4], mod[4:5])
    h2b_ref[...] = h2.astype(BF16)
    _store_row_tiles(h2t_ref, h2, words_ref)

    hh = h2.astype(BF16)
    hl = (h2 - hh.astype(F32)).astype(BF16)
    dn = (((1,), (1,)), ((), ()))
    logits = (lax.dot_general(rwh_ref[...], hh, dn, preferred_element_type=F32)
              + lax.dot_general(rwl_ref[...], hh, dn, preferred_element_type=F32)
              + lax.dot_general(rwh_ref[...], hl, dn, preferred_element_type=F32))
    scores = jax.nn.sigmoid(logits)
    tm = scores.shape[1]
    eio = lax.broadcasted_iota(jnp.int32, scores.shape, 0).astype(F32)
    work = scores + rb_ref[...]
    picked = jnp.zeros(scores.shape, F32)
    idxs, vals = [], []
    for _ in range(TOP_K):
        m = jnp.max(work, axis=0, keepdims=True)
        ik = jnp.min(jnp.where(work == m, eio, float(N_EXPERTS)), axis=0, keepdims=True)
        oh = eio == ik
        vals.append(jnp.sum(jnp.where(oh, scores, 0.0), axis=0, keepdims=True))
        idxs.append(ik)
        work = jnp.where(oh, -jnp.inf, work)
        picked = picked + oh.astype(F32)
    total = vals[0]
    for vk in vals[1:]:
        total = total + vk

    tr = lax.broadcasted_iota(jnp.int32, (tm, tm), 0)
    tc = lax.broadcasted_iota(jnp.int32, (tm, tm), 1)
    before = (tr < tc).astype(BF16)
    base = base_ref[...]
    rank_e = _dot(picked.astype(BF16), before) + base
    ranks = [jnp.sum(jnp.where(eio == ik, rank_e, 0.0), axis=0, keepdims=True) for ik in idxs]
    base = base + jnp.sum(picked, axis=1, keepdims=True)
    base_ref[...] = base
    cnt_ref[...] = base

    idx_ref[...] = jnp.concatenate(idxs, axis=0).astype(jnp.int32)
    rank_ref[...] = jnp.concatenate(ranks, axis=0).astype(jnp.int32)
    gate_ref[...] = jnp.concatenate(vals, axis=0) * (ROUTE_SCALE / total)


def _merge(a, o, sga, sgb, x2d, mods, g2, wa, wb, wo, rwh, rwl, rb, seq):
    t = x2d.shape[0]
    tm = min(TM_MERGE, seq)
    spb = seq // tm
    row = lambda w: pl.BlockSpec((tm, w), lambda i: (i, 0))
    col = lambda: pl.BlockSpec((TOP_K, tm), lambda i: (0, i))
    const = lambda s: pl.BlockSpec(s, lambda i: (0,) * len(s))
    return pl.pallas_call(
        _merge_kernel,
        grid=(t // tm,),
        in_specs=[row(D_A), row(D_Q), row(D_MODEL), row(D_MODEL), row(D_MODEL),
                  pl.BlockSpec((1, 6, D_MODEL), lambda i: (i // spb, 0, 0)),
                  const((1, D_MODEL)),
                  const((D_A, D_MODEL)), const((D_Q, D_MODEL)), const((D_MODEL, D_MODEL)),
                  const((N_EXPERTS, D_MODEL)), const((N_EXPERTS, D_MODEL)), const((N_EXPERTS, 1))],
        out_specs=[row(D_MODEL), pl.BlockSpec((tm * ROW_TILES, LANES), lambda i: (i, 0)), row(D_MODEL),
                   col(), col(), col(), const((N_EXPERTS, 1))],
        out_shape=[jax.ShapeDtypeStruct((t, D_MODEL), F32),
                   jax.ShapeDtypeStruct((t * ROW_TILES, LANES), ROW_DT),
                   jax.ShapeDtypeStruct((t, D_MODEL), BF16),
                   jax.ShapeDtypeStruct((TOP_K, t), jnp.int32),
                   jax.ShapeDtypeStruct((TOP_K, t), F32),
                   jax.ShapeDtypeStruct((TOP_K, t), jnp.int32),
                   jax.ShapeDtypeStruct((N_EXPERTS, 1), F32)],
        scratch_shapes=[pltpu.VMEM((N_EXPERTS, 1), F32), pltpu.VMEM((tm * ROW_WORDS, LANES), WORD_DT)],
        compiler_params=_params("arbitrary"),
        name="merge",
    )(a, o, sga, sgb, x2d, mods, g2, wa, wb, wo, rwh, rwl, rb)


def _row_copy(src, s_off, dst, d_off, sem):
    return pltpu.make_async_copy(src.at[pl.ds(pl.multiple_of(s_off, ROW_TILES), ROW_TILES), :],
                                 dst.at[pl.ds(pl.multiple_of(d_off, ROW_TILES), ROW_TILES), :], sem)


def _rows_wait(ref, nrows, sem):
    n = nrows * ROW_TILES
    pltpu.make_async_copy(ref.at[pl.ds(0, n), :], ref.at[pl.ds(0, n), :], sem).wait()


def _dispatch_kernel(nsteps, pad0_ref, padn_ref, nval_ref, dest_ref, h_ref, xs_ref, zero_ref, sem, zsem):
    i = pl.program_id(0)
    tm = h_ref.shape[0] // ROW_TILES
    nblocks = xs_ref.shape[0] // (EXP_BLK * ROW_TILES)
    experts_per_step = -(-N_EXPERTS // nsteps)
    tail_per_step = -(-nblocks // nsteps)

    @pl.when(i == 0)
    def _():
        zero_ref[...] = jnp.zeros_like(zero_ref)

    def body(j, carry):
        for k in range(TOP_K):
            _row_copy(h_ref, j * ROW_TILES, xs_ref, dest_ref[0, 0, j * TOP_K + k], sem).start(priority=k % 2)
        return carry

    lax.fori_loop(0, tm, body, 0)

    def zero_fill(act):
        def pad_body(r, carry):
            e = i * experts_per_step + r

            @pl.when(e < N_EXPERTS)
            def _():
                first = pad0_ref[e]
                n = padn_ref[e]
                bit = EXP_BLK // 2
                while bit:
                    off = first + (n & ~(2 * bit - 1))

                    @pl.when((n & bit) != 0)
                    def _(bit=bit, off=off):
                        act(pltpu.make_async_copy(
                            zero_ref.at[pl.ds(0, bit * ROW_TILES), :],
                            xs_ref.at[pl.ds(pl.multiple_of(off * ROW_TILES, ROW_TILES), bit * ROW_TILES), :], zsem))
                    bit //= 2
            return carry

        lax.fori_loop(0, experts_per_step, pad_body, 0)

        def tail_body(r, carry):
            blk = nval_ref[0] + i * tail_per_step + r

            @pl.when(blk < nblocks)
            def _():
                rows = EXP_BLK * ROW_TILES
                act(pltpu.make_async_copy(zero_ref, xs_ref.at[pl.ds(pl.multiple_of(blk * rows, rows), rows), :],
                                          zsem))
            return carry

        lax.fori_loop(0, tail_per_step, tail_body, 0)

    zero_fill(lambda cp: cp.start())
    _rows_wait(xs_ref, tm * TOP_K, sem)
    zero_fill(lambda cp: cp.wait())


def _dispatch(pad0, padn, nval, dest, h2t, nrows):
    t = h2t.shape[0] // ROW_TILES
    tm = dest.shape[2] // TOP_K
    nsteps = t // tm
    grid_spec = pltpu.PrefetchScalarGridSpec(
        num_scalar_prefetch=3,
        grid=(nsteps,),
        in_specs=[pl.BlockSpec((1, 1, tm * TOP_K), lambda i, *_: (i, 0, 0), memory_space=pltpu.SMEM),
                  pl.BlockSpec((tm * ROW_TILES, LANES), lambda i, *_: (i, 0))],
        out_specs=pl.BlockSpec(memory_space=pl.ANY),
        scratch_shapes=[pltpu.VMEM((EXP_BLK * ROW_TILES, LANES), ROW_DT),
                        pltpu.SemaphoreType.DMA(()), pltpu.SemaphoreType.DMA(())],
    )
    return pl.pallas_call(
        functools.partial(_dispatch_kernel, nsteps),
        grid_spec=grid_spec,
        out_shape=jax.ShapeDtypeStruct((nrows * ROW_TILES, LANES), ROW_DT),
        compiler_params=_params("arbitrary"),
        name="dispatch",
    )(pad0, padn, nval, dest, h2t)


def _expert_kernel(bsrc_ref, nval_ref, first_ref, run_ref, rexp_ref, nruns_ref,
                   xs_ref, wg_hbm, wu_hbm, wd_hbm, ys_ref,
                   xb_ref, xw_ref, wg_f, wu_f, wd_f, wg_b, wu_b, wd_b, wsem):
    b = pl.program_id(0)
    nruns = nruns_ref[0]

    def weights(j, act):
        e = rexp_ref[j]
        s = j % W_SLOTS
        for hbm, buf in ((wg_hbm, wg_f), (wu_hbm, wu_f), (wd_hbm, wd_f)):
            act(pltpu.make_async_copy(hbm.at[e], buf.at[s], wsem.at[s]))

    @pl.when(b == 0)
    def _():
        weights(0, lambda cp: cp.start())

        @pl.when(nruns > 1)
        def _():
            weights(1, lambda cp: cp.start())

    @pl.when(b < nval_ref[0])
    def _():
        j = run_ref[b]

        @pl.when(first_ref[b] == 1)
        def _():
            weights(j, lambda cp: cp.wait())

            @pl.when(j + 2 < nruns)
            def _():
                weights(j + 2, lambda cp: cp.start())

            s = j % W_SLOTS
            wg_b[...] = wg_f[s].astype(BF16)
            wu_b[...] = wu_f[s].astype(BF16)
            wd_b[...] = wd_f[s].astype(BF16)

        xw_ref[...] = pltpu.bitcast(xs_ref[...], WORD_DT)
        for s in range(ROW_WORDS):
            for h, part in enumerate(_load_row_tiles(xw_ref, s)):
                c = 2 * s + h
                xb_ref[:, c * LANES:(c + 1) * LANES] = part.astype(BF16)
        x = xb_ref[...]
        g = _dot(x, wg_b[...])
        u = _dot(x, wu_b[...])
        act = (_silu(g) * u).astype(BF16)
        _store_row_tiles(ys_ref, _dot(act, wd_b[...]), xw_ref)

    @pl.when(b >= nval_ref[0])
    def _():
        ys_ref[...] = jnp.zeros_like(ys_ref)


def _experts(bexp, bsrc, nval, xs, wg, wu, wd):
    nrows = xs.shape[0] // ROW_TILES
    nb = nrows // EXP_BLK
    blk = (EXP_BLK * ROW_TILES, LANES)
    blocks = jnp.arange(nb, dtype=jnp.int32)
    first = ((blocks == 0) | (bexp != jnp.roll(bexp, 1))).astype(jnp.int32)
    run = jnp.cumsum(first) - 1
    runs = jnp.arange(N_EXPERTS, dtype=jnp.int32)
    rexp = jnp.sum(jnp.where((first[None, :] == 1) & (run[None, :] == runs[:, None]), bexp[None, :], 0), axis=1)
    nruns = (run[nb - 1] + 1).reshape(1)
    grid_spec = pltpu.PrefetchScalarGridSpec(
        num_scalar_prefetch=6,
        grid=(nb,),
        in_specs=[pl.BlockSpec(blk, lambda b, bs, *_: (bs[b], 0)),
                  pl.BlockSpec(memory_space=pl.ANY), pl.BlockSpec(memory_space=pl.ANY),
                  pl.BlockSpec(memory_space=pl.ANY)],
        out_specs=pl.BlockSpec(blk, lambda b, *_: (b, 0)),
        scratch_shapes=[pltpu.VMEM((EXP_BLK, D_MODEL), BF16),
                        pltpu.VMEM((EXP_BLK * ROW_WORDS, LANES), WORD_DT),
                        pltpu.VMEM((W_SLOTS, D_MODEL, D_EXPERT), F32),
                        pltpu.VMEM((W_SLOTS, D_MODEL, D_EXPERT), F32),
                        pltpu.VMEM((W_SLOTS, D_EXPERT, D_MODEL), F32),
                        pltpu.VMEM((D_MODEL, D_EXPERT), BF16),
                        pltpu.VMEM((D_MODEL, D_EXPERT), BF16),
                        pltpu.VMEM((D_EXPERT, D_MODEL), BF16),
                        pltpu.SemaphoreType.DMA((W_SLOTS,))],
    )
    return pl.pallas_call(
        _expert_kernel,
        grid_spec=grid_spec,
        out_shape=jax.ShapeDtypeStruct((nrows * ROW_TILES, LANES), ROW_DT),
        compiler_params=_params("arbitrary"),
        name="experts",
    )(bsrc, nval, first, run, rexp, nruns, xs, wg, wu, wd)


def _combine_kernel(dest_ref, next_ref, ys_ref, gate_ref, x1_ref, h2_ref, mod_ref, sg_ref, su_ref, sd_ref, fg_ref,
                    out_ref, buf_ref, bufw_ref, x2_ref, sem):
    i = pl.program_id(0)
    tm = x1_ref.shape[0]
    slot = i % 2

    def gather(ids_ref, s):
        def body(j, carry):
            for k in range(TOP_K):
                _row_copy(ys_ref, ids_ref[0, 0, j * TOP_K + k], buf_ref.at[s, k], j * ROW_TILES,
                          sem.at[s]).start(priority=k % 2)
            return carry
        lax.fori_loop(0, tm, body, 0)

    @pl.when(i == 0)
    def _():
        gather(dest_ref, 0)

    @pl.when(i + 1 < pl.num_programs(0))
    def _():
        gather(next_ref, 1 - slot)

    mod = mod_ref[0]
    h = h2_ref[...]
    act = (_silu(_dot(h, sg_ref[...])) * _dot(h, su_ref[...])).astype(BF16)
    moe = _dot(act, sd_ref[...])

    g = gate_ref[...]
    g0 = g.astype(BF16)
    r1 = g - g0.astype(F32)
    g1 = r1.astype(BF16)
    g2 = (r1 - g1.astype(F32)).astype(BF16)
    eye = (lax.broadcasted_iota(jnp.int32, (tm, tm), 0)
           == lax.broadcasted_iota(jnp.int32, (tm, tm), 1)).astype(BF16)
    dn = (((1,), (1,)), ((), ()))
    gcol = (lax.dot_general(eye, g0, dn, preferred_element_type=F32)
            + lax.dot_general(eye, g1, dn, preferred_element_type=F32)
            + lax.dot_general(eye, g2, dn, preferred_element_type=F32))

    gk = [jnp.broadcast_to(gcol[:, k:k + 1], (tm, LANES)) for k in range(TOP_K)]
    for k in range(TOP_K):
        _rows_wait(buf_ref.at[slot, k], tm, sem.at[slot])
    ssq = jnp.zeros((tm, 1), F32)
    for k in range(TOP_K):
        bufw_ref[k] = pltpu.bitcast(buf_ref[slot, k], WORD_DT)
    for s in range(ROW_WORDS):
        parts = [_load_row_tiles(bufw_ref.at[k], s) for k in range(TOP_K)]
        for h in range(2):
            c = 2 * s + h
            sl = slice(c * LANES, (c + 1) * LANES)
            m = moe[:, sl]
            for k in range(TOP_K):
                m = m + gk[k] * parts[k][h]
            x2 = x1_ref[:, sl] + mod[5:6, sl] * m
            x2_ref[:, sl] = x2
            ssq = ssq + jnp.sum(x2 * x2, axis=-1, keepdims=True)
    out_ref[...] = x2_ref[...] * lax.rsqrt(ssq * (1.0 / D_MODEL) + EPS) * fg_ref[...]


def _combine(dest, ys, gate, x1, h2, mods, sg, su, sd, fg, seq):
    t = x1.shape[0]
    tm = dest.shape[2] // TOP_K
    spb = seq // tm
    nt = t // tm
    row = lambda: pl.BlockSpec((tm, D_MODEL), lambda i: (i, 0))
    const = lambda s: pl.BlockSpec(s, lambda i: (0,) * len(s))
    ids = lambda f: pl.BlockSpec((1, 1, tm * TOP_K), f, memory_space=pltpu.SMEM)
    return pl.pallas_call(
        _combine_kernel,
        grid=(nt,),
        in_specs=[ids(lambda i: (i, 0, 0)), ids(lambda i: (jnp.minimum(i + 1, nt - 1), 0, 0)),
                  pl.BlockSpec(memory_space=pl.ANY),
                  pl.BlockSpec((TOP_K, tm), lambda i: (0, i)),
                  row(), row(),
                  pl.BlockSpec((1, 6, D_MODEL), lambda i: (i // spb, 0, 0)),
                  const((D_MODEL, D_SHARED)), const((D_MODEL, D_SHARED)), const((D_SHARED, D_MODEL)),
                  const((1, D_MODEL))],
        out_specs=row(),
        out_shape=jax.ShapeDtypeStruct((t, D_MODEL), F32),
        scratch_shapes=[pltpu.VMEM((2, TOP_K, tm * ROW_TILES, LANES), ROW_DT),
                        pltpu.VMEM((TOP_K, tm * ROW_WORDS, LANES), WORD_DT), pltpu.VMEM((tm, D_MODEL), F32),
                        pltpu.SemaphoreType.DMA((2,))],
        compiler_params=_params("arbitrary"),
        name="combine",
    )(dest, dest, ys, gate, x1, h2, mods, sg, su, sd, fg)


def _rope_tables(seq):
    pos = jnp.arange(seq)
    n_freq = HEAD_DIM // 4
    inv = ROPE_THETA ** (-jnp.arange(n_freq, dtype=F32) / n_freq)
    ang_r = (pos // GRID_W)[:, None].astype(F32) * inv
    ang_c = (pos % GRID_W)[:, None].astype(F32) * inv
    cr, sr, cc, sc = jnp.cos(ang_r), jnp.sin(ang_r), jnp.cos(ang_c), jnp.sin(ang_c)
    cos = jnp.concatenate([cr, cr, cc, cc], axis=1)
    sin = jnp.concatenate([-sr, sr, -sc, sc], axis=1)
    reps = LANES // HEAD_DIM
    return jnp.tile(cos, (1, reps)), jnp.tile(sin, (1, reps))


def _layer(x2d, ctx2d, mods, batch, seq, norm1_g, norm2_g, w_in, ln_g, ln_b, gmlp_ws, gmlp_bs, sink,
           w_a, w_b, w_o, router_w, router_b, e_gate, e_up, e_down, s_gate, s_up, s_down, final_g):
    t = x2d.shape[0]
    g1 = norm1_g.reshape(1, D_MODEL)
    w_in_b = w_in.astype(BF16)
    cos, sin = _rope_tables(seq)

    kctx, vctx = _ctx_kv(ctx2d, mods, g1, w_in_b[:, C_K:C_GA])
    gu, vn, q, k, v, sga, sgb = _inproj(x2d, mods, g1, w_in_b, ln_g.reshape(1, D_A), ln_b.reshape(1, D_A),
                                        cos, sin, seq)

    bs_full = jnp.repeat(gmlp_bs.T, D_A // G_A, axis=1)
    a, o = _mix(sink, gu, vn, q, k, v, kctx, vctx, gmlp_ws.astype(BF16), bs_full, batch, seq)

    rwt = router_w.T
    rwh = rwt.astype(BF16)
    rwl = (rwt - rwh.astype(F32)).astype(BF16)
    x1, h2t, h2b, idx, gate, rank, counts = _merge(
        a, o, sga, sgb, x2d, mods, norm2_g.reshape(1, D_MODEL),
        w_a.astype(BF16), w_b.astype(BF16), w_o.astype(BF16), rwh, rwl, router_b.reshape(N_EXPERTS, 1), seq)

    cnt = counts[:, 0].astype(jnp.int32)
    pcnt = (cnt + EXP_BLK - 1) // EXP_BLK * EXP_BLK
    pend = jnp.cumsum(pcnt)
    pstart = pend - pcnt
    onehot = idx[:, :, None] == jnp.arange(N_EXPERTS, dtype=jnp.int32)
    dest = jnp.sum(jnp.where(onehot, pstart, 0), axis=-1) + rank
    nblocks = (t * TOP_K) // EXP_BLK + N_EXPERTS
    nval = (pend[-1] // EXP_BLK).astype(jnp.int32)
    bsrc = jnp.minimum(jnp.arange(nblocks, dtype=jnp.int32), nval - 1)
    bexp = jnp.sum((pend[None, :] <= (bsrc * EXP_BLK)[:, None]).astype(jnp.int32), axis=1)
    bexp = jnp.minimum(bexp, N_EXPERTS - 1)

    nval = nval.reshape(1)
    tm = min(TM_DMA, seq)
    dest = (dest * ROW_TILES).T.reshape(t // tm, 1, tm * TOP_K)
    xs = _dispatch(pstart + cnt, pcnt - cnt, nval, dest, h2t, nblocks * EXP_BLK)
    ys = _experts(bexp, bsrc, nval, xs, e_gate, e_up, e_down)
    return _combine(dest, ys, gate, x1, h2b, mods, s_gate.astype(BF16), s_up.astype(BF16),
                    s_down.astype(BF16), final_g.reshape(1, D_MODEL), seq)


def kernel(x, c, ctx, c_ctx, ada_w, ada_b, norm1_g, norm2_g, w_in, gmlp_ln_g, gmlp_ln_b, gmlp_ws, gmlp_bs,
           attn_sink, w_branch_a, w_branch_b, w_out, router_w, router_b, exp_w_gate, exp_w_up, exp_w_down,
           sh_w_gate, sh_w_up, sh_w_down, final_g):
    batch, seq, _ = x.shape
    depth = ada_w.shape[0]
    assert depth == 1, "the context stream is only carried as keys/values of a single layer"
    assert batch + 1 <= 8 and seq % WBLK == 0
    cond = jnp.concatenate([c, c_ctx[None], jnp.zeros((8 - batch - 1, D_MODEL), F32)], axis=0)
    mods = _ada(cond, ada_w[0], ada_b[0])[:batch + 1].reshape(batch + 1, 6, D_MODEL)
    out = _layer(x.reshape(batch * seq, D_MODEL), ctx.reshape(-1, D_MODEL), mods, batch, seq,
                 norm1_g[0], norm2_g[0], w_in[0], gmlp_ln_g[0], gmlp_ln_b[0], gmlp_ws[0], gmlp_bs[0],
                 attn_sink[0], w_branch_a[0], w_branch_b[0], w_out[0], router_w[0], router_b[0],
                 exp_w_gate[0], exp_w_up[0], exp_w_down[0], sh_w_gate[0], sh_w_up[0], sh_w_down[0], final_g)
    return out.reshape(batch, seq, D_MODEL)
```

```python
import functools

import jax
import jax.numpy as jnp
from jax import lax
from jax.experimental import pallas as pl
from jax.experimental.pallas import tpu as pltpu

F32 = jnp.float32
BF16 = jnp.bfloat16

D_MODEL = 1024
EPS = 1e-6
GRID_W = 64
D_A = D_MODEL // 2
G_A = 4
CHUNK = 128
N_HEADS = 8
N_KV = 2
REP = N_HEADS // N_KV
HEAD_DIM = 64
D_Q = N_HEADS * HEAD_DIM
D_KV = N_KV * HEAD_DIM
WBLK = 128
ROPE_THETA = 10000.0
N_EXPERTS = 256
TOP_K = 8
D_EXPERT = D_MODEL // 4
D_SHARED = D_MODEL // 4
ROUTE_SCALE = 2.5

C_U = 0
C_V = D_A
C_Q = 2 * D_A
C_K = C_Q + D_Q
C_VAL = C_K + D_KV
C_GA = C_VAL + D_KV
C_GB = C_GA + D_MODEL
D_IN = C_GB + D_MODEL

LANES = 128
ROPE_HALF = HEAD_DIM // 4
NEG_BIG = -1e30

TM_PROJ = 512
TM_MERGE = 512
TM_DMA = 256
COMBINE_ROWS = 32
EXP_BLK = 256
W_SLOTS = 3
ROW_DT = BF16
ROW_TILES = D_MODEL // LANES
ROW_WORDS = ROW_TILES // 2
WORD_DT = jnp.uint32
VMEM_LIMIT = 56 * 1024 * 1024


def _gelu(x):
    return 0.5 * x * (1.0 + jnp.tanh(0.7978845608028654 * (x + 0.044715 * x * x * x)))


def _silu(x):
    return x * jax.nn.sigmoid(x)


def _dot(a, b):
    return jnp.dot(a, b, preferred_element_type=F32)


def _rms_mod(x, g, shift, scale):
    ms = jnp.mean(x * x, axis=-1, keepdims=True)
    return (x * lax.rsqrt(ms + EPS)) * g * (1.0 + scale) + shift


def _params(*sem):
    return pltpu.CompilerParams(dimension_semantics=sem, vmem_limit_bytes=VMEM_LIMIT)


def _ada_kernel(c_ref, w_ref, b_ref, o_ref):
    c = c_ref[...]
    s = _silu(c).astype(BF16)
    o_ref[...] = _dot(s, w_ref[...].astype(BF16)) + b_ref[...]


def _ada(cond8, ada_w, ada_b):
    n = ada_w.shape[1]
    tn = 1536
    return pl.pallas_call(
        _ada_kernel,
        grid=(n // tn,),
        in_specs=[pl.BlockSpec((8, D_MODEL), lambda j: (0, 0)),
                  pl.BlockSpec((D_MODEL, tn), lambda j: (0, j)),
                  pl.BlockSpec((1, tn), lambda j: (0, j))],
        out_specs=pl.BlockSpec((8, tn), lambda j: (0, j)),
        out_shape=jax.ShapeDtypeStruct((8, n), F32),
        compiler_params=_params("arbitrary"),
        name="ada",
    )(cond8, ada_w, ada_b.reshape(1, n))


def _ctx_kernel(x_ref, mod_ref, g_ref, w_ref, k_ref, v_ref):
    mod = mod_ref[0]
    h = _rms_mod(x_ref[...], g_ref[...], mod[0:1], mod[1:2]).astype(BF16)
    z = _dot(h, w_ref[...])
    k_ref[...] = z[:, :D_KV].astype(BF16)
    v_ref[...] = z[:, D_KV:].astype(BF16)


def _ctx_kv(ctx2d, mods, g1, w_kv):
    n = ctx2d.shape[0]
    nb = mods.shape[0] - 1
    return pl.pallas_call(
        _ctx_kernel,
        grid=(1,),
        in_specs=[pl.BlockSpec((n, D_MODEL), lambda i: (0, 0)),
                  pl.BlockSpec((1, 6, D_MODEL), lambda i: (nb, 0, 0)),
                  pl.BlockSpec((1, D_MODEL), lambda i: (0, 0)),
                  pl.BlockSpec((D_MODEL, 2 * D_KV), lambda i: (0, 0))],
        out_specs=[pl.BlockSpec((n, D_KV), lambda i: (0, 0)),
                   pl.BlockSpec((n, D_KV), lambda i: (0, 0))],
        out_shape=[jax.ShapeDtypeStruct((n, D_KV), BF16)] * 2,
        compiler_params=_params("arbitrary"),
        name="ctx_kv",
    )(ctx2d, mods, g1, w_kv)


def _rope(t, cos, sin):
    lane = lax.broadcasted_iota(jnp.int32, (t.shape[0], LANES), 1)
    first = (lane & (2 * ROPE_HALF - 1)) < ROPE_HALF
    outs = []
    for j in range(t.shape[1] // LANES):
        tj = t[:, j * LANES:(j + 1) * LANES]
        up = pltpu.roll(tj, LANES - ROPE_HALF, 1)
        dn = pltpu.roll(tj, ROPE_HALF, 1)
        outs.append(tj * cos + jnp.where(first, up, dn) * sin)
    return outs


def _inproj_kernel(x_ref, mod_ref, g_ref, w_ref, lng_ref, lnb_ref, cos_ref, sin_ref,
                   gu_ref, vn_ref, q_ref, k_ref, v_ref, sga_ref, sgb_ref):
    mod = mod_ref[0]
    h = _rms_mod(x_ref[...], g_ref[...], mod[0:1], mod[1:2]).astype(BF16)

    def proj(lo, hi):
        return _dot(h, w_ref[:, lo:hi])

    gu_ref[...] = _gelu(proj(C_U, C_V)).astype(BF16)

    v = _gelu(proj(C_V, C_Q))
    mu = jnp.mean(v, axis=-1, keepdims=True)
    vc = v - mu
    var = jnp.mean(vc * vc, axis=-1, keepdims=True)
    vn_ref[...] = (vc * lax.rsqrt(var + EPS) * lng_ref[...] + lnb_ref[...]).astype(BF16)

    cos = cos_ref[...]
    sin = sin_ref[...]
    q = _rope(proj(C_Q, C_K) * (HEAD_DIM ** -0.5), cos, sin)
    for j, qj in enumerate(q):
        q_ref[:, j * LANES:(j + 1) * LANES] = qj.astype(BF16)
    k = _rope(proj(C_K, C_VAL), cos, sin)
    k_ref[...] = k[0].astype(BF16)
    v_ref[...] = proj(C_VAL, C_GA).astype(BF16)
    sga_ref[...] = jax.nn.sigmoid(proj(C_GA, C_GB)).astype(BF16)
    sgb_ref[...] = jax.nn.sigmoid(proj(C_GB, D_IN)).astype(BF16)


def _inproj(x2d, mods, g1, w_in, lng, lnb, cos, sin, seq):
    t = x2d.shape[0]
    tm = min(TM_PROJ, seq)
    spb = seq // tm
    row = lambda w: pl.BlockSpec((tm, w), lambda i: (i, 0))
    const = lambda s: pl.BlockSpec(s, lambda i: (0,) * len(s))
    return pl.pallas_call(
        _inproj_kernel,
        grid=(t // tm,),
        in_specs=[row(D_MODEL),
                  pl.BlockSpec((1, 6, D_MODEL), lambda i: (i // spb, 0, 0)),
                  const((1, D_MODEL)),
                  const((D_MODEL, D_IN)),
                  const((1, D_A)), const((1, D_A)),
                  pl.BlockSpec((tm, LANES), lambda i: (i % spb, 0)),
                  pl.BlockSpec((tm, LANES), lambda i: (i % spb, 0))],
        out_specs=[row(D_A), row(D_A), row(D_Q), row(D_KV), row(D_KV), row(D_MODEL), row(D_MODEL)],
        out_shape=[jax.ShapeDtypeStruct((t, w), BF16)
                   for w in (D_A, D_A, D_Q, D_KV, D_KV, D_MODEL, D_MODEL)],
        compiler_params=_params("arbitrary"),
        name="inproj",
    )(x2d, mods, g1, w_in, lng, lnb, cos, sin)


def _mix_kernel(sink_ref, *refs):
    seq_refs, (ws_ref, bs_ref, mask_ref), out_refs = refs[:11], refs[11:14], refs[14:]
    for b in range(seq_refs[0].shape[0]):
        _mix_block(sink_ref, *[r.at[b] for r in seq_refs], ws_ref, bs_ref, mask_ref, *[r.at[b] for r in out_refs])


def _mix_block(sink_ref, gu_ref, vn_ref, q_ref, kp_ref, kc_ref, kn_ref, vp_ref, vc_ref, vx_ref,
               kctx_ref, vctx_ref, ws_ref, bs_ref, mask_ref, a_ref, o_ref):
    for g in range(G_A):
        sl = slice(g * CHUNK, (g + 1) * CHUNK)
        s = _dot(ws_ref[g], vn_ref[:, sl]) + bs_ref[:, sl]
        a_ref[:, sl] = (gu_ref[:, sl].astype(F32) * s).astype(BF16)

    kcat = jnp.concatenate([kp_ref[...], kc_ref[...], kn_ref[...], kctx_ref[...]], axis=0)
    vcat = jnp.concatenate([vp_ref[...], vc_ref[...], vx_ref[...], vctx_ref[...]], axis=0)
    rows = REP * WBLK
    mask = mask_ref[0]
    rgrp = lax.broadcasted_iota(jnp.int32, (rows, 1), 0) // WBLK

    for kvh in range(N_KV):
        ksl = kcat[:, kvh * HEAD_DIM:(kvh + 1) * HEAD_DIM]
        vsl = vcat[:, kvh * HEAD_DIM:(kvh + 1) * HEAD_DIM]
        qs = jnp.concatenate(
            [q_ref[:, (kvh * REP + r) * HEAD_DIM:(kvh * REP + r + 1) * HEAD_DIM] for r in range(REP)],
            axis=0)
        sink = jnp.zeros((rows, 1), F32)
        for r in range(REP):
            sink = jnp.where(rgrp == r, sink_ref[kvh * REP + r], sink)
        s = lax.dot_general(qs, ksl, (((1,), (1,)), ((), ())), preferred_element_type=F32)
        s = s + mask
        m = jnp.maximum(jnp.max(s, axis=-1, keepdims=True), sink)
        p = jnp.exp(s - m)
        den = jnp.sum(p, axis=-1, keepdims=True) + jnp.exp(sink - m)
        o = _dot(p.astype(BF16), vsl) / den
        for r in range(0, REP, 2):
            pair = jnp.concatenate([o[r * WBLK:(r + 1) * WBLK], o[(r + 1) * WBLK:(r + 2) * WBLK]], axis=1)
            c0 = (kvh * REP + r) * HEAD_DIM
            o_ref[:, c0:c0 + 2 * HEAD_DIM] = pair.astype(BF16)


def _attn_masks(nctx):
    nwin = 3 * WBLK
    qi = (jnp.arange(REP * WBLK) & (WBLK - 1))[:, None]
    kj = jnp.arange(nwin + nctx)[None, :]
    band = (kj >= qi) & (kj <= qi + 2 * WBLK)
    masks = []
    for first in (False, True):
        for last in (False, True):
            lo = WBLK if first else 0
            hi = 2 * WBLK if last else nwin
            valid = (kj >= nwin) | (band & (kj >= lo) & (kj < hi))
            masks.append(jnp.where(valid, 0.0, NEG_BIG))
    return jnp.stack(masks).astype(F32)


def _mix(sink, gu, vn, q, k, v, kctx, vctx, ws, bs_full, batch, seq):
    t = gu.shape[0]
    nblk = seq // WBLK
    nctx = kctx.shape[0] // batch
    masks = _attn_masks(nctx)
    mspec = pl.BlockSpec((1,) + masks.shape[1:],
                         lambda n: (jnp.where(n == 0, 2, 0) + jnp.where(n == nblk - 1, 1, 0), 0, 0))

    cur = lambda w: pl.BlockSpec((batch, WBLK, w), lambda n: (0, n, 0))
    prev = pl.BlockSpec((batch, WBLK, D_KV), lambda n: (0, jnp.maximum(n - 1, 0), 0))
    nxt = pl.BlockSpec((batch, WBLK, D_KV), lambda n: (0, jnp.minimum(n + 1, nblk - 1), 0))
    cblk = pl.BlockSpec((batch, nctx, D_KV), lambda n: (0, 0, 0))
    per_seq = lambda x: x.reshape(batch, -1, x.shape[-1])
    gu, vn, q, k, v, kctx, vctx = map(per_seq, (gu, vn, q, k, v, kctx, vctx))
    a, o = pl.pallas_call(
        _mix_kernel,
        grid=(nblk,),
        in_specs=[pl.BlockSpec(memory_space=pltpu.SMEM),
                  cur(D_A), cur(D_A), cur(D_Q),
                  prev, cur(D_KV), nxt, prev, cur(D_KV), nxt,
                  cblk, cblk,
                  pl.BlockSpec((G_A, CHUNK, CHUNK), lambda n: (0, 0, 0)),
                  pl.BlockSpec((CHUNK, D_A), lambda n: (0, 0)),
                  mspec],
        out_specs=[cur(D_A), cur(D_Q)],
        out_shape=[jax.ShapeDtypeStruct((batch, seq, D_A), BF16), jax.ShapeDtypeStruct((batch, seq, D_Q), BF16)],
        compiler_params=_params("arbitrary"),
        name="mix",
    )(sink, gu, vn, q, k, k, k, v, v, v, kctx, vctx, ws, bs_full, masks)
    return a.reshape(t, D_A), o.reshape(t, D_Q)


def _store_row_tiles(ref, val, wref):
    rows = val.shape[0]
    for s in range(ROW_WORDS):
        lo = val[:, (2 * s) * LANES:(2 * s + 1) * LANES]
        hi = val[:, (2 * s + 1) * LANES:(2 * s + 2) * LANES]
        wref[pl.ds(s, rows, stride=ROW_WORDS), :] = pltpu.pack_elementwise([lo, hi], packed_dtype=BF16)
    ref[...] = pltpu.bitcast(wref[...], ROW_DT)


def _load_row_tiles(wref, s):
    words = wref[pl.ds(s, wref.shape[0] // ROW_WORDS, stride=ROW_WORDS), :]
    return tuple(pltpu.unpack_elementwise(words, index=i, packed_dtype=BF16, unpacked_dtype=F32) for i in (0, 1))


def _merge_kernel(a_ref, o_ref, sga_ref, sgb_ref, x_ref, mod_ref, g2_ref, wa_ref, wb_ref, wo_ref,
                  rwh_ref, rwl_ref, rb_ref,
                  x1_ref, h2t_ref, h2b_ref, idx_ref, gate_ref, rank_ref, cnt_ref, base_ref, words_ref):
    i = pl.program_id(0)

    @pl.when(i == 0)
    def _():
        base_ref[...] = jnp.zeros_like(base_ref)

    mod = mod_ref[0]
    ya = _dot(a_ref[...], wa_ref[...])
    yb = _dot(o_ref[...], wb_ref[...])
    y = sga_ref[...].astype(F32) * ya + sgb_ref[...].astype(F32) * yb
    x1 = x_ref[...] + mod[2:3] * _dot(y.astype(BF16), wo_ref[...])
    x1_ref[...] = x1
    h2 = _rms_mod(x1, g2_ref[...], mod[3:4], mod[4:5])
    h2b_ref[...] = h2.astype(BF16)
    _store_row_tiles(h2t_ref, h2, words_ref)

    hh = h2.astype(BF16)
    hl = (h2 - hh.astype(F32)).astype(BF16)
    dn = (((1,), (1,)), ((), ()))
    logits = (lax.dot_general(rwh_ref[...], hh, dn, preferred_element_type=F32)
              + lax.dot_general(rwl_ref[...], hh, dn, preferred_element_type=F32)
              + lax.dot_general(rwh_ref[...], hl, dn, preferred_element_type=F32))
    scores = jax.nn.sigmoid(logits)
    tm = scores.shape[1]
    eio = lax.broadcasted_iota(jnp.int32, scores.shape, 0).astype(F32)
    work = scores + rb_ref[...]
    picked = jnp.zeros(scores.shape, F32)
    idxs, vals = [], []
    for _ in range(TOP_K):
        m = jnp.max(work, axis=0, keepdims=True)
        ik = jnp.min(jnp.where(work == m, eio, float(N_EXPERTS)), axis=0, keepdims=True)
        oh = eio == ik
        vals.append(jnp.sum(jnp.where(oh, scores, 0.0), axis=0, keepdims=True))
        idxs.append(ik)
        work = jnp.where(oh, -jnp.inf, work)
        picked = picked + oh.astype(F32)
    total = vals[0]
    for vk in vals[1:]:
        total = total + vk

    tr = lax.broadcasted_iota(jnp.int32, (tm, tm), 0)
    tc = lax.broadcasted_iota(jnp.int32, (tm, tm), 1)
    before = (tr < tc).astype(BF16)
    base = base_ref[...]
    rank_e = _dot(picked.astype(BF16), before) + base
    ranks = [jnp.sum(jnp.where(eio == ik, rank_e, 0.0), axis=0, keepdims=True) for ik in idxs]
    base = base + jnp.sum(picked, axis=1, keepdims=True)
    base_ref[...] = base
    cnt_ref[...] = base

    idx_ref[...] = jnp.concatenate(idxs, axis=0).astype(jnp.int32)
    rank_ref[...] = jnp.concatenate(ranks, axis=0).astype(jnp.int32)
    gate_ref[...] = jnp.concatenate(vals, axis=0) * (ROUTE_SCALE / total)


def _merge(a, o, sga, sgb, x2d, mods, g2, wa, wb, wo, rwh, rwl, rb, seq):
    t = x2d.shape[0]
    tm = min(TM_MERGE, seq)
    spb = seq // tm
    row = lambda w: pl.BlockSpec((tm, w), lambda i: (i, 0))
    col = lambda: pl.BlockSpec((TOP_K, tm), lambda i: (0, i))
    const = lambda s: pl.BlockSpec(s, lambda i: (0,) * len(s))
    return pl.pallas_call(
        _merge_kernel,
        grid=(t // tm,),
        in_specs=[row(D_A), row(D_Q), row(D_MODEL), row(D_MODEL), row(D_MODEL),
                  pl.BlockSpec((1, 6, D_MODEL), lambda i: (i // spb, 0, 0)),
                  const((1, D_MODEL)),
                  const((D_A, D_MODEL)), const((D_Q, D_MODEL)), const((D_MODEL, D_MODEL)),
                  const((N_EXPERTS, D_MODEL)), const((N_EXPERTS, D_MODEL)), const((N_EXPERTS, 1))],
        out_specs=[row(D_MODEL), pl.BlockSpec((tm * ROW_TILES, LANES), lambda i: (i, 0)), row(D_MODEL),
                   col(), col(), col(), const((N_EXPERTS, 1))],
        out_shape=[jax.ShapeDtypeStruct((t, D_MODEL), F32),
                   jax.ShapeDtypeStruct((t * ROW_TILES, LANES), ROW_DT),
                   jax.ShapeDtypeStruct((t, D_MODEL), BF16),
                   jax.ShapeDtypeStruct((TOP_K, t), jnp.int32),
                   jax.ShapeDtypeStruct((TOP_K, t), F32),
                   jax.ShapeDtypeStruct((TOP_K, t), jnp.int32),
                   jax.ShapeDtypeStruct((N_EXPERTS, 1), F32)],
        scratch_shapes=[pltpu.VMEM((N_EXPERTS, 1), F32), pltpu.VMEM((tm * ROW_WORDS, LANES), WORD_DT)],
        compiler_params=_params("arbitrary"),
        name="merge",
    )(a, o, sga, sgb, x2d, mods, g2, wa, wb, wo, rwh, rwl, rb)


def _row_copy(src, s_off, dst, d_off, sem):
    return pltpu.make_async_copy(src.at[pl.ds(pl.multiple_of(s_off, ROW_TILES), ROW_TILES), :],
                                 dst.at[pl.ds(pl.multiple_of(d_off, ROW_TILES), ROW_TILES), :], sem)


def _rows_wait(ref, nrows, sem):
    n = nrows * ROW_TILES
    pltpu.make_async_copy(ref.at[pl.ds(0, n), :], ref.at[pl.ds(0, n), :], sem).wait()


def _dispatch_kernel(nsteps, pad0_ref, padn_ref, nval_ref, dest_ref, h_ref, xs_ref, zero_ref, sem, zsem):
    i = pl.program_id(0)
    tm = h_ref.shape[0] // ROW_TILES
    nblocks = xs_ref.shape[0] // (EXP_BLK * ROW_TILES)
    experts_per_step = -(-N_EXPERTS // nsteps)
    tail_per_step = -(-nblocks // nsteps)

    @pl.when(i == 0)
    def _():
        zero_ref[...] = jnp.zeros_like(zero_ref)

    def body(j, carry):
        for k in range(TOP_K):
            _row_copy(h_ref, j * ROW_TILES, xs_ref, dest_ref[0, 0, j * TOP_K + k], sem).start(priority=k % 2)
        return carry

    lax.fori_loop(0, tm, body, 0)

    def zero_fill(act):
        def pad_body(r, carry):
            e = i * experts_per_step + r

            @pl.when(e < N_EXPERTS)
            def _():
                first = pad0_ref[e]
                n = padn_ref[e]
                bit = EXP_BLK // 2
                while bit:
                    off = first + (n & ~(2 * bit - 1))

                    @pl.when((n & bit) != 0)
                    def _(bit=bit, off=off):
                        act(pltpu.make_async_copy(
                            zero_ref.at[pl.ds(0, bit * ROW_TILES), :],
                            xs_ref.at[pl.ds(pl.multiple_of(off * ROW_TILES, ROW_TILES), bit * ROW_TILES), :], zsem))
                    bit //= 2
            return carry

        lax.fori_loop(0, experts_per_step, pad_body, 0)

        def tail_body(r, carry):
            blk = nval_ref[0] + i * tail_per_step + r

            @pl.when(blk < nblocks)
            def _():
                rows = EXP_BLK * ROW_TILES
                act(pltpu.make_async_copy(zero_ref, xs_ref.at[pl.ds(pl.multiple_of(blk * rows, rows), rows), :],
                                          zsem))
            return carry

        lax.fori_loop(0, tail_per_step, tail_body, 0)

    zero_fill(lambda cp: cp.start())
    _rows_wait(xs_ref, tm * TOP_K, sem)
    zero_fill(lambda cp: cp.wait())


def _dispatch(pad0, padn, nval, dest, h2t, nrows):
    t = h2t.shape[0] // ROW_TILES
    tm = dest.shape[2] // TOP_K
    nsteps = t // tm
    grid_spec = pltpu.PrefetchScalarGridSpec(
        num_scalar_prefetch=3,
        grid=(nsteps,),
        in_specs=[pl.BlockSpec((1, 1, tm * TOP_K), lambda i, *_: (i, 0, 0), memory_space=pltpu.SMEM),
                  pl.BlockSpec((tm * ROW_TILES, LANES), lambda i, *_: (i, 0))],
        out_specs=pl.BlockSpec(memory_space=pl.ANY),
        scratch_shapes=[pltpu.VMEM((EXP_BLK * ROW_TILES, LANES), ROW_DT),
                        pltpu.SemaphoreType.DMA(()), pltpu.SemaphoreType.DMA(())],
    )
    return pl.pallas_call(
        functools.partial(_dispatch_kernel, nsteps),
        grid_spec=grid_spec,
        out_shape=jax.ShapeDtypeStruct((nrows * ROW_TILES, LANES), ROW_DT),
        compiler_params=_params("arbitrary"),
        name="dispatch",
    )(pad0, padn, nval, dest, h2t)


def _expert_kernel(bsrc_ref, nval_ref, first_ref, run_ref, rexp_ref, nruns_ref,
                   xs_ref, wg_hbm, wu_hbm, wd_hbm, ys_ref,
                   xb_ref, xw_ref, wg_f, wu_f, wd_f, wg_b, wu_b, wd_b, wsem):
    b = pl.program_id(0)
    nruns = nruns_ref[0]

    def weights(j, act):
        e = rexp_ref[j]
        s = j % W_SLOTS
        for hbm, buf in ((wg_hbm, wg_f), (wu_hbm, wu_f), (wd_hbm, wd_f)):
            act(pltpu.make_async_copy(hbm.at[e], buf.at[s], wsem.at[s]))

    @pl.when(b == 0)
    def _():
        weights(0, lambda cp: cp.start())

        @pl.when(nruns > 1)
        def _():
            weights(1, lambda cp: cp.start())

    @pl.when(b < nval_ref[0])
    def _():
        j = run_ref[b]

        @pl.when(first_ref[b] == 1)
        def _():
            weights(j, lambda cp: cp.wait())

            @pl.when(j + 2 < nruns)
            def _():
                weights(j + 2, lambda cp: cp.start())

            s = j % W_SLOTS
            wg_b[...] = wg_f[s].astype(BF16)
            wu_b[...] = wu_f[s].astype(BF16)
            wd_b[...] = wd_f[s].astype(BF16)

        xw_ref[...] = pltpu.bitcast(xs_ref[...], WORD_DT)
        for s in range(ROW_WORDS):
            for h, part in enumerate(_load_row_tiles(xw_ref, s)):
                c = 2 * s + h
                xb_ref[:, c * LANES:(c + 1) * LANES] = part.astype(BF16)
        x = xb_ref[...]
        g = _dot(x, wg_b[...])
        u = _dot(x, wu_b[...])
        act = (_silu(g) * u).astype(BF16)
        _store_row_tiles(ys_ref, _dot(act, wd_b[...]), xw_ref)

    @pl.when(b >= nval_ref[0])
    def _():
        ys_ref[...] = jnp.zeros_like(ys_ref)


def _experts(bexp, bsrc, nval, xs, wg, wu, wd):
    nrows = xs.shape[0] // ROW_TILES
    nb = nrows // EXP_BLK
    blk = (EXP_BLK * ROW_TILES, LANES)
    blocks = jnp.arange(nb, dtype=jnp.int32)
    first = ((blocks == 0) | (bexp != jnp.roll(bexp, 1))).astype(jnp.int32)
    run = jnp.cumsum(first) - 1
    runs = jnp.arange(N_EXPERTS, dtype=jnp.int32)
    rexp = jnp.sum(jnp.where((first[None, :] == 1) & (run[None, :] == runs[:, None]), bexp[None, :], 0), axis=1)
    nruns = (run[nb - 1] + 1).reshape(1)
    grid_spec = pltpu.PrefetchScalarGridSpec(
        num_scalar_prefetch=6,
        grid=(nb,),
        in_specs=[pl.BlockSpec(blk, lambda b, bs, *_: (bs[b], 0)),
                  pl.BlockSpec(memory_space=pl.ANY), pl.BlockSpec(memory_space=pl.ANY),
                  pl.BlockSpec(memory_space=pl.ANY)],
        out_specs=pl.BlockSpec(blk, lambda b, *_: (b, 0)),
        scratch_shapes=[pltpu.VMEM((EXP_BLK, D_MODEL), BF16),
                        pltpu.VMEM((EXP_BLK * ROW_WORDS, LANES), WORD_DT),
                        pltpu.VMEM((W_SLOTS, D_MODEL, D_EXPERT), F32),
                        pltpu.VMEM((W_SLOTS, D_MODEL, D_EXPERT), F32),
                        pltpu.VMEM((W_SLOTS, D_EXPERT, D_MODEL), F32),
                        pltpu.VMEM((D_MODEL, D_EXPERT), BF16),
                        pltpu.VMEM((D_MODEL, D_EXPERT), BF16),
                        pltpu.VMEM((D_EXPERT, D_MODEL), BF16),
                        pltpu.SemaphoreType.DMA((W_SLOTS,))],
    )
    return pl.pallas_call(
        _expert_kernel,
        grid_spec=grid_spec,
        out_shape=jax.ShapeDtypeStruct((nrows * ROW_TILES, LANES), ROW_DT),
        compiler_params=_params("arbitrary"),
        name="experts",
    )(bsrc, nval, first, run, rexp, nruns, xs, wg, wu, wd)


def _combine_kernel(dest_ref, next_ref, ys_ref, gate_ref, x1_ref, h2_ref, mod_ref, sg_ref, su_ref, sd_ref, fg_ref,
                    out_ref, buf_ref, bufw_ref, x2_ref, gk_ref, sem):
    i = pl.program_id(0)
    tm = x1_ref.shape[0]
    slot = i % 2

    def gather(ids_ref, s):
        def body(j, carry):
            for k in range(TOP_K):
                _row_copy(ys_ref, ids_ref[0, 0, j * TOP_K + k], buf_ref.at[s, k], j * ROW_TILES,
                          sem.at[s]).start(priority=k % 2)
            return carry
        lax.fori_loop(0, tm, body, 0)

    @pl.when(i == 0)
    def _():
        gather(dest_ref, 0)

    mod = mod_ref[0]
    h = h2_ref[...]
    act = (_silu(_dot(h, sg_ref[...])) * _dot(h, su_ref[...])).astype(BF16)
    moe = _dot(act, sd_ref[...])

    g = gate_ref[...]
    g0 = g.astype(BF16)
    r1 = g - g0.astype(F32)
    g1 = r1.astype(BF16)
    g2 = (r1 - g1.astype(F32)).astype(BF16)
    eye = (lax.broadcasted_iota(jnp.int32, (tm, tm), 0)
           == lax.broadcasted_iota(jnp.int32, (tm, tm), 1)).astype(BF16)
    dn = (((1,), (1,)), ((), ()))
    gcol = (lax.dot_general(eye, g0, dn, preferred_element_type=F32)
            + lax.dot_general(eye, g1, dn, preferred_element_type=F32)
            + lax.dot_general(eye, g2, dn, preferred_element_type=F32))

    for k in range(TOP_K):
        gk_ref[k] = jnp.broadcast_to(gcol[:, k:k + 1], (tm, LANES))
    x2_ref[...] = moe
    for k in range(TOP_K):
        _rows_wait(buf_ref.at[slot, k], tm, sem.at[slot])
    for k in range(TOP_K):
        bufw_ref[k] = pltpu.bitcast(buf_ref[slot, k], WORD_DT)
    nxt = 1 - slot

    def chunk(rc, carry):
        r0 = pl.multiple_of(rc * COMBINE_ROWS, COMBINE_ROWS)
        rs = pl.ds(r0, COMBINE_ROWS)
        for jj in range(COMBINE_ROWS):
            for k in range(TOP_K):
                _row_copy(ys_ref, next_ref[0, 0, (r0 + jj) * TOP_K + k], buf_ref.at[nxt, k],
                          (r0 + jj) * ROW_TILES, sem.at[nxt]).start(priority=k % 2)
        ssq = jnp.zeros((COMBINE_ROWS, 1), F32)
        for s in range(ROW_WORDS):
            acc = [x2_ref[rs, (2 * s + h) * LANES:(2 * s + h + 1) * LANES] for h in range(2)]
            for k in range(TOP_K):
                words = bufw_ref[k, pl.ds(r0 * ROW_WORDS + s, COMBINE_ROWS, stride=ROW_WORDS), :]
                g = gk_ref[k, rs, :]
                for h in range(2):
                    acc[h] = acc[h] + g * pltpu.unpack_elementwise(words, index=h, packed_dtype=BF16,
                                                                   unpacked_dtype=F32)
            for h in range(2):
                sl = slice((2 * s + h) * LANES, (2 * s + h + 1) * LANES)
                x2 = x1_ref[rs, sl] + mod[5:6, sl] * acc[h]
                x2_ref[rs, sl] = x2
                ssq = ssq + jnp.sum(x2 * x2, axis=-1, keepdims=True)
        out_ref[rs, :] = x2_ref[rs, :] * lax.rsqrt(ssq * (1.0 / D_MODEL) + EPS) * fg_ref[...]
        return carry

    lax.fori_loop(0, tm // COMBINE_ROWS, chunk, 0)

    @pl.when(i == pl.num_programs(0) - 1)
    def _():
        for k in range(TOP_K):
            _rows_wait(buf_ref.at[nxt, k], tm, sem.at[nxt])


def _combine(dest, ys, gate, x1, h2, mods, sg, su, sd, fg, seq):
    t = x1.shape[0]
    tm = dest.shape[2] // TOP_K
    spb = seq // tm
    nt = t // tm
    row = lambda: pl.BlockSpec((tm, D_MODEL), lambda i: (i, 0))
    const = lambda s: pl.BlockSpec(s, lambda i: (0,) * len(s))
    ids = lambda f: pl.BlockSpec((1, 1, tm * TOP_K), f, memory_space=pltpu.SMEM)
    return pl.pallas_call(
        _combine_kernel,
        grid=(nt,),
        in_specs=[ids(lambda i: (i, 0, 0)), ids(lambda i: (jnp.minimum(i + 1, nt - 1), 0, 0)),
                  pl.BlockSpec(memory_space=pl.ANY),
                  pl.BlockSpec((TOP_K, tm), lambda i: (0, i)),
                  row(), row(),
                  pl.BlockSpec((1, 6, D_MODEL), lambda i: (i // spb, 0, 0)),
                  const((D_MODEL, D_SHARED)), const((D_MODEL, D_SHARED)), const((D_SHARED, D_MODEL)),
                  const((1, D_MODEL))],
        out_specs=row(),
        out_shape=jax.ShapeDtypeStruct((t, D_MODEL), F32),
        scratch_shapes=[pltpu.VMEM((2, TOP_K, tm * ROW_TILES, LANES), ROW_DT),
                        pltpu.VMEM((TOP_K, tm * ROW_WORDS, LANES), WORD_DT), pltpu.VMEM((tm, D_MODEL), F32),
                        pltpu.VMEM((TOP_K, tm, LANES), F32),
                        pltpu.SemaphoreType.DMA((2,))],
        compiler_params=_params("arbitrary"),
        name="combine",
    )(dest, dest, ys, gate, x1, h2, mods, sg, su, sd, fg)


def _rope_tables(seq):
    pos = jnp.arange(seq)
    n_freq = HEAD_DIM // 4
    inv = ROPE_THETA ** (-jnp.arange(n_freq, dtype=F32) / n_freq)
    ang_r = (pos // GRID_W)[:, None].astype(F32) * inv
    ang_c = (pos % GRID_W)[:, None].astype(F32) * inv
    cr, sr, cc, sc = jnp.cos(ang_r), jnp.sin(ang_r), jnp.cos(ang_c), jnp.sin(ang_c)
    cos = jnp.concatenate([cr, cr, cc, cc], axis=1)
    sin = jnp.concatenate([-sr, sr, -sc, sc], axis=1)
    reps = LANES // HEAD_DIM
    return jnp.tile(cos, (1, reps)), jnp.tile(sin, (1, reps))


def _layer(x2d, ctx2d, mods, batch, seq, norm1_g, norm2_g, w_in, ln_g, ln_b, gmlp_ws, gmlp_bs, sink,
           w_a, w_b, w_o, router_w, router_b, e_gate, e_up, e_down, s_gate, s_up, s_down, final_g):
    t = x2d.shape[0]
    g1 = norm1_g.reshape(1, D_MODEL)
    w_in_b = w_in.astype(BF16)
    cos, sin = _rope_tables(seq)

    kctx, vctx = _ctx_kv(ctx2d, mods, g1, w_in_b[:, C_K:C_GA])
    gu, vn, q, k, v, sga, sgb = _inproj(x2d, mods, g1, w_in_b, ln_g.reshape(1, D_A), ln_b.reshape(1, D_A),
                                        cos, sin, seq)

    bs_full = jnp.repeat(gmlp_bs.T, D_A // G_A, axis=1)
    a, o = _mix(sink, gu, vn, q, k, v, kctx, vctx, gmlp_ws.astype(BF16), bs_full, batch, seq)

    rwt = router_w.T
    rwh = rwt.astype(BF16)
    rwl = (rwt - rwh.astype(F32)).astype(BF16)
    x1, h2t, h2b, idx, gate, rank, counts = _merge(
        a, o, sga, sgb, x2d, mods, norm2_g.reshape(1, D_MODEL),
        w_a.astype(BF16), w_b.astype(BF16), w_o.astype(BF16), rwh, rwl, router_b.reshape(N_EXPERTS, 1), seq)

    cnt = counts[:, 0].astype(jnp.int32)
    pcnt = (cnt + EXP_BLK - 1) // EXP_BLK * EXP_BLK
    pend = jnp.cumsum(pcnt)
    pstart = pend - pcnt
    onehot = idx[:, :, None] == jnp.arange(N_EXPERTS, dtype=jnp.int32)
    dest = jnp.sum(jnp.where(onehot, pstart, 0), axis=-1) + rank
    nblocks = (t * TOP_K) // EXP_BLK + N_EXPERTS
    nval = (pend[-1] // EXP_BLK).astype(jnp.int32)
    bsrc = jnp.minimum(jnp.arange(nblocks, dtype=jnp.int32), nval - 1)
    bexp = jnp.sum((pend[None, :] <= (bsrc * EXP_BLK)[:, None]).astype(jnp.int32), axis=1)
    bexp = jnp.minimum(bexp, N_EXPERTS - 1)

    nval = nval.reshape(1)
    tm = min(TM_DMA, seq)
    dest = (dest * ROW_TILES).T.reshape(t // tm, 1, tm * TOP_K)
    xs = _dispatch(pstart + cnt, pcnt - cnt, nval, dest, h2t, nblocks * EXP_BLK)
    ys = _experts(bexp, bsrc, nval, xs, e_gate, e_up, e_down)
    return _combine(dest, ys, gate, x1, h2b, mods, s_gate.astype(BF16), s_up.astype(BF16),
                    s_down.astype(BF16), final_g.reshape(1, D_MODEL), seq)


def kernel(x, c, ctx, c_ctx, ada_w, ada_b, norm1_g, norm2_g, w_in, gmlp_ln_g, gmlp_ln_b, gmlp_ws, gmlp_bs,
           attn_sink, w_branch_a, w_branch_b, w_out, router_w, router_b, exp_w_gate, exp_w_up, exp_w_down,
           sh_w_gate, sh_w_up, sh_w_down, final_g):
    batch, seq, _ = x.shape
    depth = ada_w.shape[0]
    assert depth == 1, "the context stream is only carried as keys/values of a single layer"
    assert batch + 1 <= 8 and seq % WBLK == 0
    cond = jnp.concatenate([c, c_ctx[None], jnp.zeros((8 - batch - 1, D_MODEL), F32)], axis=0)
    mods = _ada(cond, ada_w[0], ada_b[0])[:batch + 1].reshape(batch + 1, 6, D_MODEL)
    out = _layer(x.reshape(batch * seq, D_MODEL), ctx.reshape(-1, D_MODEL), mods, batch, seq,
                 norm1_g[0], norm2_g[0], w_in[0], gmlp_ln_g[0], gmlp_ln_b[0], gmlp_ws[0], gmlp_bs[0],
                 attn_sink[0], w_branch_a[0], w_branch_b[0], w_out[0], router_w[0], router_b[0],
                 exp_w_gate[0], exp_w_up[0], exp_w_down[0], sh_w_gate[0], sh_w_up[0], sh_w_down[0], final_g)
    return out.reshape(batch, seq, D_MODEL)
```

```python
import functools

import jax
import jax.numpy as jnp
from jax import lax
from jax.experimental import pallas as pl
from jax.experimental.pallas import tpu as pltpu

F32 = jnp.float32
BF16 = jnp.bfloat16

D_MODEL = 1024
EPS = 1e-6
GRID_W = 64
D_A = D_MODEL // 2
G_A = 4
CHUNK = 128
N_HEADS = 8
N_KV = 2
REP = N_HEADS // N_KV
HEAD_DIM = 64
D_Q = N_HEADS * HEAD_DIM
D_KV = N_KV * HEAD_DIM
WBLK = 128
ROPE_THETA = 10000.0
N_EXPERTS = 256
TOP_K = 8
D_EXPERT = D_MODEL // 4
D_SHARED = D_MODEL // 4
ROUTE_SCALE = 2.5

C_U = 0
C_V = D_A
C_Q = 2 * D_A
C_K = C_Q + D_Q
C_VAL = C_K + D_KV
C_GA = C_VAL + D_KV
C_GB = C_GA + D_MODEL
D_IN = C_GB + D_MODEL

LANES = 128
ROPE_HALF = HEAD_DIM // 4
NEG_BIG = -1e30

TM_PROJ = 512
TM_MERGE = 512
TM_DMA = 256
COMBINE_ROWS = 32
EXP_BLK = 256
SLOT_TILE = 256
IMAP_GROUP = 8
W_SLOTS = 3
ROW_DT = BF16
ROW_TILES = D_MODEL // LANES
ROW_WORDS = ROW_TILES // 2
WORD_DT = jnp.uint32
VMEM_LIMIT = 56 * 1024 * 1024


def _gelu(x):
    return 0.5 * x * (1.0 + jnp.tanh(0.7978845608028654 * (x + 0.044715 * x * x * x)))


def _silu(x):
    return x * jax.nn.sigmoid(x)


def _dot(a, b):
    return jnp.dot(a, b, preferred_element_type=F32)


def _rms_mod(x, g, shift, scale):
    ms = jnp.mean(x * x, axis=-1, keepdims=True)
    return (x * lax.rsqrt(ms + EPS)) * g * (1.0 + scale) + shift


def _params(*sem):
    return pltpu.CompilerParams(dimension_semantics=sem, vmem_limit_bytes=VMEM_LIMIT)


def _ada_kernel(c_ref, w_ref, b_ref, o_ref):
    c = c_ref[...]
    s = _silu(c).astype(BF16)
    o_ref[...] = _dot(s, w_ref[...].astype(BF16)) + b_ref[...]


def _ada(cond8, ada_w, ada_b):
    n = ada_w.shape[1]
    tn = 1536
    return pl.pallas_call(
        _ada_kernel,
        grid=(n // tn,),
        in_specs=[pl.BlockSpec((8, D_MODEL), lambda j: (0, 0)),
                  pl.BlockSpec((D_MODEL, tn), lambda j: (0, j)),
                  pl.BlockSpec((1, tn), lambda j: (0, j))],
        out_specs=pl.BlockSpec((8, tn), lambda j: (0, j)),
        out_shape=jax.ShapeDtypeStruct((8, n), F32),
        compiler_params=_params("arbitrary"),
        name="ada",
    )(cond8, ada_w, ada_b.reshape(1, n))


def _ctx_kernel(x_ref, mod_ref, g_ref, w_ref, k_ref, v_ref):
    mod = mod_ref[0]
    h = _rms_mod(x_ref[...], g_ref[...], mod[0:1], mod[1:2]).astype(BF16)
    z = _dot(h, w_ref[...])
    k_ref[...] = z[:, :D_KV].astype(BF16)
    v_ref[...] = z[:, D_KV:].astype(BF16)


def _ctx_kv(ctx2d, mods, g1, w_kv):
    n = ctx2d.shape[0]
    nb = mods.shape[0] - 1
    return pl.pallas_call(
        _ctx_kernel,
        grid=(1,),
        in_specs=[pl.BlockSpec((n, D_MODEL), lambda i: (0, 0)),
                  pl.BlockSpec((1, 6, D_MODEL), lambda i: (nb, 0, 0)),
                  pl.BlockSpec((1, D_MODEL), lambda i: (0, 0)),
                  pl.BlockSpec((D_MODEL, 2 * D_KV), lambda i: (0, 0))],
        out_specs=[pl.BlockSpec((n, D_KV), lambda i: (0, 0)),
                   pl.BlockSpec((n, D_KV), lambda i: (0, 0))],
        out_shape=[jax.ShapeDtypeStruct((n, D_KV), BF16)] * 2,
        compiler_params=_params("arbitrary"),
        name="ctx_kv",
    )(ctx2d, mods, g1, w_kv)


def _rope(t, cos, sin):
    lane = lax.broadcasted_iota(jnp.int32, (t.shape[0], LANES), 1)
    first = (lane & (2 * ROPE_HALF - 1)) < ROPE_HALF
    outs = []
    for j in range(t.shape[1] // LANES):
        tj = t[:, j * LANES:(j + 1) * LANES]
        up = pltpu.roll(tj, LANES - ROPE_HALF, 1)
        dn = pltpu.roll(tj, ROPE_HALF, 1)
        outs.append(tj * cos + jnp.where(first, up, dn) * sin)
    return outs


def _inproj_kernel(x_ref, mod_ref, g_ref, w_ref, lng_ref, lnb_ref, cos_ref, sin_ref,
                   gu_ref, vn_ref, q_ref, k_ref, v_ref, sga_ref, sgb_ref):
    mod = mod_ref[0]
    h = _rms_mod(x_ref[...], g_ref[...], mod[0:1], mod[1:2]).astype(BF16)

    def proj(lo, hi):
        return _dot(h, w_ref[:, lo:hi])

    gu_ref[...] = _gelu(proj(C_U, C_V)).astype(BF16)

    v = _gelu(proj(C_V, C_Q))
    mu = jnp.mean(v, axis=-1, keepdims=True)
    vc = v - mu
    var = jnp.mean(vc * vc, axis=-1, keepdims=True)
    vn_ref[...] = (vc * lax.rsqrt(var + EPS) * lng_ref[...] + lnb_ref[...]).astype(BF16)

    cos = cos_ref[...]
    sin = sin_ref[...]
    q = _rope(proj(C_Q, C_K) * (HEAD_DIM ** -0.5), cos, sin)
    for j, qj in enumerate(q):
        q_ref[:, j * LANES:(j + 1) * LANES] = qj.astype(BF16)
    k = _rope(proj(C_K, C_VAL), cos, sin)
    k_ref[...] = k[0].astype(BF16)
    v_ref[...] = proj(C_VAL, C_GA).astype(BF16)
    sga_ref[...] = jax.nn.sigmoid(proj(C_GA, C_GB)).astype(BF16)
    sgb_ref[...] = jax.nn.sigmoid(proj(C_GB, D_IN)).astype(BF16)


def _inproj(x2d, mods, g1, w_in, lng, lnb, cos, sin, seq):
    t = x2d.shape[0]
    tm = min(TM_PROJ, seq)
    spb = seq // tm
    row = lambda w: pl.BlockSpec((tm, w), lambda i: (i, 0))
    const = lambda s: pl.BlockSpec(s, lambda i: (0,) * len(s))
    return pl.pallas_call(
        _inproj_kernel,
        grid=(t // tm,),
        in_specs=[row(D_MODEL),
                  pl.BlockSpec((1, 6, D_MODEL), lambda i: (i // spb, 0, 0)),
                  const((1, D_MODEL)),
                  const((D_MODEL, D_IN)),
                  const((1, D_A)), const((1, D_A)),
                  pl.BlockSpec((tm, LANES), lambda i: (i % spb, 0)),
                  pl.BlockSpec((tm, LANES), lambda i: (i % spb, 0))],
        out_specs=[row(D_A), row(D_A), row(D_Q), row(D_KV), row(D_KV), row(D_MODEL), row(D_MODEL)],
        out_shape=[jax.ShapeDtypeStruct((t, w), BF16)
                   for w in (D_A, D_A, D_Q, D_KV, D_KV, D_MODEL, D_MODEL)],
        compiler_params=_params("arbitrary"),
        name="inproj",
    )(x2d, mods, g1, w_in, lng, lnb, cos, sin)


def _mix_kernel(sink_ref, *refs):
    seq_refs, (ws_ref, bs_ref, mask_ref), out_refs = refs[:11], refs[11:14], refs[14:]
    for b in range(seq_refs[0].shape[0]):
        _mix_block(sink_ref, *[r.at[b] for r in seq_refs], ws_ref, bs_ref, mask_ref, *[r.at[b] for r in out_refs])


def _mix_block(sink_ref, gu_ref, vn_ref, q_ref, kp_ref, kc_ref, kn_ref, vp_ref, vc_ref, vx_ref,
               kctx_ref, vctx_ref, ws_ref, bs_ref, mask_ref, a_ref, o_ref):
    for g in range(G_A):
        sl = slice(g * CHUNK, (g + 1) * CHUNK)
        s = _dot(ws_ref[g], vn_ref[:, sl]) + bs_ref[:, sl]
        a_ref[:, sl] = (gu_ref[:, sl].astype(F32) * s).astype(BF16)

    kcat = jnp.concatenate([kp_ref[...], kc_ref[...], kn_ref[...], kctx_ref[...]], axis=0)
    vcat = jnp.concatenate([vp_ref[...], vc_ref[...], vx_ref[...], vctx_ref[...]], axis=0)
    rows = REP * WBLK
    mask = mask_ref[0]
    rgrp = lax.broadcasted_iota(jnp.int32, (rows, 1), 0) // WBLK

    for kvh in range(N_KV):
        ksl = kcat[:, kvh * HEAD_DIM:(kvh + 1) * HEAD_DIM]
        vsl = vcat[:, kvh * HEAD_DIM:(kvh + 1) * HEAD_DIM]
        qs = jnp.concatenate(
            [q_ref[:, (kvh * REP + r) * HEAD_DIM:(kvh * REP + r + 1) * HEAD_DIM] for r in range(REP)],
            axis=0)
        sink = jnp.zeros((rows, 1), F32)
        for r in range(REP):
            sink = jnp.where(rgrp == r, sink_ref[kvh * REP + r], sink)
        s = lax.dot_general(qs, ksl, (((1,), (1,)), ((), ())), preferred_element_type=F32)
        s = s + mask
        m = jnp.maximum(jnp.max(s, axis=-1, keepdims=True), sink)
        p = jnp.exp(s - m)
        den = jnp.sum(p, axis=-1, keepdims=True) + jnp.exp(sink - m)
        o = _dot(p.astype(BF16), vsl) / den
        for r in range(0, REP, 2):
            pair = jnp.concatenate([o[r * WBLK:(r + 1) * WBLK], o[(r + 1) * WBLK:(r + 2) * WBLK]], axis=1)
            c0 = (kvh * REP + r) * HEAD_DIM
            o_ref[:, c0:c0 + 2 * HEAD_DIM] = pair.astype(BF16)


def _attn_masks(nctx):
    nwin = 3 * WBLK
    qi = (jnp.arange(REP * WBLK) & (WBLK - 1))[:, None]
    kj = jnp.arange(nwin + nctx)[None, :]
    band = (kj >= qi) & (kj <= qi + 2 * WBLK)
    masks = []
    for first in (False, True):
        for last in (False, True):
            lo = WBLK if first else 0
            hi = 2 * WBLK if last else nwin
            valid = (kj >= nwin) | (band & (kj >= lo) & (kj < hi))
            masks.append(jnp.where(valid, 0.0, NEG_BIG))
    return jnp.stack(masks).astype(F32)


def _mix(sink, gu, vn, q, k, v, kctx, vctx, ws, bs_full, batch, seq):
    t = gu.shape[0]
    nblk = seq // WBLK
    nctx = kctx.shape[0] // batch
    masks = _attn_masks(nctx)
    mspec = pl.BlockSpec((1,) + masks.shape[1:],
                         lambda n: (jnp.where(n == 0, 2, 0) + jnp.where(n == nblk - 1, 1, 0), 0, 0))

    cur = lambda w: pl.BlockSpec((batch, WBLK, w), lambda n: (0, n, 0))
    prev = pl.BlockSpec((batch, WBLK, D_KV), lambda n: (0, jnp.maximum(n - 1, 0), 0))
    nxt = pl.BlockSpec((batch, WBLK, D_KV), lambda n: (0, jnp.minimum(n + 1, nblk - 1), 0))
    cblk = pl.BlockSpec((batch, nctx, D_KV), lambda n: (0, 0, 0))
    per_seq = lambda x: x.reshape(batch, -1, x.shape[-1])
    gu, vn, q, k, v, kctx, vctx = map(per_seq, (gu, vn, q, k, v, kctx, vctx))
    a, o = pl.pallas_call(
        _mix_kernel,
        grid=(nblk,),
        in_specs=[pl.BlockSpec(memory_space=pltpu.SMEM),
                  cur(D_A), cur(D_A), cur(D_Q),
                  prev, cur(D_KV), nxt, prev, cur(D_KV), nxt,
                  cblk, cblk,
                  pl.BlockSpec((G_A, CHUNK, CHUNK), lambda n: (0, 0, 0)),
                  pl.BlockSpec((CHUNK, D_A), lambda n: (0, 0)),
                  mspec],
        out_specs=[cur(D_A), cur(D_Q)],
        out_shape=[jax.ShapeDtypeStruct((batch, seq, D_A), BF16), jax.ShapeDtypeStruct((batch, seq, D_Q), BF16)],
        compiler_params=_params("arbitrary"),
        name="mix",
    )(sink, gu, vn, q, k, k, k, v, v, v, kctx, vctx, ws, bs_full, masks)
    return a.reshape(t, D_A), o.reshape(t, D_Q)


def _store_row_tiles(ref, val, wref):
    rows = val.shape[0]
    for s in range(ROW_WORDS):
        lo = val[:, (2 * s) * LANES:(2 * s + 1) * LANES]
        hi = val[:, (2 * s + 1) * LANES:(2 * s + 2) * LANES]
        wref[pl.ds(s, rows, stride=ROW_WORDS), :] = pltpu.pack_elementwise([lo, hi], packed_dtype=BF16)
    ref[...] = pltpu.bitcast(wref[...], ROW_DT)


def _load_row_tiles(wref, s):
    words = wref[pl.ds(s, wref.shape[0] // ROW_WORDS, stride=ROW_WORDS), :]
    return tuple(pltpu.unpack_elementwise(words, index=i, packed_dtype=BF16, unpacked_dtype=F32) for i in (0, 1))


def _merge_kernel(a_ref, o_ref, sga_ref, sgb_ref, x_ref, mod_ref, g2_ref, wa_ref, wb_ref, wo_ref,
                  rwh_ref, rwl_ref, rb_ref,
                  x1_ref, h2t_ref, h2b_ref, idx_ref, gate_ref, rank_ref, cnt_ref, ci_ref, cum_ref,
                  base_ref, words_ref):
    i = pl.program_id(0)

    @pl.when(i == 0)
    def _():
        base_ref[...] = jnp.zeros_like(base_ref)

    mod = mod_ref[0]
    ya = _dot(a_ref[...], wa_ref[...])
    yb = _dot(o_ref[...], wb_ref[...])
    y = sga_ref[...].astype(F32) * ya + sgb_ref[...].astype(F32) * yb
    x1 = x_ref[...] + mod[2:3] * _dot(y.astype(BF16), wo_ref[...])
    x1_ref[...] = x1
    h2 = _rms_mod(x1, g2_ref[...], mod[3:4], mod[4:5])
    h2b_ref[...] = h2.astype(BF16)
    _store_row_tiles(h2t_ref, h2, words_ref)

    hh = h2.astype(BF16)
    hl = (h2 - hh.astype(F32)).astype(BF16)
    dn = (((1,), (1,)), ((), ()))
    logits = (lax.dot_general(rwh_ref[...], hh, dn, preferred_element_type=F32)
              + lax.dot_general(rwl_ref[...], hh, dn, preferred_element_type=F32)
              + lax.dot_general(rwh_ref[...], hl, dn, preferred_element_type=F32))
    scores = jax.nn.sigmoid(logits)
    tm = scores.shape[1]
    eio = lax.broadcasted_iota(jnp.int32, scores.shape, 0).astype(F32)
    work = scores + rb_ref[...]
    picked = jnp.zeros(scores.shape, F32)
    idxs, vals = [], []
    for _ in range(TOP_K):
        m = jnp.max(work, axis=0, keepdims=True)
        ik = jnp.min(jnp.where(work == m, eio, float(N_EXPERTS)), axis=0, keepdims=True)
        oh = eio == ik
        vals.append(jnp.sum(jnp.where(oh, scores, 0.0), axis=0, keepdims=True))
        idxs.append(ik)
        work = jnp.where(oh, -jnp.inf, work)
        picked = picked + oh.astype(F32)
    total = vals[0]
    for vk in vals[1:]:
        total = total + vk

    tr = lax.broadcasted_iota(jnp.int32, (tm, tm), 0)
    tc = lax.broadcasted_iota(jnp.int32, (tm, tm), 1)
    before = (tr < tc).astype(BF16)
    base = base_ref[...]
    earlier = _dot(picked.astype(BF16), before)
    rank_e = earlier + base
    ranks = [jnp.sum(jnp.where(eio == ik, rank_e, 0.0), axis=0, keepdims=True) for ik in idxs]
    for j in range(tm // SLOT_TILE):
        sl = slice(j * SLOT_TILE, (j + 1) * SLOT_TILE)
        ci_ref[:, sl] = (earlier[:, sl] + picked[:, sl] - (base - base_ref[...])).astype(BF16)
        base = base + jnp.sum(picked[:, sl], axis=1, keepdims=True)
        cum_ref[j] = base
    base_ref[...] = base
    cnt_ref[...] = base

    idx_ref[...] = jnp.concatenate(idxs, axis=0).astype(jnp.int32)
    rank_ref[...] = jnp.concatenate(ranks, axis=0).astype(jnp.int32)
    gate_ref[...] = jnp.concatenate(vals, axis=0) * (ROUTE_SCALE / total)


def _merge(a, o, sga, sgb, x2d, mods, g2, wa, wb, wo, rwh, rwl, rb, seq):
    t = x2d.shape[0]
    tm = min(TM_MERGE, seq)
    spb = seq // tm
    row = lambda w: pl.BlockSpec((tm, w), lambda i: (i, 0))
    col = lambda: pl.BlockSpec((TOP_K, tm), lambda i: (0, i))
    const = lambda s: pl.BlockSpec(s, lambda i: (0,) * len(s))
    return pl.pallas_call(
        _merge_kernel,
        grid=(t // tm,),
        in_specs=[row(D_A), row(D_Q), row(D_MODEL), row(D_MODEL), row(D_MODEL),
                  pl.BlockSpec((1, 6, D_MODEL), lambda i: (i // spb, 0, 0)),
                  const((1, D_MODEL)),
                  const((D_A, D_MODEL)), const((D_Q, D_MODEL)), const((D_MODEL, D_MODEL)),
                  const((N_EXPERTS, D_MODEL)), const((N_EXPERTS, D_MODEL)), const((N_EXPERTS, 1))],
        out_specs=[row(D_MODEL), pl.BlockSpec((tm * ROW_TILES, LANES), lambda i: (i, 0)), row(D_MODEL),
                   col(), col(), col(), const((N_EXPERTS, 1)),
                   pl.BlockSpec((N_EXPERTS, tm), lambda i: (0, i)),
                   pl.BlockSpec((tm // SLOT_TILE, N_EXPERTS, 1), lambda i: (i, 0, 0))],
        out_shape=[jax.ShapeDtypeStruct((t, D_MODEL), F32),
                   jax.ShapeDtypeStruct((t * ROW_TILES, LANES), ROW_DT),
                   jax.ShapeDtypeStruct((t, D_MODEL), BF16),
                   jax.ShapeDtypeStruct((TOP_K, t), jnp.int32),
                   jax.ShapeDtypeStruct((TOP_K, t), F32),
                   jax.ShapeDtypeStruct((TOP_K, t), jnp.int32),
                   jax.ShapeDtypeStruct((N_EXPERTS, 1), F32),
                   jax.ShapeDtypeStruct((N_EXPERTS, t), BF16),
                   jax.ShapeDtypeStruct((t // SLOT_TILE, N_EXPERTS, 1), F32)],
        scratch_shapes=[pltpu.VMEM((N_EXPERTS, 1), F32), pltpu.VMEM((tm * ROW_WORDS, LANES), WORD_DT)],
        compiler_params=_params("arbitrary"),
        name="merge",
    )(a, o, sga, sgb, x2d, mods, g2, wa, wb, wo, rwh, rwl, rb)


def _row_copy(src, s_off, dst, d_off, sem):
    return pltpu.make_async_copy(src.at[pl.ds(pl.multiple_of(s_off, ROW_TILES), ROW_TILES), :],
                                 dst.at[pl.ds(pl.multiple_of(d_off, ROW_TILES), ROW_TILES), :], sem)


def _rows_wait(ref, nrows, sem):
    n = nrows * ROW_TILES
    pltpu.make_async_copy(ref.at[pl.ds(0, n), :], ref.at[pl.ds(0, n), :], sem).wait()


def _slots_kernel(bexp_ref, r0_ref, cnt_ref, *refs):
    ci_refs, cum_ref, tok_ref = refs[:IMAP_GROUP], refs[IMAP_GROUP], refs[IMAP_GROUP + 1]
    i = pl.program_id(0)
    nt, tm = ci_refs[0].shape[1:]
    lane = lax.broadcasted_iota(jnp.int32, (1, EXP_BLK), 1)
    tile = lax.broadcasted_iota(jnp.int32, (nt, EXP_BLK), 0).astype(F32)
    for g in range(IMAP_GROUP):
        b = i * IMAP_GROUP + g
        r = (r0_ref[b] + lane).astype(F32)
        cum = cum_ref[g]
        jstar = jnp.sum((cum <= r).astype(F32), axis=0, keepdims=True)
        onehot = (tile == jstar).astype(BF16)
        before = jnp.sum(jnp.where(tile == jstar - 1.0, cum, 0.0), axis=0, keepdims=True)
        isel = lax.dot_general(ci_refs[g][0], onehot, (((0,), (0,)), ((), ())),
                               preferred_element_type=F32)
        tloc = jnp.sum((isel <= r - before).astype(F32), axis=0, keepdims=True)
        tok = (jstar * tm + tloc).astype(jnp.int32)
        tok_ref[g] = jnp.where((r0_ref[b] + lane) < cnt_ref[b], tok, 0)


def _slots(bexp, r0, cntb, ci3, cumb):
    nblocks = bexp.shape[0]
    nt, tm = ci3.shape[1:]
    per = lambda g: pl.BlockSpec((1, nt, tm), lambda i, be, *_: (be[i * IMAP_GROUP + g], 0, 0))
    grid_spec = pltpu.PrefetchScalarGridSpec(
        num_scalar_prefetch=3,
        grid=(nblocks // IMAP_GROUP,),
        in_specs=[per(g) for g in range(IMAP_GROUP)] + [pl.BlockSpec((IMAP_GROUP, nt, 1), lambda i, *_: (i, 0, 0))],
        out_specs=pl.BlockSpec((IMAP_GROUP, 1, EXP_BLK), lambda i, *_: (i, 0, 0)),
    )
    return pl.pallas_call(
        _slots_kernel,
        grid_spec=grid_spec,
        out_shape=jax.ShapeDtypeStruct((nblocks, 1, EXP_BLK), jnp.int32),
        compiler_params=_params("arbitrary"),
        name="slots",
    )(bexp, r0, cntb, *([ci3] * IMAP_GROUP), cumb)


def _expert_kernel(nval_ref, first_ref, run_ref, rexp_ref, nruns_ref,
                   tok_ref, next_ref, h_hbm, wg_hbm, wu_hbm, wd_hbm, ys_ref,
                   xin_ref, xb_ref, xw_ref, wg_f, wu_f, wd_f, wg_b, wu_b, wd_b, wsem, gsem):
    b = pl.program_id(0)
    nruns = nruns_ref[0]
    slot = b % 2

    @pl.when(b == 0)
    def _():
        def body(r, carry):
            _row_copy(h_hbm, tok_ref[0, 0, r] * ROW_TILES, xin_ref.at[0], r * ROW_TILES, gsem.at[0]).start()
            return carry
        lax.fori_loop(0, EXP_BLK, body, 0)

    def weights(j, act):
        e = rexp_ref[j]
        s = j % W_SLOTS
        for hbm, buf in ((wg_hbm, wg_f), (wu_hbm, wu_f), (wd_hbm, wd_f)):
            act(pltpu.make_async_copy(hbm.at[e], buf.at[s], wsem.at[s]))

    @pl.when(b == 0)
    def _():
        weights(0, lambda cp: cp.start())

        @pl.when(nruns > 1)
        def _():
            weights(1, lambda cp: cp.start())

    @pl.when(b < nval_ref[0])
    def _():
        j = run_ref[b]

        @pl.when(first_ref[b] == 1)
        def _():
            weights(j, lambda cp: cp.wait())

            @pl.when(j + 2 < nruns)
            def _():
                weights(j + 2, lambda cp: cp.start())

            s = j % W_SLOTS
            wg_b[...] = wg_f[s].astype(BF16)
            wu_b[...] = wu_f[s].astype(BF16)
            wd_b[...] = wd_f[s].astype(BF16)

        _rows_wait(xin_ref.at[slot], EXP_BLK, gsem.at[slot])
        xw_ref[...] = pltpu.bitcast(xin_ref[slot], WORD_DT)
        for r in range(EXP_BLK):
            _row_copy(h_hbm, next_ref[0, 0, r] * ROW_TILES, xin_ref.at[1 - slot], r * ROW_TILES,
                      gsem.at[1 - slot]).start(priority=r % 2)
        for s in range(ROW_WORDS):
            for h, part in enumerate(_load_row_tiles(xw_ref, s)):
                c = 2 * s + h
                xb_ref[:, c * LANES:(c + 1) * LANES] = part.astype(BF16)
        x = xb_ref[...]
        g = _dot(x, wg_b[...])
        u = _dot(x, wu_b[...])
        act = (_silu(g) * u).astype(BF16)
        _store_row_tiles(ys_ref, _dot(act, wd_b[...]), xw_ref)

        @pl.when(b == nval_ref[0] - 1)
        def _():
            _rows_wait(xin_ref.at[1 - slot], EXP_BLK, gsem.at[1 - slot])

    @pl.when(b >= nval_ref[0])
    def _():
        ys_ref[...] = jnp.zeros_like(ys_ref)


def _experts(bexp, nval, tok, h2t, wg, wu, wd):
    nb = bexp.shape[0]
    nrows = nb * EXP_BLK
    blk = (EXP_BLK * ROW_TILES, LANES)
    ids = lambda f: pl.BlockSpec((1, 1, EXP_BLK), f, memory_space=pltpu.SMEM)
    blocks = jnp.arange(nb, dtype=jnp.int32)
    first = ((blocks == 0) | (bexp != jnp.roll(bexp, 1))).astype(jnp.int32)
    run = jnp.cumsum(first) - 1
    runs = jnp.arange(N_EXPERTS, dtype=jnp.int32)
    rexp = jnp.sum(jnp.where((first[None, :] == 1) & (run[None, :] == runs[:, None]), bexp[None, :], 0), axis=1)
    nruns = (run[nb - 1] + 1).reshape(1)
    grid_spec = pltpu.PrefetchScalarGridSpec(
        num_scalar_prefetch=5,
        grid=(nb,),
        in_specs=[ids(lambda b, *_: (b, 0, 0)), ids(lambda b, *_: (jnp.minimum(b + 1, nb - 1), 0, 0)),
                  pl.BlockSpec(memory_space=pl.ANY),
                  pl.BlockSpec(memory_space=pl.ANY), pl.BlockSpec(memory_space=pl.ANY),
                  pl.BlockSpec(memory_space=pl.ANY)],
        out_specs=pl.BlockSpec(blk, lambda b, *_: (b, 0)),
        scratch_shapes=[pltpu.VMEM((2, EXP_BLK * ROW_TILES, LANES), ROW_DT),
                        pltpu.VMEM((EXP_BLK, D_MODEL), BF16),
                        pltpu.VMEM((EXP_BLK * ROW_WORDS, LANES), WORD_DT),
                        pltpu.VMEM((W_SLOTS, D_MODEL, D_EXPERT), F32),
                        pltpu.VMEM((W_SLOTS, D_MODEL, D_EXPERT), F32),
                        pltpu.VMEM((W_SLOTS, D_EXPERT, D_MODEL), F32),
                        pltpu.VMEM((D_MODEL, D_EXPERT), BF16),
                        pltpu.VMEM((D_MODEL, D_EXPERT), BF16),
                        pltpu.VMEM((D_EXPERT, D_MODEL), BF16),
                        pltpu.SemaphoreType.DMA((W_SLOTS,)), pltpu.SemaphoreType.DMA((2,))],
    )
    return pl.pallas_call(
        _expert_kernel,
        grid_spec=grid_spec,
        out_shape=jax.ShapeDtypeStruct((nrows * ROW_TILES, LANES), ROW_DT),
        compiler_params=_params("arbitrary"),
        name="experts",
    )(nval, first, run, rexp, nruns, tok, tok, h2t, wg, wu, wd)


def _combine_kernel(dest_ref, next_ref, ys_ref, gate_ref, x1_ref, h2_ref, mod_ref, sg_ref, su_ref, sd_ref, fg_ref,
                    out_ref, buf_ref, bufw_ref, x2_ref, gk_ref, sem):
    i = pl.program_id(0)
    tm = x1_ref.shape[0]
    slot = i % 2

    def gather(ids_ref, s):
        def body(j, carry):
            for k in range(TOP_K):
                _row_copy(ys_ref, ids_ref[0, 0, j * TOP_K + k], buf_ref.at[s, k], j * ROW_TILES,
                          sem.at[s]).start(priority=k % 2)
            return carry
        lax.fori_loop(0, tm, body, 0)

    @pl.when(i == 0)
    def _():
        gather(dest_ref, 0)

    mod = mod_ref[0]
    h = h2_ref[...]
    act = (_silu(_dot(h, sg_ref[...])) * _dot(h, su_ref[...])).astype(BF16)
    moe = _dot(act, sd_ref[...])

    g = gate_ref[...]
    g0 = g.astype(BF16)
    r1 = g - g0.astype(F32)
    g1 = r1.astype(BF16)
    g2 = (r1 - g1.astype(F32)).astype(BF16)
    eye = (lax.broadcasted_iota(jnp.int32, (tm, tm), 0)
           == lax.broadcasted_iota(jnp.int32, (tm, tm), 1)).astype(BF16)
    dn = (((1,), (1,)), ((), ()))
    gcol = (lax.dot_general(eye, g0, dn, preferred_element_type=F32)
            + lax.dot_general(eye, g1, dn, preferred_element_type=F32)
            + lax.dot_general(eye, g2, dn, preferred_element_type=F32))

    for k in range(TOP_K):
        gk_ref[k] = jnp.broadcast_to(gcol[:, k:k + 1], (tm, LANES))
    x2_ref[...] = moe
    for k in range(TOP_K):
        _rows_wait(buf_ref.at[slot, k], tm, sem.at[slot])
    for k in range(TOP_K):
        bufw_ref[k] = pltpu.bitcast(buf_ref[slot, k], WORD_DT)
    nxt = 1 - slot

    def chunk(rc, carry):
        r0 = pl.multiple_of(rc * COMBINE_ROWS, COMBINE_ROWS)
        rs = pl.ds(r0, COMBINE_ROWS)
        for jj in range(COMBINE_ROWS):
            for k in range(TOP_K):
                _row_copy(ys_ref, next_ref[0, 0, (r0 + jj) * TOP_K + k], buf_ref.at[nxt, k],
                          (r0 + jj) * ROW_TILES, sem.at[nxt]).start(priority=k % 2)
        ssq = jnp.zeros((COMBINE_ROWS, 1), F32)
        for s in range(ROW_WORDS):
            acc = [x2_ref[rs, (2 * s + h) * LANES:(2 * s + h + 1) * LANES] for h in range(2)]
            for k in range(TOP_K):
                words = bufw_ref[k, pl.ds(r0 * ROW_WORDS + s, COMBINE_ROWS, stride=ROW_WORDS), :]
                g = gk_ref[k, rs, :]
                for h in range(2):
                    acc[h] = acc[h] + g * pltpu.unpack_elementwise(words, index=h, packed_dtype=BF16,
                                                                   unpacked_dtype=F32)
            for h in range(2):
                sl = slice((2 * s + h) * LANES, (2 * s + h + 1) * LANES)
                x2 = x1_ref[rs, sl] + mod[5:6, sl] * acc[h]
                x2_ref[rs, sl] = x2
                ssq = ssq + jnp.sum(x2 * x2, axis=-1, keepdims=True)
        out_ref[rs, :] = x2_ref[rs, :] * lax.rsqrt(ssq * (1.0 / D_MODEL) + EPS) * fg_ref[...]
        return carry

    lax.fori_loop(0, tm // COMBINE_ROWS, chunk, 0)

    @pl.when(i == pl.num_programs(0) - 1)
    def _():
        for k in range(TOP_K):
            _rows_wait(buf_ref.at[nxt, k], tm, sem.at[nxt])


def _combine(dest, ys, gate, x1, h2, mods, sg, su, sd, fg, seq):
    t = x1.shape[0]
    tm = dest.shape[2] // TOP_K
    spb = seq // tm
    nt = t // tm
    row = lambda: pl.BlockSpec((tm, D_MODEL), lambda i: (i, 0))
    const = lambda s: pl.BlockSpec(s, lambda i: (0,) * len(s))
    ids = lambda f: pl.BlockSpec((1, 1, tm * TOP_K), f, memory_space=pltpu.SMEM)
    return pl.pallas_call(
        _combine_kernel,
        grid=(nt,),
        in_specs=[ids(lambda i: (i, 0, 0)), ids(lambda i: (jnp.minimum(i + 1, nt - 1), 0, 0)),
                  pl.BlockSpec(memory_space=pl.ANY),
                  pl.BlockSpec((TOP_K, tm), lambda i: (0, i)),
                  row(), row(),
                  pl.BlockSpec((1, 6, D_MODEL), lambda i: (i // spb, 0, 0)),
                  const((D_MODEL, D_SHARED)), const((D_MODEL, D_SHARED)), const((D_SHARED, D_MODEL)),
                  const((1, D_MODEL))],
        out_specs=row(),
        out_shape=jax.ShapeDtypeStruct((t, D_MODEL), F32),
        scratch_shapes=[pltpu.VMEM((2, TOP_K, tm * ROW_TILES, LANES), ROW_DT),
                        pltpu.VMEM((TOP_K, tm * ROW_WORDS, LANES), WORD_DT), pltpu.VMEM((tm, D_MODEL), F32),
                        pltpu.VMEM((TOP_K, tm, LANES), F32),
                        pltpu.SemaphoreType.DMA((2,))],
        compiler_params=_params("arbitrary"),
        name="combine",
    )(dest, dest, ys, gate, x1, h2, mods, sg, su, sd, fg)


def _rope_tables(seq):
    pos = jnp.arange(seq)
    n_freq = HEAD_DIM // 4
    inv = ROPE_THETA ** (-jnp.arange(n_freq, dtype=F32) / n_freq)
    ang_r = (pos // GRID_W)[:, None].astype(F32) * inv
    ang_c = (pos % GRID_W)[:, None].astype(F32) * inv
    cr, sr, cc, sc = jnp.cos(ang_r), jnp.sin(ang_r), jnp.cos(ang_c), jnp.sin(ang_c)
    cos = jnp.concatenate([cr, cr, cc, cc], axis=1)
    sin = jnp.concatenate([-sr, sr, -sc, sc], axis=1)
    reps = LANES // HEAD_DIM
    return jnp.tile(cos, (1, reps)), jnp.tile(sin, (1, reps))


def _layer(x2d, ctx2d, mods, batch, seq, norm1_g, norm2_g, w_in, ln_g, ln_b, gmlp_ws, gmlp_bs, sink,
           w_a, w_b, w_o, router_w, router_b, e_gate, e_up, e_down, s_gate, s_up, s_down, final_g):
    t = x2d.shape[0]
    g1 = norm1_g.reshape(1, D_MODEL)
    w_in_b = w_in.astype(BF16)
    cos, sin = _rope_tables(seq)

    kctx, vctx = _ctx_kv(ctx2d, mods, g1, w_in_b[:, C_K:C_GA])
    gu, vn, q, k, v, sga, sgb = _inproj(x2d, mods, g1, w_in_b, ln_g.reshape(1, D_A), ln_b.reshape(1, D_A),
                                        cos, sin, seq)

    bs_full = jnp.repeat(gmlp_bs.T, D_A // G_A, axis=1)
    a, o = _mix(sink, gu, vn, q, k, v, kctx, vctx, gmlp_ws.astype(BF16), bs_full, batch, seq)

    rwt = router_w.T
    rwh = rwt.astype(BF16)
    rwl = (rwt - rwh.astype(F32)).astype(BF16)
    x1, h2t, h2b, idx, gate, rank, counts, ci, cum = _merge(
        a, o, sga, sgb, x2d, mods, norm2_g.reshape(1, D_MODEL),
        w_a.astype(BF16), w_b.astype(BF16), w_o.astype(BF16), rwh, rwl, router_b.reshape(N_EXPERTS, 1), seq)

    cnt = counts[:, 0].astype(jnp.int32)
    pcnt = (cnt + EXP_BLK - 1) // EXP_BLK * EXP_BLK
    pend = jnp.cumsum(pcnt)
    pstart = pend - pcnt
    onehot = idx[:, :, None] == jnp.arange(N_EXPERTS, dtype=jnp.int32)
    dest = jnp.sum(jnp.where(onehot, pstart, 0), axis=-1) + rank
    nblocks = (t * TOP_K) // EXP_BLK + N_EXPERTS
    nval = (pend[-1] // EXP_BLK).astype(jnp.int32)
    bsrc = jnp.minimum(jnp.arange(nblocks, dtype=jnp.int32), nval - 1)
    bexp = jnp.sum((pend[None, :] <= (bsrc * EXP_BLK)[:, None]).astype(jnp.int32), axis=1)
    bexp = jnp.minimum(bexp, N_EXPERTS - 1)

    nval = nval.reshape(1)
    r0 = bsrc * EXP_BLK - jnp.take(pstart, bexp)
    ntile = cum.shape[0]
    cumb = jnp.take(cum[:, :, 0], bexp, axis=1).T[:, :, None]
    tok = _slots(bexp, r0, jnp.take(cnt, bexp), ci.reshape(N_EXPERTS, ntile, SLOT_TILE), cumb)
    ys = _experts(bexp, nval, tok, h2t, e_gate, e_up, e_down)
    tm = min(TM_DMA, seq)
    dest = (dest * ROW_TILES).T.reshape(t // tm, 1, tm * TOP_K)
    return _combine(dest, ys, gate, x1, h2b, mods, s_gate.astype(BF16), s_up.astype(BF16),
                    s_down.astype(BF16), final_g.reshape(1, D_MODEL), seq)


def kernel(x, c, ctx, c_ctx, ada_w, ada_b, norm1_g, norm2_g, w_in, gmlp_ln_g, gmlp_ln_b, gmlp_ws, gmlp_bs,
           attn_sink, w_branch_a, w_branch_b, w_out, router_w, router_b, exp_w_gate, exp_w_up, exp_w_down,
           sh_w_gate, sh_w_up, sh_w_down, final_g):
    batch, seq, _ = x.shape
    depth = ada_w.shape[0]
    assert depth == 1, "the context stream is only carried as keys/values of a single layer"
    assert batch + 1 <= 8 and seq % WBLK == 0
    cond = jnp.concatenate([c, c_ctx[None], jnp.zeros((8 - batch - 1, D_MODEL), F32)], axis=0)
    mods = _ada(cond, ada_w[0], ada_b[0])[:batch + 1].reshape(batch + 1, 6, D_MODEL)
    out = _layer(x.reshape(batch * seq, D_MODEL), ctx.reshape(-1, D_MODEL), mods, batch, seq,
                 norm1_g[0], norm2_g[0], w_in[0], gmlp_ln_g[0], gmlp_ln_b[0], gmlp_ws[0], gmlp_bs[0],
                 attn_sink[0], w_branch_a[0], w_branch_b[0], w_out[0], router_w[0], router_b[0],
                 exp_w_gate[0], exp_w_up[0], exp_w_down[0], sh_w_gate[0], sh_w_up[0], sh_w_down[0], final_g)
    return out.reshape(batch, seq, D_MODEL)
```

```python
import functools

import jax
import jax.numpy as jnp
from jax import lax
from jax.experimental import pallas as pl
from jax.experimental.pallas import tpu as pltpu

F32 = jnp.float32
BF16 = jnp.bfloat16

D_MODEL = 1024
EPS = 1e-6
GRID_W = 64
D_A = D_MODEL // 2
G_A = 4
CHUNK = 128
N_HEADS = 8
N_KV = 2
REP = N_HEADS // N_KV
HEAD_DIM = 64
D_Q = N_HEADS * HEAD_DIM
D_KV = N_KV * HEAD_DIM
WBLK = 128
ROPE_THETA = 10000.0
N_EXPERTS = 256
TOP_K = 8
D_EXPERT = D_MODEL // 4
D_SHARED = D_MODEL // 4
ROUTE_SCALE = 2.5

C_U = 0
C_V = D_A
C_Q = 2 * D_A
C_K = C_Q + D_Q
C_VAL = C_K + D_KV
C_GA = C_VAL + D_KV
C_GB = C_GA + D_MODEL
D_IN = C_GB + D_MODEL

LANES = 128
ROPE_HALF = HEAD_DIM // 4
NEG_BIG = -1e30

TM_PROJ = 512
TM_MERGE = 512
TM_DMA = 256
COMBINE_ROWS = 32
EXP_BLK = 256
SLOT_TILE = 256
IMAP_GROUP = 8
IDS_CHUNK = 8
W_SLOTS = 3
ROW_DT = BF16
ROW_TILES = D_MODEL // LANES
ROW_WORDS = ROW_TILES // 2
WORD_DT = jnp.uint32
VMEM_LIMIT = 56 * 1024 * 1024


def _gelu(x):
    return 0.5 * x * (1.0 + jnp.tanh(0.7978845608028654 * (x + 0.044715 * x * x * x)))


def _silu(x):
    return x * jax.nn.sigmoid(x)


def _dot(a, b):
    return jnp.dot(a, b, preferred_element_type=F32)


def _rms_mod(x, g, shift, scale):
    ms = jnp.mean(x * x, axis=-1, keepdims=True)
    return (x * lax.rsqrt(ms + EPS)) * g * (1.0 + scale) + shift


def _params(*sem):
    return pltpu.CompilerParams(dimension_semantics=sem, vmem_limit_bytes=VMEM_LIMIT)


def _ada_kernel(c_ref, w_ref, b_ref, o_ref):
    c = c_ref[...]
    s = _silu(c).astype(BF16)
    o_ref[...] = _dot(s, w_ref[...].astype(BF16)) + b_ref[...]


def _ada(cond8, ada_w, ada_b):
    n = ada_w.shape[1]
    tn = 1536
    return pl.pallas_call(
        _ada_kernel,
        grid=(n // tn,),
        in_specs=[pl.BlockSpec((8, D_MODEL), lambda j: (0, 0)),
                  pl.BlockSpec((D_MODEL, tn), lambda j: (0, j)),
                  pl.BlockSpec((1, tn), lambda j: (0, j))],
        out_specs=pl.BlockSpec((8, tn), lambda j: (0, j)),
        out_shape=jax.ShapeDtypeStruct((8, n), F32),
        compiler_params=_params("arbitrary"),
        name="ada",
    )(cond8, ada_w, ada_b.reshape(1, n))


def _ctx_kernel(x_ref, mod_ref, g_ref, w_ref, k_ref, v_ref):
    mod = mod_ref[0]
    h = _rms_mod(x_ref[...], g_ref[...], mod[0:1], mod[1:2]).astype(BF16)
    z = _dot(h, w_ref[...])
    k_ref[...] = z[:, :D_KV].astype(BF16)
    v_ref[...] = z[:, D_KV:].astype(BF16)


def _ctx_kv(ctx2d, mods, g1, w_kv):
    n = ctx2d.shape[0]
    nb = mods.shape[0] - 1
    return pl.pallas_call(
        _ctx_kernel,
        grid=(1,),
        in_specs=[pl.BlockSpec((n, D_MODEL), lambda i: (0, 0)),
                  pl.BlockSpec((1, 6, D_MODEL), lambda i: (nb, 0, 0)),
                  pl.BlockSpec((1, D_MODEL), lambda i: (0, 0)),
                  pl.BlockSpec((D_MODEL, 2 * D_KV), lambda i: (0, 0))],
        out_specs=[pl.BlockSpec((n, D_KV), lambda i: (0, 0)),
                   pl.BlockSpec((n, D_KV), lambda i: (0, 0))],
        out_shape=[jax.ShapeDtypeStruct((n, D_KV), BF16)] * 2,
        compiler_params=_params("arbitrary"),
        name="ctx_kv",
    )(ctx2d, mods, g1, w_kv)


def _rope(t, cos, sin):
    lane = lax.broadcasted_iota(jnp.int32, (t.shape[0], LANES), 1)
    first = (lane & (2 * ROPE_HALF - 1)) < ROPE_HALF
    outs = []
    for j in range(t.shape[1] // LANES):
        tj = t[:, j * LANES:(j + 1) * LANES]
        up = pltpu.roll(tj, LANES - ROPE_HALF, 1)
        dn = pltpu.roll(tj, ROPE_HALF, 1)
        outs.append(tj * cos + jnp.where(first, up, dn) * sin)
    return outs


def _inproj_kernel(x_ref, mod_ref, g_ref, w_ref, lng_ref, lnb_ref, cos_ref, sin_ref,
                   gu_ref, vn_ref, q_ref, k_ref, v_ref, sga_ref, sgb_ref):
    mod = mod_ref[0]
    h = _rms_mod(x_ref[...], g_ref[...], mod[0:1], mod[1:2]).astype(BF16)

    def proj(lo, hi):
        return _dot(h, w_ref[:, lo:hi])

    gu_ref[...] = _gelu(proj(C_U, C_V)).astype(BF16)

    v = _gelu(proj(C_V, C_Q))
    mu = jnp.mean(v, axis=-1, keepdims=True)
    vc = v - mu
    var = jnp.mean(vc * vc, axis=-1, keepdims=True)
    vn_ref[...] = (vc * lax.rsqrt(var + EPS) * lng_ref[...] + lnb_ref[...]).astype(BF16)

    cos = cos_ref[...]
    sin = sin_ref[...]
    q = _rope(proj(C_Q, C_K) * (HEAD_DIM ** -0.5), cos, sin)
    for j, qj in enumerate(q):
        q_ref[:, j * LANES:(j + 1) * LANES] = qj.astype(BF16)
    k = _rope(proj(C_K, C_VAL), cos, sin)
    k_ref[...] = k[0].astype(BF16)
    v_ref[...] = proj(C_VAL, C_GA).astype(BF16)
    sga_ref[...] = jax.nn.sigmoid(proj(C_GA, C_GB)).astype(BF16)
    sgb_ref[...] = jax.nn.sigmoid(proj(C_GB, D_IN)).astype(BF16)


def _inproj(x2d, mods, g1, w_in, lng, lnb, cos, sin, seq):
    t = x2d.shape[0]
    tm = min(TM_PROJ, seq)
    spb = seq // tm
    row = lambda w: pl.BlockSpec((tm, w), lambda i: (i, 0))
    const = lambda s: pl.BlockSpec(s, lambda i: (0,) * len(s))
    return pl.pallas_call(
        _inproj_kernel,
        grid=(t // tm,),
        in_specs=[row(D_MODEL),
                  pl.BlockSpec((1, 6, D_MODEL), lambda i: (i // spb, 0, 0)),
                  const((1, D_MODEL)),
                  const((D_MODEL, D_IN)),
                  const((1, D_A)), const((1, D_A)),
                  pl.BlockSpec((tm, LANES), lambda i: (i % spb, 0)),
                  pl.BlockSpec((tm, LANES), lambda i: (i % spb, 0))],
        out_specs=[row(D_A), row(D_A), row(D_Q), row(D_KV), row(D_KV), row(D_MODEL), row(D_MODEL)],
        out_shape=[jax.ShapeDtypeStruct((t, w), BF16)
                   for w in (D_A, D_A, D_Q, D_KV, D_KV, D_MODEL, D_MODEL)],
        compiler_params=_params("arbitrary"),
        name="inproj",
    )(x2d, mods, g1, w_in, lng, lnb, cos, sin)


def _mix_kernel(sink_ref, *refs):
    seq_refs, (ws_ref, bs_ref, mask_ref), out_refs = refs[:11], refs[11:14], refs[14:]
    for b in range(seq_refs[0].shape[0]):
        _mix_block(sink_ref, *[r.at[b] for r in seq_refs], ws_ref, bs_ref, mask_ref, *[r.at[b] for r in out_refs])


def _mix_block(sink_ref, gu_ref, vn_ref, q_ref, kp_ref, kc_ref, kn_ref, vp_ref, vc_ref, vx_ref,
               kctx_ref, vctx_ref, ws_ref, bs_ref, mask_ref, a_ref, o_ref):
    for g in range(G_A):
        sl = slice(g * CHUNK, (g + 1) * CHUNK)
        s = _dot(ws_ref[g], vn_ref[:, sl]) + bs_ref[:, sl]
        a_ref[:, sl] = (gu_ref[:, sl].astype(F32) * s).astype(BF16)

    kcat = jnp.concatenate([kp_ref[...], kc_ref[...], kn_ref[...], kctx_ref[...]], axis=0)
    vcat = jnp.concatenate([vp_ref[...], vc_ref[...], vx_ref[...], vctx_ref[...]], axis=0)
    rows = REP * WBLK
    mask = mask_ref[0]
    rgrp = lax.broadcasted_iota(jnp.int32, (rows, 1), 0) // WBLK

    for kvh in range(N_KV):
        ksl = kcat[:, kvh * HEAD_DIM:(kvh + 1) * HEAD_DIM]
        vsl = vcat[:, kvh * HEAD_DIM:(kvh + 1) * HEAD_DIM]
        qs = jnp.concatenate(
            [q_ref[:, (kvh * REP + r) * HEAD_DIM:(kvh * REP + r + 1) * HEAD_DIM] for r in range(REP)],
            axis=0)
        sink = jnp.zeros((rows, 1), F32)
        for r in range(REP):
            sink = jnp.where(rgrp == r, sink_ref[kvh * REP + r], sink)
        s = lax.dot_general(qs, ksl, (((1,), (1,)), ((), ())), preferred_element_type=F32)
        s = s + mask
        m = jnp.maximum(jnp.max(s, axis=-1, keepdims=True), sink)
        p = jnp.exp(s - m)
        den = jnp.sum(p, axis=-1, keepdims=True) + jnp.exp(sink - m)
        o = _dot(p.astype(BF16), vsl) / den
        for r in range(0, REP, 2):
            pair = jnp.concatenate([o[r * WBLK:(r + 1) * WBLK], o[(r + 1) * WBLK:(r + 2) * WBLK]], axis=1)
            c0 = (kvh * REP + r) * HEAD_DIM
            o_ref[:, c0:c0 + 2 * HEAD_DIM] = pair.astype(BF16)


def _attn_masks(nctx):
    nwin = 3 * WBLK
    qi = (jnp.arange(REP * WBLK) & (WBLK - 1))[:, None]
    kj = jnp.arange(nwin + nctx)[None, :]
    band = (kj >= qi) & (kj <= qi + 2 * WBLK)
    masks = []
    for first in (False, True):
        for last in (False, True):
            lo = WBLK if first else 0
            hi = 2 * WBLK if last else nwin
            valid = (kj >= nwin) | (band & (kj >= lo) & (kj < hi))
            masks.append(jnp.where(valid, 0.0, NEG_BIG))
    return jnp.stack(masks).astype(F32)


def _mix(sink, gu, vn, q, k, v, kctx, vctx, ws, bs_full, batch, seq):
    t = gu.shape[0]
    nblk = seq // WBLK
    nctx = kctx.shape[0] // batch
    masks = _attn_masks(nctx)
    mspec = pl.BlockSpec((1,) + masks.shape[1:],
                         lambda n: (jnp.where(n == 0, 2, 0) + jnp.where(n == nblk - 1, 1, 0), 0, 0))

    cur = lambda w: pl.BlockSpec((batch, WBLK, w), lambda n: (0, n, 0))
    prev = pl.BlockSpec((batch, WBLK, D_KV), lambda n: (0, jnp.maximum(n - 1, 0), 0))
    nxt = pl.BlockSpec((batch, WBLK, D_KV), lambda n: (0, jnp.minimum(n + 1, nblk - 1), 0))
    cblk = pl.BlockSpec((batch, nctx, D_KV), lambda n: (0, 0, 0))
    per_seq = lambda x: x.reshape(batch, -1, x.shape[-1])
    gu, vn, q, k, v, kctx, vctx = map(per_seq, (gu, vn, q, k, v, kctx, vctx))
    a, o = pl.pallas_call(
        _mix_kernel,
        grid=(nblk,),
        in_specs=[pl.BlockSpec(memory_space=pltpu.SMEM),
                  cur(D_A), cur(D_A), cur(D_Q),
                  prev, cur(D_KV), nxt, prev, cur(D_KV), nxt,
                  cblk, cblk,
                  pl.BlockSpec((G_A, CHUNK, CHUNK), lambda n: (0, 0, 0)),
                  pl.BlockSpec((CHUNK, D_A), lambda n: (0, 0)),
                  mspec],
        out_specs=[cur(D_A), cur(D_Q)],
        out_shape=[jax.ShapeDtypeStruct((batch, seq, D_A), BF16), jax.ShapeDtypeStruct((batch, seq, D_Q), BF16)],
        compiler_params=_params("arbitrary"),
        name="mix",
    )(sink, gu, vn, q, k, k, k, v, v, v, kctx, vctx, ws, bs_full, masks)
    return a.reshape(t, D_A), o.reshape(t, D_Q)


def _store_row_tiles(ref, val, wref):
    rows = val.shape[0]
    for s in range(ROW_WORDS):
        lo = val[:, (2 * s) * LANES:(2 * s + 1) * LANES]
        hi = val[:, (2 * s + 1) * LANES:(2 * s + 2) * LANES]
        wref[pl.ds(s, rows, stride=ROW_WORDS), :] = pltpu.pack_elementwise([lo, hi], packed_dtype=BF16)
    ref[...] = pltpu.bitcast(wref[...], ROW_DT)


def _load_row_tiles(wref, s):
    words = wref[pl.ds(s, wref.shape[0] // ROW_WORDS, stride=ROW_WORDS), :]
    return tuple(pltpu.unpack_elementwise(words, index=i, packed_dtype=BF16, unpacked_dtype=F32) for i in (0, 1))


def _merge_kernel(a_ref, o_ref, sga_ref, sgb_ref, x_ref, mod_ref, g2_ref, wa_ref, wb_ref, wo_ref,
                  rwh_ref, rwl_ref, rb_ref,
                  x1_ref, h2t_ref, h2b_ref, idx_ref, gate_ref, rank_ref, cnt_ref, ci_ref, cum_ref,
                  base_ref, words_ref):
    i = pl.program_id(0)

    @pl.when(i == 0)
    def _():
        base_ref[...] = jnp.zeros_like(base_ref)

    mod = mod_ref[0]
    ya = _dot(a_ref[...], wa_ref[...])
    yb = _dot(o_ref[...], wb_ref[...])
    y = sga_ref[...].astype(F32) * ya + sgb_ref[...].astype(F32) * yb
    x1 = x_ref[...] + mod[2:3] * _dot(y.astype(BF16), wo_ref[...])
    x1_ref[...] = x1
    h2 = _rms_mod(x1, g2_ref[...], mod[3:4], mod[4:5])
    h2b_ref[...] = h2.astype(BF16)
    _store_row_tiles(h2t_ref, h2, words_ref)

    hh = h2.astype(BF16)
    hl = (h2 - hh.astype(F32)).astype(BF16)
    dn = (((1,), (1,)), ((), ()))
    logits = (lax.dot_general(rwh_ref[...], hh, dn, preferred_element_type=F32)
              + lax.dot_general(rwl_ref[...], hh, dn, preferred_element_type=F32)
              + lax.dot_general(rwh_ref[...], hl, dn, preferred_element_type=F32))
    scores = jax.nn.sigmoid(logits)
    tm = scores.shape[1]
    eio = lax.broadcasted_iota(jnp.int32, scores.shape, 0).astype(F32)
    work = scores + rb_ref[...]
    picked = jnp.zeros(scores.shape, F32)
    idxs, vals = [], []
    for _ in range(TOP_K):
        m = jnp.max(work, axis=0, keepdims=True)
        ik = jnp.min(jnp.where(work == m, eio, float(N_EXPERTS)), axis=0, keepdims=True)
        oh = eio == ik
        vals.append(jnp.sum(jnp.where(oh, scores, 0.0), axis=0, keepdims=True))
        idxs.append(ik)
        work = jnp.where(oh, -jnp.inf, work)
        picked = picked + oh.astype(F32)
    total = vals[0]
    for vk in vals[1:]:
        total = total + vk

    tr = lax.broadcasted_iota(jnp.int32, (tm, tm), 0)
    tc = lax.broadcasted_iota(jnp.int32, (tm, tm), 1)
    before = (tr < tc).astype(BF16)
    base = base_ref[...]
    earlier = _dot(picked.astype(BF16), before)
    rank_e = earlier + base
    ranks = [jnp.sum(jnp.where(eio == ik, rank_e, 0.0), axis=0, keepdims=True) for ik in idxs]
    for j in range(tm // SLOT_TILE):
        sl = slice(j * SLOT_TILE, (j + 1) * SLOT_TILE)
        ci_ref[:, sl] = (earlier[:, sl] + picked[:, sl] - (base - base_ref[...])).astype(BF16)
        base = base + jnp.sum(picked[:, sl], axis=1, keepdims=True)
        cum_ref[j] = base
    base_ref[...] = base
    cnt_ref[...] = base

    idx_ref[...] = jnp.concatenate(idxs, axis=0).astype(jnp.int32)
    rank_ref[...] = jnp.concatenate(ranks, axis=0).astype(jnp.int32)
    gate_ref[...] = jnp.concatenate(vals, axis=0) * (ROUTE_SCALE / total)


def _merge(a, o, sga, sgb, x2d, mods, g2, wa, wb, wo, rwh, rwl, rb, seq):
    t = x2d.shape[0]
    tm = min(TM_MERGE, seq)
    spb = seq // tm
    row = lambda w: pl.BlockSpec((tm, w), lambda i: (i, 0))
    col = lambda: pl.BlockSpec((TOP_K, tm), lambda i: (0, i))
    const = lambda s: pl.BlockSpec(s, lambda i: (0,) * len(s))
    return pl.pallas_call(
        _merge_kernel,
        grid=(t // tm,),
        in_specs=[row(D_A), row(D_Q), row(D_MODEL), row(D_MODEL), row(D_MODEL),
                  pl.BlockSpec((1, 6, D_MODEL), lambda i: (i // spb, 0, 0)),
                  const((1, D_MODEL)),
                  const((D_A, D_MODEL)), const((D_Q, D_MODEL)), const((D_MODEL, D_MODEL)),
                  const((N_EXPERTS, D_MODEL)), const((N_EXPERTS, D_MODEL)), const((N_EXPERTS, 1))],
        out_specs=[row(D_MODEL), pl.BlockSpec((tm * ROW_TILES, LANES), lambda i: (i, 0)), row(D_MODEL),
                   col(), col(), col(), const((N_EXPERTS, 1)),
                   pl.BlockSpec((N_EXPERTS, tm), lambda i: (0, i)),
                   pl.BlockSpec((tm // SLOT_TILE, N_EXPERTS, 1), lambda i: (i, 0, 0))],
        out_shape=[jax.ShapeDtypeStruct((t, D_MODEL), F32),
                   jax.ShapeDtypeStruct((t * ROW_TILES, LANES), ROW_DT),
                   jax.ShapeDtypeStruct((t, D_MODEL), BF16),
                   jax.ShapeDtypeStruct((TOP_K, t), jnp.int32),
                   jax.ShapeDtypeStruct((TOP_K, t), F32),
                   jax.ShapeDtypeStruct((TOP_K, t), jnp.int32),
                   jax.ShapeDtypeStruct((N_EXPERTS, 1), F32),
                   jax.ShapeDtypeStruct((N_EXPERTS, t), BF16),
                   jax.ShapeDtypeStruct((t // SLOT_TILE, N_EXPERTS, 1), F32)],
        scratch_shapes=[pltpu.VMEM((N_EXPERTS, 1), F32), pltpu.VMEM((tm * ROW_WORDS, LANES), WORD_DT)],
        compiler_params=_params("arbitrary"),
        name="merge",
    )(a, o, sga, sgb, x2d, mods, g2, wa, wb, wo, rwh, rwl, rb)


def _row_copy(src, s_off, dst, d_off, sem):
    return pltpu.make_async_copy(src.at[pl.ds(pl.multiple_of(s_off, ROW_TILES), ROW_TILES), :],
                                 dst.at[pl.ds(pl.multiple_of(d_off, ROW_TILES), ROW_TILES), :], sem)


def _rows_wait(ref, nrows, sem):
    n = nrows * ROW_TILES
    pltpu.make_async_copy(ref.at[pl.ds(0, n), :], ref.at[pl.ds(0, n), :], sem).wait()


def _slots_kernel(bexp_ref, r0_ref, cnt_ref, *refs):
    ci_refs, cum_ref, tok_ref = refs[:IMAP_GROUP], refs[IMAP_GROUP], refs[IMAP_GROUP + 1]
    i = pl.program_id(0)
    nt, tm = ci_refs[0].shape[1:]
    lane = lax.broadcasted_iota(jnp.int32, (1, EXP_BLK), 1)
    tile = lax.broadcasted_iota(jnp.int32, (nt, EXP_BLK), 0).astype(F32)
    for g in range(IMAP_GROUP):
        b = i * IMAP_GROUP + g
        r = (r0_ref[b] + lane).astype(F32)
        cum = cum_ref[g]
        jstar = jnp.sum((cum <= r).astype(F32), axis=0, keepdims=True)
        onehot = (tile == jstar).astype(BF16)
        before = jnp.sum(jnp.where(tile == jstar - 1.0, cum, 0.0), axis=0, keepdims=True)
        isel = lax.dot_general(ci_refs[g][0], onehot, (((0,), (0,)), ((), ())),
                               preferred_element_type=F32)
        tloc = jnp.sum((isel <= r - before).astype(F32), axis=0, keepdims=True)
        tok = (jstar * tm + tloc).astype(jnp.int32)
        tok_ref[g] = jnp.where((r0_ref[b] + lane) < cnt_ref[b], tok, 0)


def _slots(bexp, r0, cntb, ci3, cumb):
    nblocks = bexp.shape[0]
    nt, tm = ci3.shape[1:]
    per = lambda g: pl.BlockSpec((1, nt, tm), lambda i, be, *_: (be[i * IMAP_GROUP + g], 0, 0))
    grid_spec = pltpu.PrefetchScalarGridSpec(
        num_scalar_prefetch=3,
        grid=(nblocks // IMAP_GROUP,),
        in_specs=[per(g) for g in range(IMAP_GROUP)] + [pl.BlockSpec((IMAP_GROUP, nt, 1), lambda i, *_: (i, 0, 0))],
        out_specs=pl.BlockSpec((IMAP_GROUP, 1, EXP_BLK), lambda i, *_: (i, 0, 0)),
    )
    return pl.pallas_call(
        _slots_kernel,
        grid_spec=grid_spec,
        out_shape=jax.ShapeDtypeStruct((nblocks, 1, EXP_BLK), jnp.int32),
        compiler_params=_params("arbitrary"),
        name="slots",
    )(bexp, r0, cntb, *([ci3] * IMAP_GROUP), cumb)


def _expert_kernel(nval_ref, first_ref, run_ref, rexp_ref, nruns_ref,
                   tok_hbm, h_hbm, wg_hbm, wu_hbm, wd_hbm, ys_ref,
                   ids_ref, xin_ref, xb_ref, xw_ref, wg_f, wu_f, wd_f, wg_b, wu_b, wd_b, wsem, gsem, tsem):
    b = pl.program_id(0)
    nruns = nruns_ref[0]
    slot = b % 2

    chunk = b // IDS_CHUNK
    cslot = chunk % 2
    off = b % IDS_CHUNK
    nchunks = pl.num_programs(0) // IDS_CHUNK

    def ids(c, act):
        rows = pl.ds(pl.multiple_of(c * IDS_CHUNK, IDS_CHUNK), 2 * IDS_CHUNK)
        act(pltpu.make_async_copy(tok_hbm.at[rows], ids_ref.at[c % 2], tsem.at[c % 2]))

    @pl.when(b == 0)
    def _():
        ids(0, lambda cp: cp.start())

    @pl.when(off == 0)
    def _():
        ids(chunk, lambda cp: cp.wait())

        @pl.when(chunk + 1 < nchunks)
        def _():
            ids(chunk + 1, lambda cp: cp.start())

    @pl.when(b == 0)
    def _():
        def body(r, carry):
            _row_copy(h_hbm, ids_ref[0, 0, r] * ROW_TILES, xin_ref.at[0], r * ROW_TILES, gsem.at[0]).start()
            return carry
        lax.fori_loop(0, EXP_BLK, body, 0)

    def weights(j, act):
        e = rexp_ref[j]
        s = j % W_SLOTS
        for hbm, buf in ((wg_hbm, wg_f), (wu_hbm, wu_f), (wd_hbm, wd_f)):
            act(pltpu.make_async_copy(hbm.at[e], buf.at[s], wsem.at[s]))

    @pl.when(b == 0)
    def _():
        weights(0, lambda cp: cp.start(priority=1))

        @pl.when(nruns > 1)
        def _():
            weights(1, lambda cp: cp.start(priority=1))

    @pl.when(b < nval_ref[0])
    def _():
        j = run_ref[b]

        @pl.when(first_ref[b] == 1)
        def _():
            weights(j, lambda cp: cp.wait())

            @pl.when(j + 2 < nruns)
            def _():
                weights(j + 2, lambda cp: cp.start(priority=1))

            s = j % W_SLOTS
            wg_b[...] = wg_f[s].astype(BF16)
            wu_b[...] = wu_f[s].astype(BF16)
            wd_b[...] = wd_f[s].astype(BF16)

        _rows_wait(xin_ref.at[slot], EXP_BLK, gsem.at[slot])
        xw_ref[...] = pltpu.bitcast(xin_ref[slot], WORD_DT)
        for r in range(EXP_BLK):
            _row_copy(h_hbm, ids_ref[cslot, off + 1, r] * ROW_TILES, xin_ref.at[1 - slot], r * ROW_TILES,
                      gsem.at[1 - slot]).start()
        for s in range(ROW_WORDS):
            for h, part in enumerate(_load_row_tiles(xw_ref, s)):
                c = 2 * s + h
                xb_ref[:, c * LANES:(c + 1) * LANES] = part.astype(BF16)
        x = xb_ref[...]
        g = _dot(x, wg_b[...])
        u = _dot(x, wu_b[...])
        act = (_silu(g) * u).astype(BF16)
        _store_row_tiles(ys_ref, _dot(act, wd_b[...]), xw_ref)

        @pl.when(b == nval_ref[0] - 1)
        def _():
            _rows_wait(xin_ref.at[1 - slot], EXP_BLK, gsem.at[1 - slot])

    @pl.when(b >= nval_ref[0])
    def _():
        ys_ref[...] = jnp.zeros_like(ys_ref)


def _experts(bexp, nval, tok, h2t, wg, wu, wd):
    nb = bexp.shape[0]
    nrows = nb * EXP_BLK
    blk = (EXP_BLK * ROW_TILES, LANES)
    assert nb % IDS_CHUNK == 0
    tok = jnp.pad(tok.reshape(nb, EXP_BLK), ((0, IDS_CHUNK), (0, 0)))
    blocks = jnp.arange(nb, dtype=jnp.int32)
    first = ((blocks == 0) | (bexp != jnp.roll(bexp, 1))).astype(jnp.int32)
    run = jnp.cumsum(first) - 1
    runs = jnp.arange(N_EXPERTS, dtype=jnp.int32)
    rexp = jnp.sum(jnp.where((first[None, :] == 1) & (run[None, :] == runs[:, None]), bexp[None, :], 0), axis=1)
    nruns = (run[nb - 1] + 1).reshape(1)
    grid_spec = pltpu.PrefetchScalarGridSpec(
        num_scalar_prefetch=5,
        grid=(nb,),
        in_specs=[pl.BlockSpec(memory_space=pl.ANY)] * 5,
        out_specs=pl.BlockSpec(blk, lambda b, *_: (b, 0)),
        scratch_shapes=[pltpu.SMEM((2, 2 * IDS_CHUNK, EXP_BLK), jnp.int32),
                        pltpu.VMEM((2, EXP_BLK * ROW_TILES, LANES), ROW_DT),
                        pltpu.VMEM((EXP_BLK, D_MODEL), BF16),
                        pltpu.VMEM((EXP_BLK * ROW_WORDS, LANES), WORD_DT),
                        pltpu.VMEM((W_SLOTS, D_MODEL, D_EXPERT), F32),
                        pltpu.VMEM((W_SLOTS, D_MODEL, D_EXPERT), F32),
                        pltpu.VMEM((W_SLOTS, D_EXPERT, D_MODEL), F32),
                        pltpu.VMEM((D_MODEL, D_EXPERT), BF16),
                        pltpu.VMEM((D_MODEL, D_EXPERT), BF16),
                        pltpu.VMEM((D_EXPERT, D_MODEL), BF16),
                        pltpu.SemaphoreType.DMA((W_SLOTS,)), pltpu.SemaphoreType.DMA((2,)),
                        pltpu.SemaphoreType.DMA((2,))],
    )
    return pl.pallas_call(
        _expert_kernel,
        grid_spec=grid_spec,
        out_shape=jax.ShapeDtypeStruct((nrows * ROW_TILES, LANES), ROW_DT),
        compiler_params=_params("arbitrary"),
        name="experts",
    )(nval, first, run, rexp, nruns, tok, h2t, wg, wu, wd)


def _combine_kernel(dest_ref, next_ref, ys_ref, gate_ref, x1_ref, h2_ref, mod_ref, sg_ref, su_ref, sd_ref, fg_ref,
                    out_ref, buf_ref, bufw_ref, x2_ref, gk_ref, sem):
    i = pl.program_id(0)
    tm = x1_ref.shape[0]
    slot = i % 2

    def gather(ids_ref, s):
        def body(j, carry):
            for k in range(TOP_K):
                _row_copy(ys_ref, ids_ref[0, 0, j * TOP_K + k], buf_ref.at[s, k], j * ROW_TILES,
                          sem.at[s]).start(priority=k % 2)
            return carry
        lax.fori_loop(0, tm, body, 0)

    @pl.when(i == 0)
    def _():
        gather(dest_ref, 0)

    mod = mod_ref[0]
    h = h2_ref[...]
    act = (_silu(_dot(h, sg_ref[...])) * _dot(h, su_ref[...])).astype(BF16)
    moe = _dot(act, sd_ref[...])

    g = gate_ref[...]
    g0 = g.astype(BF16)
    r1 = g - g0.astype(F32)
    g1 = r1.astype(BF16)
    g2 = (r1 - g1.astype(F32)).astype(BF16)
    eye = (lax.broadcasted_iota(jnp.int32, (tm, tm), 0)
           == lax.broadcasted_iota(jnp.int32, (tm, tm), 1)).astype(BF16)
    dn = (((1,), (1,)), ((), ()))
    gcol = (lax.dot_general(eye, g0, dn, preferred_element_type=F32)
            + lax.dot_general(eye, g1, dn, preferred_element_type=F32)
            + lax.dot_general(eye, g2, dn, preferred_element_type=F32))

    for k in range(TOP_K):
        gk_ref[k] = jnp.broadcast_to(gcol[:, k:k + 1], (tm, LANES))
    x2_ref[...] = moe
    for k in range(TOP_K):
        _rows_wait(buf_ref.at[slot, k], tm, sem.at[slot])
    for k in range(TOP_K):
        bufw_ref[k] = pltpu.bitcast(buf_ref[slot, k], WORD_DT)
    nxt = 1 - slot

    def chunk(rc, carry):
        r0 = pl.multiple_of(rc * COMBINE_ROWS, COMBINE_ROWS)
        rs = pl.ds(r0, COMBINE_ROWS)
        for jj in range(COMBINE_ROWS):
            for k in range(TOP_K):
                _row_copy(ys_ref, next_ref[0, 0, (r0 + jj) * TOP_K + k], buf_ref.at[nxt, k],
                          (r0 + jj) * ROW_TILES, sem.at[nxt]).start(priority=k % 2)
        ssq = jnp.zeros((COMBINE_ROWS, 1), F32)
        for s in range(ROW_WORDS):
            acc = [x2_ref[rs, (2 * s + h) * LANES:(2 * s + h + 1) * LANES] for h in range(2)]
            for k in range(TOP_K):
                words = bufw_ref[k, pl.ds(r0 * ROW_WORDS + s, COMBINE_ROWS, stride=ROW_WORDS), :]
                g = gk_ref[k, rs, :]
                for h in range(2):
                    acc[h] = acc[h] + g * pltpu.unpack_elementwise(words, index=h, packed_dtype=BF16,
                                                                   unpacked_dtype=F32)
            for h in range(2):
                sl = slice((2 * s + h) * LANES, (2 * s + h + 1) * LANES)
                x2 = x1_ref[rs, sl] + mod[5:6, sl] * acc[h]
                x2_ref[rs, sl] = x2
                ssq = ssq + jnp.sum(x2 * x2, axis=-1, keepdims=True)
        out_ref[rs, :] = x2_ref[rs, :] * lax.rsqrt(ssq * (1.0 / D_MODEL) + EPS) * fg_ref[...]
        return carry

    lax.fori_loop(0, tm // COMBINE_ROWS, chunk, 0)

    @pl.when(i == pl.num_programs(0) - 1)
    def _():
        for k in range(TOP_K):
            _rows_wait(buf_ref.at[nxt, k], tm, sem.at[nxt])


def _combine(dest, ys, gate, x1, h2, mods, sg, su, sd, fg, seq):
    t = x1.shape[0]
    tm = dest.shape[2] // TOP_K
    spb = seq // tm
    nt = t // tm
    row = lambda: pl.BlockSpec((tm, D_MODEL), lambda i: (i, 0))
    const = lambda s: pl.BlockSpec(s, lambda i: (0,) * len(s))
    ids = lambda f: pl.BlockSpec((1, 1, tm * TOP_K), f, memory_space=pltpu.SMEM)
    return pl.pallas_call(
        _combine_kernel,
        grid=(nt,),
        in_specs=[ids(lambda i: (i, 0, 0)), ids(lambda i: (jnp.minimum(i + 1, nt - 1), 0, 0)),
                  pl.BlockSpec(memory_space=pl.ANY),
                  pl.BlockSpec((TOP_K, tm), lambda i: (0, i)),
                  row(), row(),
                  pl.BlockSpec((1, 6, D_MODEL), lambda i: (i // spb, 0, 0)),
                  const((D_MODEL, D_SHARED)), const((D_MODEL, D_SHARED)), const((D_SHARED, D_MODEL)),
                  const((1, D_MODEL))],
        out_specs=row(),
        out_shape=jax.ShapeDtypeStruct((t, D_MODEL), F32),
        scratch_shapes=[pltpu.VMEM((2, TOP_K, tm * ROW_TILES, LANES), ROW_DT),
                        pltpu.VMEM((TOP_K, tm * ROW_WORDS, LANES), WORD_DT), pltpu.VMEM((tm, D_MODEL), F32),
                        pltpu.VMEM((TOP_K, tm, LANES), F32),
                        pltpu.SemaphoreType.DMA((2,))],
        compiler_params=_params("arbitrary"),
        name="combine",
    )(dest, dest, ys, gate, x1, h2, mods, sg, su, sd, fg)


def _rope_tables(seq):
    pos = jnp.arange(seq)
    n_freq = HEAD_DIM // 4
    inv = ROPE_THETA ** (-jnp.arange(n_freq, dtype=F32) / n_freq)
    ang_r = (pos // GRID_W)[:, None].astype(F32) * inv
    ang_c = (pos % GRID_W)[:, None].astype(F32) * inv
    cr, sr, cc, sc = jnp.cos(ang_r), jnp.sin(ang_r), jnp.cos(ang_c), jnp.sin(ang_c)
    cos = jnp.concatenate([cr, cr, cc, cc], axis=1)
    sin = jnp.concatenate([-sr, sr, -sc, sc], axis=1)
    reps = LANES // HEAD_DIM
    return jnp.tile(cos, (1, reps)), jnp.tile(sin, (1, reps))


def _layer(x2d, ctx2d, mods, batch, seq, norm1_g, norm2_g, w_in, ln_g, ln_b, gmlp_ws, gmlp_bs, sink,
           w_a, w_b, w_o, router_w, router_b, e_gate, e_up, e_down, s_gate, s_up, s_down, final_g):
    t = x2d.shape[0]
    g1 = norm1_g.reshape(1, D_MODEL)
    w_in_b = w_in.astype(BF16)
    cos, sin = _rope_tables(seq)

    kctx, vctx = _ctx_kv(ctx2d, mods, g1, w_in_b[:, C_K:C_GA])
    gu, vn, q, k, v, sga, sgb = _inproj(x2d, mods, g1, w_in_b, ln_g.reshape(1, D_A), ln_b.reshape(1, D_A),
                                        cos, sin, seq)

    bs_full = jnp.repeat(gmlp_bs.T, D_A // G_A, axis=1)
    a, o = _mix(sink, gu, vn, q, k, v, kctx, vctx, gmlp_ws.astype(BF16), bs_full, batch, seq)

    rwt = router_w.T
    rwh = rwt.astype(BF16)
    rwl = (rwt - rwh.astype(F32)).astype(BF16)
    x1, h2t, h2b, idx, gate, rank, counts, ci, cum = _merge(
        a, o, sga, sgb, x2d, mods, norm2_g.reshape(1, D_MODEL),
        w_a.astype(BF16), w_b.astype(BF16), w_o.astype(BF16), rwh, rwl, router_b.reshape(N_EXPERTS, 1), seq)

    cnt = counts[:, 0].astype(jnp.int32)
    pcnt = (cnt + EXP_BLK - 1) // EXP_BLK * EXP_BLK
    pend = jnp.cumsum(pcnt)
    pstart = pend - pcnt
    onehot = idx[:, :, None] == jnp.arange(N_EXPERTS, dtype=jnp.int32)
    dest = jnp.sum(jnp.where(onehot, pstart, 0), axis=-1) + rank
    nblocks = (t * TOP_K) // EXP_BLK + N_EXPERTS
    nval = (pend[-1] // EXP_BLK).astype(jnp.int32)
    bsrc = jnp.minimum(jnp.arange(nblocks, dtype=jnp.int32), nval - 1)
    bexp = jnp.sum((pend[None, :] <= (bsrc * EXP_BLK)[:, None]).astype(jnp.int32), axis=1)
    bexp = jnp.minimum(bexp, N_EXPERTS - 1)

    nval = nval.reshape(1)
    r0 = bsrc * EXP_BLK - jnp.take(pstart, bexp)
    ntile = cum.shape[0]
    cumb = jnp.take(cum[:, :, 0], bexp, axis=1).T[:, :, None]
    tok = _slots(bexp, r0, jnp.take(cnt, bexp), ci.reshape(N_EXPERTS, ntile, SLOT_TILE), cumb)
    ys = _experts(bexp, nval, tok, h2t, e_gate, e_up, e_down)
    tm = min(TM_DMA, seq)
    dest = (dest * ROW_TILES).T.reshape(t // tm, 1, tm * TOP_K)
    return _combine(dest, ys, gate, x1, h2b, mods, s_gate.astype(BF16), s_up.astype(BF16),
                    s_down.astype(BF16), final_g.reshape(1, D_MODEL), seq)


def kernel(x, c, ctx, c_ctx, ada_w, ada_b, norm1_g, norm2_g, w_in, gmlp_ln_g, gmlp_ln_b, gmlp_ws, gmlp_bs,
           attn_sink, w_branch_a, w_branch_b, w_out, router_w, router_b, exp_w_gate, exp_w_up, exp_w_down,
           sh_w_gate, sh_w_up, sh_w_down, final_g):
    batch, seq, _ = x.shape
    depth = ada_w.shape[0]
    assert depth == 1, "the context stream is only carried as keys/values of a single layer"
    assert batch + 1 <= 8 and seq % WBLK == 0
    cond = jnp.concatenate([c, c_ctx[None], jnp.zeros((8 - batch - 1, D_MODEL), F32)], axis=0)
    mods = _ada(cond, ada_w[0], ada_b[0])[:batch + 1].reshape(batch + 1, 6, D_MODEL)
    out = _layer(x.reshape(batch * seq, D_MODEL), ctx.reshape(-1, D_MODEL), mods, batch, seq,
                 norm1_g[0], norm2_g[0], w_in[0], gmlp_ln_g[0], gmlp_ln_b[0], gmlp_ws[0], gmlp_bs[0],
                 attn_sink[0], w_branch_a[0], w_branch_b[0], w_out[0], router_w[0], router_b[0],
                 exp_w_gate[0], exp_w_up[0], exp_w_down[0], sh_w_gate[0], sh_w_up[0], sh_w_down[0], final_g)
    return out.reshape(batch, seq, D_MODEL)
```

```python
import functools

import jax
import jax.numpy as jnp
from jax import lax
from jax.experimental import pallas as pl
from jax.experimental.pallas import tpu as pltpu

F32 = jnp.float32
BF16 = jnp.bfloat16

D_MODEL = 1024
EPS = 1e-6
GRID_W = 64
D_A = D_MODEL // 2
G_A = 4
CHUNK = 128
N_HEADS = 8
N_KV = 2
REP = N_HEADS // N_KV
HEAD_DIM = 64
D_Q = N_HEADS * HEAD_DIM
D_KV = N_KV * HEAD_DIM
WBLK = 128
ROPE_THETA = 10000.0
N_EXPERTS = 256
TOP_K = 8
D_EXPERT = D_MODEL // 4
D_SHARED = D_MODEL // 4
ROUTE_SCALE = 2.5

C_U = 0
C_V = D_A
C_Q = 2 * D_A
C_K = C_Q + D_Q
C_VAL = C_K + D_KV
C_GA = C_VAL + D_KV
C_GB = C_GA + D_MODEL
D_IN = C_GB + D_MODEL

LANES = 128
ROPE_HALF = HEAD_DIM // 4
NEG_BIG = -1e30

TM_PROJ = 512
TM_MERGE = 512
TM_DMA = 256
COMBINE_ROWS = 32
COMBINE_PRE = 64
EXP_BLK = 256
W_SLOTS = 3
ROW_DT = BF16
ROW_TILES = D_MODEL // LANES
ROW_WORDS = ROW_TILES // 2
WORD_DT = jnp.uint32
VMEM_LIMIT = 56 * 1024 * 1024


def _gelu(x):
    return 0.5 * x * (1.0 + jnp.tanh(0.7978845608028654 * (x + 0.044715 * x * x * x)))


def _silu(x):
    return x * jax.nn.sigmoid(x)


def _dot(a, b):
    return jnp.dot(a, b, preferred_element_type=F32)


def _rms_mod(x, g, shift, scale):
    ms = jnp.mean(x * x, axis=-1, keepdims=True)
    return (x * lax.rsqrt(ms + EPS)) * g * (1.0 + scale) + shift


def _params(*sem):
    return pltpu.CompilerParams(dimension_semantics=sem, vmem_limit_bytes=VMEM_LIMIT)


def _ada_kernel(c_ref, w_ref, b_ref, o_ref):
    c = c_ref[...]
    s = _silu(c).astype(BF16)
    o_ref[...] = _dot(s, w_ref[...].astype(BF16)) + b_ref[...]


def _ada(cond8, ada_w, ada_b):
    n = ada_w.shape[1]
    tn = 1536
    return pl.pallas_call(
        _ada_kernel,
        grid=(n // tn,),
        in_specs=[pl.BlockSpec((8, D_MODEL), lambda j: (0, 0)),
                  pl.BlockSpec((D_MODEL, tn), lambda j: (0, j)),
                  pl.BlockSpec((1, tn), lambda j: (0, j))],
        out_specs=pl.BlockSpec((8, tn), lambda j: (0, j)),
        out_shape=jax.ShapeDtypeStruct((8, n), F32),
        compiler_params=_params("arbitrary"),
        name="ada",
    )(cond8, ada_w, ada_b.reshape(1, n))


def _ctx_kernel(x_ref, mod_ref, g_ref, w_ref, k_ref, v_ref):
    mod = mod_ref[0]
    h = _rms_mod(x_ref[...], g_ref[...], mod[0:1], mod[1:2]).astype(BF16)
    z = _dot(h, w_ref[...])
    k_ref[...] = z[:, :D_KV].astype(BF16)
    v_ref[...] = z[:, D_KV:].astype(BF16)


def _ctx_kv(ctx2d, mods, g1, w_kv):
    n = ctx2d.shape[0]
    nb = mods.shape[0] - 1
    return pl.pallas_call(
        _ctx_kernel,
        grid=(1,),
        in_specs=[pl.BlockSpec((n, D_MODEL), lambda i: (0, 0)),
                  pl.BlockSpec((1, 6, D_MODEL), lambda i: (nb, 0, 0)),
                  pl.BlockSpec((1, D_MODEL), lambda i: (0, 0)),
                  pl.BlockSpec((D_MODEL, 2 * D_KV), lambda i: (0, 0))],
        out_specs=[pl.BlockSpec((n, D_KV), lambda i: (0, 0)),
                   pl.BlockSpec((n, D_KV), lambda i: (0, 0))],
        out_shape=[jax.ShapeDtypeStruct((n, D_KV), BF16)] * 2,
        compiler_params=_params("arbitrary"),
        name="ctx_kv",
    )(ctx2d, mods, g1, w_kv)


def _rope(t, cos, sin):
    lane = lax.broadcasted_iota(jnp.int32, (t.shape[0], LANES), 1)
    first = (lane & (2 * ROPE_HALF - 1)) < ROPE_HALF
    outs = []
    for j in range(t.shape[1] // LANES):
        tj = t[:, j * LANES:(j + 1) * LANES]
        up = pltpu.roll(tj, LANES - ROPE_HALF, 1)
        dn = pltpu.roll(tj, ROPE_HALF, 1)
        outs.append(tj * cos + jnp.where(first, up, dn) * sin)
    return outs


def _inproj_kernel(x_ref, mod_ref, g_ref, w_ref, lng_ref, lnb_ref, cos_ref, sin_ref,
                   gu_ref, vn_ref, q_ref, k_ref, v_ref, sga_ref, sgb_ref):
    mod = mod_ref[0]
    h = _rms_mod(x_ref[...], g_ref[...], mod[0:1], mod[1:2]).astype(BF16)

    def proj(lo, hi):
        return _dot(h, w_ref[:, lo:hi])

    gu_ref[...] = _gelu(proj(C_U, C_V)).astype(BF16)

    v = _gelu(proj(C_V, C_Q))
    mu = jnp.mean(v, axis=-1, keepdims=True)
    vc = v - mu
    var = jnp.mean(vc * vc, axis=-1, keepdims=True)
    vn_ref[...] = (vc * lax.rsqrt(var + EPS) * lng_ref[...] + lnb_ref[...]).astype(BF16)

    cos = cos_ref[...]
    sin = sin_ref[...]
    q = _rope(proj(C_Q, C_K) * (HEAD_DIM ** -0.5), cos, sin)
    for j, qj in enumerate(q):
        q_ref[:, j * LANES:(j + 1) * LANES] = qj.astype(BF16)
    k = _rope(proj(C_K, C_VAL), cos, sin)
    k_ref[...] = k[0].astype(BF16)
    v_ref[...] = proj(C_VAL, C_GA).astype(BF16)
    sga_ref[...] = jax.nn.sigmoid(proj(C_GA, C_GB)).astype(BF16)
    sgb_ref[...] = jax.nn.sigmoid(proj(C_GB, D_IN)).astype(BF16)


def _inproj(x2d, mods, g1, w_in, lng, lnb, cos, sin, seq):
    t = x2d.shape[0]
    tm = min(TM_PROJ, seq)
    spb = seq // tm
    row = lambda w: pl.BlockSpec((tm, w), lambda i: (i, 0))
    const = lambda s: pl.BlockSpec(s, lambda i: (0,) * len(s))
    return pl.pallas_call(
        _inproj_kernel,
        grid=(t // tm,),
        in_specs=[row(D_MODEL),
                  pl.BlockSpec((1, 6, D_MODEL), lambda i: (i // spb, 0, 0)),
                  const((1, D_MODEL)),
                  const((D_MODEL, D_IN)),
                  const((1, D_A)), const((1, D_A)),
                  pl.BlockSpec((tm, LANES), lambda i: (i % spb, 0)),
                  pl.BlockSpec((tm, LANES), lambda i: (i % spb, 0))],
        out_specs=[row(D_A), row(D_A), row(D_Q), row(D_KV), row(D_KV), row(D_MODEL), row(D_MODEL)],
        out_shape=[jax.ShapeDtypeStruct((t, w), BF16)
                   for w in (D_A, D_A, D_Q, D_KV, D_KV, D_MODEL, D_MODEL)],
        compiler_params=_params("arbitrary"),
        name="inproj",
    )(x2d, mods, g1, w_in, lng, lnb, cos, sin)


def _mix_kernel(sink_ref, *refs):
    seq_refs, (ws_ref, bs_ref, mask_ref), out_refs = refs[:11], refs[11:14], refs[14:]
    for b in range(seq_refs[0].shape[0]):
        _mix_block(sink_ref, *[r.at[b] for r in seq_refs], ws_ref, bs_ref, mask_ref, *[r.at[b] for r in out_refs])


def _mix_block(sink_ref, gu_ref, vn_ref, q_ref, kp_ref, kc_ref, kn_ref, vp_ref, vc_ref, vx_ref,
               kctx_ref, vctx_ref, ws_ref, bs_ref, mask_ref, a_ref, o_ref):
    for g in range(G_A):
        sl = slice(g * CHUNK, (g + 1) * CHUNK)
        s = _dot(ws_ref[g], vn_ref[:, sl]) + bs_ref[:, sl]
        a_ref[:, sl] = (gu_ref[:, sl].astype(F32) * s).astype(BF16)

    kcat = jnp.concatenate([kp_ref[...], kc_ref[...], kn_ref[...], kctx_ref[...]], axis=0)
    vcat = jnp.concatenate([vp_ref[...], vc_ref[...], vx_ref[...], vctx_ref[...]], axis=0)
    rows = REP * WBLK
    mask = mask_ref[0]
    rgrp = lax.broadcasted_iota(jnp.int32, (rows, 1), 0) // WBLK

    for kvh in range(N_KV):
        ksl = kcat[:, kvh * HEAD_DIM:(kvh + 1) * HEAD_DIM]
        vsl = vcat[:, kvh * HEAD_DIM:(kvh + 1) * HEAD_DIM]
        qs = jnp.concatenate(
            [q_ref[:, (kvh * REP + r) * HEAD_DIM:(kvh * REP + r + 1) * HEAD_DIM] for r in range(REP)],
            axis=0)
        sink = jnp.zeros((rows, 1), F32)
        for r in range(REP):
            sink = jnp.where(rgrp == r, sink_ref[kvh * REP + r], sink)
        s = lax.dot_general(qs, ksl, (((1,), (1,)), ((), ())), preferred_element_type=F32)
        s = s + mask
        m = jnp.maximum(jnp.max(s, axis=-1, keepdims=True), sink)
        p = jnp.exp(s - m)
        den = jnp.sum(p, axis=-1, keepdims=True) + jnp.exp(sink - m)
        o = _dot(p.astype(BF16), vsl) / den
        for r in range(0, REP, 2):
            pair = jnp.concatenate([o[r * WBLK:(r + 1) * WBLK], o[(r + 1) * WBLK:(r + 2) * WBLK]], axis=1)
            c0 = (kvh * REP + r) * HEAD_DIM
            o_ref[:, c0:c0 + 2 * HEAD_DIM] = pair.astype(BF16)


def _attn_masks(nctx):
    nwin = 3 * WBLK
    qi = (jnp.arange(REP * WBLK) & (WBLK - 1))[:, None]
    kj = jnp.arange(nwin + nctx)[None, :]
    band = (kj >= qi) & (kj <= qi + 2 * WBLK)
    masks = []
    for first in (False, True):
        for last in (False, True):
            lo = WBLK if first else 0
            hi = 2 * WBLK if last else nwin
            valid = (kj >= nwin) | (band & (kj >= lo) & (kj < hi))
            masks.append(jnp.where(valid, 0.0, NEG_BIG))
    return jnp.stack(masks).astype(F32)


def _mix(sink, gu, vn, q, k, v, kctx, vctx, ws, bs_full, batch, seq):
    t = gu.shape[0]
    nblk = seq // WBLK
    nctx = kctx.shape[0] // batch
    masks = _attn_masks(nctx)
    mspec = pl.BlockSpec((1,) + masks.shape[1:],
                         lambda n: (jnp.where(n == 0, 2, 0) + jnp.where(n == nblk - 1, 1, 0), 0, 0))

    cur = lambda w: pl.BlockSpec((batch, WBLK, w), lambda n: (0, n, 0))
    prev = pl.BlockSpec((batch, WBLK, D_KV), lambda n: (0, jnp.maximum(n - 1, 0), 0))
    nxt = pl.BlockSpec((batch, WBLK, D_KV), lambda n: (0, jnp.minimum(n + 1, nblk - 1), 0))
    cblk = pl.BlockSpec((batch, nctx, D_KV), lambda n: (0, 0, 0))
    per_seq = lambda x: x.reshape(batch, -1, x.shape[-1])
    gu, vn, q, k, v, kctx, vctx = map(per_seq, (gu, vn, q, k, v, kctx, vctx))
    a, o = pl.pallas_call(
        _mix_kernel,
        grid=(nblk,),
        in_specs=[pl.BlockSpec(memory_space=pltpu.SMEM),
                  cur(D_A), cur(D_A), cur(D_Q),
                  prev, cur(D_KV), nxt, prev, cur(D_KV), nxt,
                  cblk, cblk,
                  pl.BlockSpec((G_A, CHUNK, CHUNK), lambda n: (0, 0, 0)),
                  pl.BlockSpec((CHUNK, D_A), lambda n: (0, 0)),
                  mspec],
        out_specs=[cur(D_A), cur(D_Q)],
        out_shape=[jax.ShapeDtypeStruct((batch, seq, D_A), BF16), jax.ShapeDtypeStruct((batch, seq, D_Q), BF16)],
        compiler_params=_params("arbitrary"),
        name="mix",
    )(sink, gu, vn, q, k, k, k, v, v, v, kctx, vctx, ws, bs_full, masks)
    return a.reshape(t, D_A), o.reshape(t, D_Q)


def _store_row_tiles(ref, val, wref):
    rows = val.shape[0]
    for s in range(ROW_WORDS):
        lo = val[:, (2 * s) * LANES:(2 * s + 1) * LANES]
        hi = val[:, (2 * s + 1) * LANES:(2 * s + 2) * LANES]
        wref[pl.ds(s, rows, stride=ROW_WORDS), :] = pltpu.pack_elementwise([lo, hi], packed_dtype=BF16)
    ref[...] = pltpu.bitcast(wref[...], ROW_DT)


def _load_row_tiles(wref, s):
    words = wref[pl.ds(s, wref.shape[0] // ROW_WORDS, stride=ROW_WORDS), :]
    return tuple(pltpu.unpack_elementwise(words, index=i, packed_dtype=BF16, unpacked_dtype=F32) for i in (0, 1))


def _merge_kernel(a_ref, o_ref, sga_ref, sgb_ref, x_ref, mod_ref, g2_ref, wa_ref, wb_ref, wo_ref,
                  rwh_ref, rwl_ref, rb_ref,
                  x1_ref, h2t_ref, h2b_ref, idx_ref, gate_ref, rank_ref, cnt_ref, base_ref, words_ref):
    i = pl.program_id(0)

    @pl.when(i == 0)
    def _():
        base_ref[...] = jnp.zeros_like(base_ref)

    mod = mod_ref[0]
    ya = _dot(a_ref[...], wa_ref[...])
    yb = _dot(o_ref[...], wb_ref[...])
    y = sga_ref[...].astype(F32) * ya + sgb_ref[...].astype(F32) * yb
    x1 = x_ref[...] + mod[2:3] * _dot(y.astype(BF16), wo_ref[...])
    x1_ref[...] = x1
    h2 = _rms_mod(x1, g2_ref[...], mod[3:4], mod[4:5])
    h2b_ref[...] = h2.astype(BF16)
    _store_row_tiles(h2t_ref, h2, words_ref)

    hh = h2.astype(BF16)
    hl = (h2 - hh.astype(F32)).astype(BF16)
    dn = (((1,), (1,)), ((), ()))
    logits = (lax.dot_general(rwh_ref[...], hh, dn, preferred_element_type=F32)
              + lax.dot_general(rwl_ref[...], hh, dn, preferred_element_type=F32)
              + lax.dot_general(rwh_ref[...], hl, dn, preferred_element_type=F32))
    scores = jax.nn.sigmoid(logits)
    tm = scores.shape[1]
    eio = lax.broadcasted_iota(jnp.int32, scores.shape, 0).astype(F32)
    work = scores + rb_ref[...]
    picked = jnp.zeros(scores.shape, F32)
    idxs, vals = [], []
    for _ in range(TOP_K):
        m = jnp.max(work, axis=0, keepdims=True)
        ik = jnp.min(jnp.where(work == m, eio, float(N_EXPERTS)), axis=0, keepdims=True)
        oh = eio == ik
        vals.append(jnp.sum(jnp.where(oh, scores, 0.0), axis=0, keepdims=True))
        idxs.append(ik)
        work = jnp.where(oh, -jnp.inf, work)
        picked = picked + oh.astype(F32)
    total = vals[0]
    for vk in vals[1:]:
        total = total + vk

    tr = lax.broadcasted_iota(jnp.int32, (tm, tm), 0)
    tc = lax.broadcasted_iota(jnp.int32, (tm, tm), 1)
    before = (tr < tc).astype(BF16)
    base = base_ref[...]
    rank_e = _dot(picked.astype(BF16), before) + base
    ranks = [jnp.sum(jnp.where(eio == ik, rank_e, 0.0), axis=0, keepdims=True) for ik in idxs]
    base = base + jnp.sum(picked, axis=1, keepdims=True)
    base_ref[...] = base
    cnt_ref[...] = base

    idx_ref[...] = jnp.concatenate(idxs, axis=0).astype(jnp.int32)
    rank_ref[...] = jnp.concatenate(ranks, axis=0).astype(jnp.int32)
    gate_ref[...] = jnp.concatenate(vals, axis=0) * (ROUTE_SCALE / total)


def _merge(a, o, sga, sgb, x2d, mods, g2, wa, wb, wo, rwh, rwl, rb, seq):
    t = x2d.shape[0]
    tm = min(TM_MERGE, seq)
    spb = seq // tm
    row = lambda w: pl.BlockSpec((tm, w), lambda i: (i, 0))
    col = lambda: pl.BlockSpec((TOP_K, tm), lambda i: (0, i))
    const = lambda s: pl.BlockSpec(s, lambda i: (0,) * len(s))
    return pl.pallas_call(
        _merge_kernel,
        grid=(t // tm,),
        in_specs=[row(D_A), row(D_Q), row(D_MODEL), row(D_MODEL), row(D_MODEL),
                  pl.BlockSpec((1, 6, D_MODEL), lambda i: (i // spb, 0, 0)),
                  const((1, D_MODEL)),
                  const((D_A, D_MODEL)), const((D_Q, D_MODEL)), const((D_MODEL, D_MODEL)),
                  const((N_EXPERTS, D_MODEL)), const((N_EXPERTS, D_MODEL)), const((N_EXPERTS, 1))],
        out_specs=[row(D_MODEL), pl.BlockSpec((tm * ROW_TILES, LANES), lambda i: (i, 0)), row(D_MODEL),
                   col(), col(), col(), const((N_EXPERTS, 1))],
        out_shape=[jax.ShapeDtypeStruct((t, D_MODEL), F32),
                   jax.ShapeDtypeStruct((t * ROW_TILES, LANES), ROW_DT),
                   jax.ShapeDtypeStruct((t, D_MODEL), BF16),
                   jax.ShapeDtypeStruct((TOP_K, t), jnp.int32),
                   jax.ShapeDtypeStruct((TOP_K, t), F32),
                   jax.ShapeDtypeStruct((TOP_K, t), jnp.int32),
                   jax.ShapeDtypeStruct((N_EXPERTS, 1), F32)],
        scratch_shapes=[pltpu.VMEM((N_EXPERTS, 1), F32), pltpu.VMEM((tm * ROW_WORDS, LANES), WORD_DT)],
        compiler_params=_params("arbitrary"),
        name="merge",
    )(a, o, sga, sgb, x2d, mods, g2, wa, wb, wo, rwh, rwl, rb)


def _row_copy(src, s_off, dst, d_off, sem):
    return pltpu.make_async_copy(src.at[pl.ds(pl.multiple_of(s_off, ROW_TILES), ROW_TILES), :],
                                 dst.at[pl.ds(pl.multiple_of(d_off, ROW_TILES), ROW_TILES), :], sem)


def _rows_wait(ref, nrows, sem):
    n = nrows * ROW_TILES
    pltpu.make_async_copy(ref.at[pl.ds(0, n), :], ref.at[pl.ds(0, n), :], sem).wait()


def _dispatch_kernel(nsteps, pad0_ref, padn_ref, nval_ref, dest_ref, h_ref, xs_ref, zero_ref, sem, zsem):
    i = pl.program_id(0)
    tm = h_ref.shape[0] // ROW_TILES
    nblocks = xs_ref.shape[0] // (EXP_BLK * ROW_TILES)
    experts_per_step = -(-N_EXPERTS // nsteps)
    tail_per_step = -(-nblocks // nsteps)

    @pl.when(i == 0)
    def _():
        zero_ref[...] = jnp.zeros_like(zero_ref)

    def body(j, carry):
        for k in range(TOP_K):
            _row_copy(h_ref, j * ROW_TILES, xs_ref, dest_ref[0, 0, j * TOP_K + k], sem).start(priority=k % 2)
        return carry

    lax.fori_loop(0, tm, body, 0)

    def zero_fill(act):
        def pad_body(r, carry):
            e = i * experts_per_step + r

            @pl.when(e < N_EXPERTS)
            def _():
                first = pad0_ref[e]
                n = padn_ref[e]
                bit = EXP_BLK // 2
                while bit:
                    off = first + (n & ~(2 * bit - 1))

                    @pl.when((n & bit) != 0)
                    def _(bit=bit, off=off):
                        act(pltpu.make_async_copy(
                            zero_ref.at[pl.ds(0, bit * ROW_TILES), :],
                            xs_ref.at[pl.ds(pl.multiple_of(off * ROW_TILES, ROW_TILES), bit * ROW_TILES), :], zsem))
                    bit //= 2
            return carry

        lax.fori_loop(0, experts_per_step, pad_body, 0)

        def tail_body(r, carry):
            blk = nval_ref[0] + i * tail_per_step + r

            @pl.when(blk < nblocks)
            def _():
                rows = EXP_BLK * ROW_TILES
                act(pltpu.make_async_copy(zero_ref, xs_ref.at[pl.ds(pl.multiple_of(blk * rows, rows), rows), :],
                                          zsem))
            return carry

        lax.fori_loop(0, tail_per_step, tail_body, 0)

    zero_fill(lambda cp: cp.start())
    _rows_wait(xs_ref, tm * TOP_K, sem)
    zero_fill(lambda cp: cp.wait())


def _dispatch(pad0, padn, nval, dest, h2t, nrows):
    t = h2t.shape[0] // ROW_TILES
    tm = dest.shape[2] // TOP_K
    nsteps = t // tm
    grid_spec = pltpu.PrefetchScalarGridSpec(
        num_scalar_prefetch=3,
        grid=(nsteps,),
        in_specs=[pl.BlockSpec((1, 1, tm * TOP_K), lambda i, *_: (i, 0, 0), memory_space=pltpu.SMEM),
                  pl.BlockSpec((tm * ROW_TILES, LANES), lambda i, *_: (i, 0))],
        out_specs=pl.BlockSpec(memory_space=pl.ANY),
        scratch_shapes=[pltpu.VMEM((EXP_BLK * ROW_TILES, LANES), ROW_DT),
                        pltpu.SemaphoreType.DMA(()), pltpu.SemaphoreType.DMA(())],
    )
    return pl.pallas_call(
        functools.partial(_dispatch_kernel, nsteps),
        grid_spec=grid_spec,
        out_shape=jax.ShapeDtypeStruct((nrows * ROW_TILES, LANES), ROW_DT),
        compiler_params=_params("arbitrary"),
        name="dispatch",
    )(pad0, padn, nval, dest, h2t)


def _expert_kernel(bsrc_ref, nval_ref, first_ref, run_ref, rexp_ref, nruns_ref,
                   xs_ref, wg_hbm, wu_hbm, wd_hbm, ys_ref,
                   xb_ref, xw_ref, wg_f, wu_f, wd_f, wg_b, wu_b, wd_b, wsem):
    b = pl.program_id(0)
    nruns = nruns_ref[0]

    def weights(j, act):
        e = rexp_ref[j]
        s = j % W_SLOTS
        for hbm, buf in ((wg_hbm, wg_f), (wu_hbm, wu_f), (wd_hbm, wd_f)):
            act(pltpu.make_async_copy(hbm.at[e], buf.at[s], wsem.at[s]))

    @pl.when(b == 0)
    def _():
        weights(0, lambda cp: cp.start())

        @pl.when(nruns > 1)
        def _():
            weights(1, lambda cp: cp.start())

    @pl.when(b < nval_ref[0])
    def _():
        j = run_ref[b]

        @pl.when(first_ref[b] == 1)
        def _():
            weights(j, lambda cp: cp.wait())

            @pl.when(j + 2 < nruns)
            def _():
                weights(j + 2, lambda cp: cp.start())

            s = j % W_SLOTS
            wg_b[...] = wg_f[s].astype(BF16)
            wu_b[...] = wu_f[s].astype(BF16)
            wd_b[...] = wd_f[s].astype(BF16)

        xw_ref[...] = pltpu.bitcast(xs_ref[...], WORD_DT)
        for s in range(ROW_WORDS):
            for h, part in enumerate(_load_row_tiles(xw_ref, s)):
                c = 2 * s + h
                xb_ref[:, c * LANES:(c + 1) * LANES] = part.astype(BF16)
        x = xb_ref[...]
        g = _dot(x, wg_b[...])
        u = _dot(x, wu_b[...])
        act = (_silu(g) * u).astype(BF16)
        _store_row_tiles(ys_ref, _dot(act, wd_b[...]), xw_ref)

    @pl.when(b >= nval_ref[0])
    def _():
        ys_ref[...] = jnp.zeros_like(ys_ref)


def _experts(bexp, bsrc, nval, xs, wg, wu, wd):
    nrows = xs.shape[0] // ROW_TILES
    nb = nrows // EXP_BLK
    blk = (EXP_BLK * ROW_TILES, LANES)
    blocks = jnp.arange(nb, dtype=jnp.int32)
    first = ((blocks == 0) | (bexp != jnp.roll(bexp, 1))).astype(jnp.int32)
    run = jnp.cumsum(first) - 1
    runs = jnp.arange(N_EXPERTS, dtype=jnp.int32)
    rexp = jnp.sum(jnp.where((first[None, :] == 1) & (run[None, :] == runs[:, None]), bexp[None, :], 0), axis=1)
    nruns = (run[nb - 1] + 1).reshape(1)
    grid_spec = pltpu.PrefetchScalarGridSpec(
        num_scalar_prefetch=6,
        grid=(nb,),
        in_specs=[pl.BlockSpec(blk, lambda b, bs, *_: (bs[b], 0)),
                  pl.BlockSpec(memory_space=pl.ANY), pl.BlockSpec(memory_space=pl.ANY),
                  pl.BlockSpec(memory_space=pl.ANY)],
        out_specs=pl.BlockSpec(blk, lambda b, *_: (b, 0)),
        scratch_shapes=[pltpu.VMEM((EXP_BLK, D_MODEL), BF16),
                        pltpu.VMEM((EXP_BLK * ROW_WORDS, LANES), WORD_DT),
                        pltpu.VMEM((W_SLOTS, D_MODEL, D_EXPERT), F32),
                        pltpu.VMEM((W_SLOTS, D_MODEL, D_EXPERT), F32),
                        pltpu.VMEM((W_SLOTS, D_EXPERT, D_MODEL), F32),
                        pltpu.VMEM((D_MODEL, D_EXPERT), BF16),
                        pltpu.VMEM((D_MODEL, D_EXPERT), BF16),
                        pltpu.VMEM((D_EXPERT, D_MODEL), BF16),
                        pltpu.SemaphoreType.DMA((W_SLOTS,))],
    )
    return pl.pallas_call(
        _expert_kernel,
        grid_spec=grid_spec,
        out_shape=jax.ShapeDtypeStruct((nrows * ROW_TILES, LANES), ROW_DT),
        compiler_params=_params("arbitrary"),
        name="experts",
    )(bsrc, nval, first, run, rexp, nruns, xs, wg, wu, wd)


def _combine_kernel(dest_ref, next_ref, ys_ref, gate_ref, x1_ref, h2_ref, mod_ref, sg_ref, su_ref, sd_ref, fg_ref,
                    out_ref, buf_ref, bufw_ref, x2_ref, gk_ref, sem):
    i = pl.program_id(0)
    tm = x1_ref.shape[0]
    slot = i % 2

    def gather(ids_ref, s):
        def body(j, carry):
            for k in range(TOP_K):
                _row_copy(ys_ref, ids_ref[0, 0, j * TOP_K + k], buf_ref.at[s, k], j * ROW_TILES,
                          sem.at[s]).start(priority=k % 2)
            return carry
        lax.fori_loop(0, tm, body, 0)

    @pl.when(i == 0)
    def _():
        gather(dest_ref, 0)

    nxt = 1 - slot

    def fetch_next(j):
        for k in range(TOP_K):
            _row_copy(ys_ref, next_ref[0, 0, j * TOP_K + k], buf_ref.at[nxt, k], j * ROW_TILES,
                      sem.at[nxt]).start(priority=k % 2)

    for j in range(COMBINE_PRE):
        fetch_next(j)

    mod = mod_ref[0]
    h = h2_ref[...]
    act = (_silu(_dot(h, sg_ref[...])) * _dot(h, su_ref[...])).astype(BF16)
    moe = _dot(act, sd_ref[...])

    g = gate_ref[...]
    g0 = g.astype(BF16)
    r1 = g - g0.astype(F32)
    g1 = r1.astype(BF16)
    g2 = (r1 - g1.astype(F32)).astype(BF16)
    eye = (lax.broadcasted_iota(jnp.int32, (tm, tm), 0)
           == lax.broadcasted_iota(jnp.int32, (tm, tm), 1)).astype(BF16)
    dn = (((1,), (1,)), ((), ()))
    gcol = (lax.dot_general(eye, g0, dn, preferred_element_type=F32)
            + lax.dot_general(eye, g1, dn, preferred_element_type=F32)
            + lax.dot_general(eye, g2, dn, preferred_element_type=F32))

    for k in range(TOP_K):
        gk_ref[k] = jnp.broadcast_to(gcol[:, k:k + 1], (tm, LANES))
    x2_ref[...] = moe
    for k in range(TOP_K):
        _rows_wait(buf_ref.at[slot, k], tm, sem.at[slot])
    for k in range(TOP_K):
        bufw_ref[k] = pltpu.bitcast(buf_ref[slot, k], WORD_DT)
    nchunks = tm // COMBINE_ROWS
    per_chunk = (tm - COMBINE_PRE) // nchunks

    def chunk(rc, carry):
        r0 = pl.multiple_of(rc * COMBINE_ROWS, COMBINE_ROWS)
        rs = pl.ds(r0, COMBINE_ROWS)
        for jj in range(per_chunk):
            fetch_next(COMBINE_PRE + rc * per_chunk + jj)
        ssq = jnp.zeros((COMBINE_ROWS, 1), F32)
        for s in range(ROW_WORDS):
            acc = [x2_ref[rs, (2 * s + h) * LANES:(2 * s + h + 1) * LANES] for h in range(2)]
            for k in range(TOP_K):
                words = bufw_ref[k, pl.ds(r0 * ROW_WORDS + s, COMBINE_ROWS, stride=ROW_WORDS), :]
                g = gk_ref[k, rs, :]
                for h in range(2):
                    acc[h] = acc[h] + g * pltpu.unpack_elementwise(words, index=h, packed_dtype=BF16,
                                                                   unpacked_dtype=F32)
            for h in range(2):
                sl = slice((2 * s + h) * LANES, (2 * s + h + 1) * LANES)
                x2 = x1_ref[rs, sl] + mod[5:6, sl] * acc[h]
                x2_ref[rs, sl] = x2
                ssq = ssq + jnp.sum(x2 * x2, axis=-1, keepdims=True)
        out_ref[rs, :] = x2_ref[rs, :] * lax.rsqrt(ssq * (1.0 / D_MODEL) + EPS) * fg_ref[...]
        return carry

    assert COMBINE_PRE + nchunks * per_chunk == tm
    lax.fori_loop(0, nchunks, chunk, 0)

    @pl.when(i == pl.num_programs(0) - 1)
    def _():
        for k in range(TOP_K):
            _rows_wait(buf_ref.at[nxt, k], tm, sem.at[nxt])


def _combine(dest, ys, gate, x1, h2, mods, sg, su, sd, fg, seq):
    t = x1.shape[0]
    tm = dest.shape[2] // TOP_K
    spb = seq // tm
    nt = t // tm
    row = lambda: pl.BlockSpec((tm, D_MODEL), lambda i: (i, 0))
    const = lambda s: pl.BlockSpec(s, lambda i: (0,) * len(s))
    ids = lambda f: pl.BlockSpec((1, 1, tm * TOP_K), f, memory_space=pltpu.SMEM)
    return pl.pallas_call(
        _combine_kernel,
        grid=(nt,),
        in_specs=[ids(lambda i: (i, 0, 0)), ids(lambda i: (jnp.minimum(i + 1, nt - 1), 0, 0)),
                  pl.BlockSpec(memory_space=pl.ANY),
                  pl.BlockSpec((TOP_K, tm), lambda i: (0, i)),
                  row(), row(),
                  pl.BlockSpec((1, 6, D_MODEL), lambda i: (i // spb, 0, 0)),
                  const((D_MODEL, D_SHARED)), const((D_MODEL, D_SHARED)), const((D_SHARED, D_MODEL)),
                  const((1, D_MODEL))],
        out_specs=row(),
        out_shape=jax.ShapeDtypeStruct((t, D_MODEL), F32),
        scratch_shapes=[pltpu.VMEM((2, TOP_K, tm * ROW_TILES, LANES), ROW_DT),
                        pltpu.VMEM((TOP_K, tm * ROW_WORDS, LANES), WORD_DT), pltpu.VMEM((tm, D_MODEL), F32),
                        pltpu.VMEM((TOP_K, tm, LANES), F32),
                        pltpu.SemaphoreType.DMA((2,))],
        compiler_params=_params("arbitrary"),
        name="combine",
    )(dest, dest, ys, gate, x1, h2, mods, sg, su, sd, fg)


def _rope_tables(seq):
    pos = jnp.arange(seq)
    n_freq = HEAD_DIM // 4
    inv = ROPE_THETA ** (-jnp.arange(n_freq, dtype=F32) / n_freq)
    ang_r = (pos // GRID_W)[:, None].astype(F32) * inv
    ang_c = (pos % GRID_W)[:, None].astype(F32) * inv
    cr, sr, cc, sc = jnp.cos(ang_r), jnp.sin(ang_r), jnp.cos(ang_c), jnp.sin(ang_c)
    cos = jnp.concatenate([cr, cr, cc, cc], axis=1)
    sin = jnp.concatenate([-sr, sr, -sc, sc], axis=1)
    reps = LANES // HEAD_DIM
    return jnp.tile(cos, (1, reps)), jnp.tile(sin, (1, reps))


def _layer(x2d, ctx2d, mods, batch, seq, norm1_g, norm2_g, w_in, ln_g, ln_b, gmlp_ws, gmlp_bs, sink,
           w_a, w_b, w_o, router_w, router_b, e_gate, e_up, e_down, s_gate, s_up, s_down, final_g):
    t = x2d.shape[0]
    g1 = norm1_g.reshape(1, D_MODEL)
    w_in_b = w_in.astype(BF16)
    cos, sin = _rope_tables(seq)

    kctx, vctx = _ctx_kv(ctx2d, mods, g1, w_in_b[:, C_K:C_GA])
    gu, vn, q, k, v, sga, sgb = _inproj(x2d, mods, g1, w_in_b, ln_g.reshape(1, D_A), ln_b.reshape(1, D_A),
                                        cos, sin, seq)

    bs_full = jnp.repeat(gmlp_bs.T, D_A // G_A, axis=1)
    a, o = _mix(sink, gu, vn, q, k, v, kctx, vctx, gmlp_ws.astype(BF16), bs_full, batch, seq)

    rwt = router_w.T
    rwh = rwt.astype(BF16)
    rwl = (rwt - rwh.astype(F32)).astype(BF16)
    x1, h2t, h2b, idx, gate, rank, counts = _merge(
        a, o, sga, sgb, x2d, mods, norm2_g.reshape(1, D_MODEL),
        w_a.astype(BF16), w_b.astype(BF16), w_o.astype(BF16), rwh, rwl, router_b.reshape(N_EXPERTS, 1), seq)

    cnt = counts[:, 0].astype(jnp.int32)
    pcnt = (cnt + EXP_BLK - 1) // EXP_BLK * EXP_BLK
    pend = jnp.cumsum(pcnt)
    pstart = pend - pcnt
    onehot = idx[:, :, None] == jnp.arange(N_EXPERTS, dtype=jnp.int32)
    dest = jnp.sum(jnp.where(onehot, pstart, 0), axis=-1) + rank
    nblocks = (t * TOP_K) // EXP_BLK + N_EXPERTS
    nval = (pend[-1] // EXP_BLK).astype(jnp.int32)
    bsrc = jnp.minimum(jnp.arange(nblocks, dtype=jnp.int32), nval - 1)
    bexp = jnp.sum((pend[None, :] <= (bsrc * EXP_BLK)[:, None]).astype(jnp.int32), axis=1)
    bexp = jnp.minimum(bexp, N_EXPERTS - 1)

    nval = nval.reshape(1)
    tm = min(TM_DMA, seq)
    dest = (dest * ROW_TILES).T.reshape(t // tm, 1, tm * TOP_K)
    xs = _dispatch(pstart + cnt, pcnt - cnt, nval, dest, h2t, nblocks * EXP_BLK)
    ys = _experts(bexp, bsrc, nval, xs, e_gate, e_up, e_down)
    return _combine(dest, ys, gate, x1, h2b, mods, s_gate.astype(BF16), s_up.astype(BF16),
                    s_down.astype(BF16), final_g.reshape(1, D_MODEL), seq)


def kernel(x, c, ctx, c_ctx, ada_w, ada_b, norm1_g, norm2_g, w_in, gmlp_ln_g, gmlp_ln_b, gmlp_ws, gmlp_bs,
           attn_sink, w_branch_a, w_branch_b, w_out, router_w, router_b, exp_w_gate, exp_w_up, exp_w_down,
           sh_w_gate, sh_w_up, sh_w_down, final_g):
    batch, seq, _ = x.shape
    depth = ada_w.shape[0]
    assert depth == 1, "the context stream is only carried as keys/values of a single layer"
    assert batch + 1 <= 8 and seq % WBLK == 0
    cond = jnp.concatenate([c, c_ctx[None], jnp.zeros((8 - batch - 1, D_MODEL), F32)], axis=0)
    mods = _ada(cond, ada_w[0], ada_b[0])[:batch + 1].reshape(batch + 1, 6, D_MODEL)
    out = _layer(x.reshape(batch * seq, D_MODEL), ctx.reshape(-1, D_MODEL), mods, batch, seq,
                 norm1_g[0], norm2_g[0], w_in[0], gmlp_ln_g[0], gmlp_ln_b[0], gmlp_ws[0], gmlp_bs[0],
                 attn_sink[0], w_branch_a[0], w_branch_b[0], w_out[0], router_w[0], router_b[0],
                 exp_w_gate[0], exp_w_up[0], exp_w_down[0], sh_w_gate[0], sh_w_up[0], sh_w_down[0], final_g)
    return out.reshape(batch, seq, D_MODEL)
```

```python
import functools

import jax
import jax.numpy as jnp
from jax import lax
from jax.experimental import pallas as pl
from jax.experimental.pallas import tpu as pltpu

F32 = jnp.float32
BF16 = jnp.bfloat16

D_MODEL = 1024
EPS = 1e-6
GRID_W = 64
D_A = D_MODEL // 2
G_A = 4
CHUNK = 128
N_HEADS = 8
N_KV = 2
REP = N_HEADS // N_KV
HEAD_DIM = 64
D_Q = N_HEADS * HEAD_DIM
D_KV = N_KV * HEAD_DIM
WBLK = 128
ROPE_THETA = 10000.0
N_EXPERTS = 256
TOP_K = 8
D_EXPERT = D_MODEL // 4
D_SHARED = D_MODEL // 4
ROUTE_SCALE = 2.5

C_U = 0
C_V = D_A
C_Q = 2 * D_A
C_K = C_Q + D_Q
C_VAL = C_K + D_KV
C_GA = C_VAL + D_KV
C_GB = C_GA + D_MODEL
D_IN = C_GB + D_MODEL

LANES = 128
ROPE_HALF = HEAD_DIM // 4
NEG_BIG = -1e30

TM_PROJ = 512
TM_MERGE = 512
TM_DMA = 256
COMBINE_ROWS = 32
COMBINE_PRE = 64
EXP_BLK = 256
W_SLOTS = 3
ROW_DT = BF16
ROW_TILES = D_MODEL // LANES
ROW_WORDS = ROW_TILES // 2
WORD_DT = jnp.uint32
VMEM_LIMIT = 56 * 1024 * 1024


def _gelu(x):
    return 0.5 * x * (1.0 + jnp.tanh(0.7978845608028654 * (x + 0.044715 * x * x * x)))


def _silu(x):
    return x * jax.nn.sigmoid(x)


def _dot(a, b):
    return jnp.dot(a, b, preferred_element_type=F32)


def _rms_mod(x, g, shift, scale):
    ms = jnp.mean(x * x, axis=-1, keepdims=True)
    return (x * lax.rsqrt(ms + EPS)) * g * (1.0 + scale) + shift


def _params(*sem):
    return pltpu.CompilerParams(dimension_semantics=sem, vmem_limit_bytes=VMEM_LIMIT)


def _ada_kernel(c_ref, w_ref, b_ref, o_ref):
    c = c_ref[...]
    s = _silu(c).astype(BF16)
    o_ref[...] = _dot(s, w_ref[...].astype(BF16)) + b_ref[...]


def _ada(cond8, ada_w, ada_b):
    n = ada_w.shape[1]
    tn = 1536
    return pl.pallas_call(
        _ada_kernel,
        grid=(n // tn,),
        in_specs=[pl.BlockSpec((8, D_MODEL), lambda j: (0, 0)),
                  pl.BlockSpec((D_MODEL, tn), lambda j: (0, j)),
                  pl.BlockSpec((1, tn), lambda j: (0, j))],
        out_specs=pl.BlockSpec((8, tn), lambda j: (0, j)),
        out_shape=jax.ShapeDtypeStruct((8, n), F32),
        compiler_params=_params("arbitrary"),
        name="ada",
    )(cond8, ada_w, ada_b.reshape(1, n))


def _ctx_kernel(x_ref, mod_ref, g_ref, w_ref, k_ref, v_ref):
    mod = mod_ref[0]
    h = _rms_mod(x_ref[...], g_ref[...], mod[0:1], mod[1:2]).astype(BF16)
    z = _dot(h, w_ref[...])
    k_ref[...] = z[:, :D_KV].astype(BF16)
    v_ref[...] = z[:, D_KV:].astype(BF16)


def _ctx_kv(ctx2d, mods, g1, w_kv):
    n = ctx2d.shape[0]
    nb = mods.shape[0] - 1
    return pl.pallas_call(
        _ctx_kernel,
        grid=(1,),
        in_specs=[pl.BlockSpec((n, D_MODEL), lambda i: (0, 0)),
                  pl.BlockSpec((1, 6, D_MODEL), lambda i: (nb, 0, 0)),
                  pl.BlockSpec((1, D_MODEL), lambda i: (0, 0)),
                  pl.BlockSpec((D_MODEL, 2 * D_KV), lambda i: (0, 0))],
        out_specs=[pl.BlockSpec((n, D_KV), lambda i: (0, 0)),
                   pl.BlockSpec((n, D_KV), lambda i: (0, 0))],
        out_shape=[jax.ShapeDtypeStruct((n, D_KV), BF16)] * 2,
        compiler_params=_params("arbitrary"),
        name="ctx_kv",
    )(ctx2d, mods, g1, w_kv)


def _rope(t, cos, sin):
    lane = lax.broadcasted_iota(jnp.int32, (t.shape[0], LANES), 1)
    first = (lane & (2 * ROPE_HALF - 1)) < ROPE_HALF
    outs = []
    for j in range(t.shape[1] // LANES):
        tj = t[:, j * LANES:(j + 1) * LANES]
        up = pltpu.roll(tj, LANES - ROPE_HALF, 1)
        dn = pltpu.roll(tj, ROPE_HALF, 1)
        outs.append(tj * cos + jnp.where(first, up, dn) * sin)
    return outs


def _inproj_kernel(x_ref, mod_ref, g_ref, w_ref, lng_ref, lnb_ref, cos_ref, sin_ref,
                   gu_ref, vn_ref, q_ref, k_ref, v_ref, h_ref):
    mod = mod_ref[0]
    h = _rms_mod(x_ref[...], g_ref[...], mod[0:1], mod[1:2]).astype(BF16)

    def proj(lo, hi):
        return _dot(h, w_ref[:, lo:hi])

    gu_ref[...] = _gelu(proj(C_U, C_V)).astype(BF16)

    v = _gelu(proj(C_V, C_Q))
    mu = jnp.mean(v, axis=-1, keepdims=True)
    vc = v - mu
    var = jnp.mean(vc * vc, axis=-1, keepdims=True)
    vn_ref[...] = (vc * lax.rsqrt(var + EPS) * lng_ref[...] + lnb_ref[...]).astype(BF16)

    cos = cos_ref[...]
    sin = sin_ref[...]
    q = _rope(proj(C_Q, C_K) * (HEAD_DIM ** -0.5), cos, sin)
    for j, qj in enumerate(q):
        q_ref[:, j * LANES:(j + 1) * LANES] = qj.astype(BF16)
    k = _rope(proj(C_K, C_VAL), cos, sin)
    k_ref[...] = k[0].astype(BF16)
    v_ref[...] = proj(C_VAL, C_GA).astype(BF16)
    h_ref[...] = h


def _inproj(x2d, mods, g1, w_in, lng, lnb, cos, sin, seq):
    t = x2d.shape[0]
    tm = min(TM_PROJ, seq)
    spb = seq // tm
    row = lambda w: pl.BlockSpec((tm, w), lambda i: (i, 0))
    const = lambda s: pl.BlockSpec(s, lambda i: (0,) * len(s))
    return pl.pallas_call(
        _inproj_kernel,
        grid=(t // tm,),
        in_specs=[row(D_MODEL),
                  pl.BlockSpec((1, 6, D_MODEL), lambda i: (i // spb, 0, 0)),
                  const((1, D_MODEL)),
                  const((D_MODEL, C_GA)),
                  const((1, D_A)), const((1, D_A)),
                  pl.BlockSpec((tm, LANES), lambda i: (i % spb, 0)),
                  pl.BlockSpec((tm, LANES), lambda i: (i % spb, 0))],
        out_specs=[row(D_A), row(D_A), row(D_Q), row(D_KV), row(D_KV), row(D_MODEL)],
        out_shape=[jax.ShapeDtypeStruct((t, w), BF16)
                   for w in (D_A, D_A, D_Q, D_KV, D_KV, D_MODEL)],
        compiler_params=_params("arbitrary"),
        name="inproj",
    )(x2d, mods, g1, w_in, lng, lnb, cos, sin)


def _mix_kernel(sink_ref, *refs):
    seq_refs, (ws_ref, bs_ref, mask_ref), out_refs = refs[:11], refs[11:14], refs[14:]
    for b in range(seq_refs[0].shape[0]):
        _mix_block(sink_ref, *[r.at[b] for r in seq_refs], ws_ref, bs_ref, mask_ref, *[r.at[b] for r in out_refs])


def _mix_block(sink_ref, gu_ref, vn_ref, q_ref, kp_ref, kc_ref, kn_ref, vp_ref, vc_ref, vx_ref,
               kctx_ref, vctx_ref, ws_ref, bs_ref, mask_ref, a_ref, o_ref):
    for g in range(G_A):
        sl = slice(g * CHUNK, (g + 1) * CHUNK)
        s = _dot(ws_ref[g], vn_ref[:, sl]) + bs_ref[:, sl]
        a_ref[:, sl] = (gu_ref[:, sl].astype(F32) * s).astype(BF16)

    kcat = jnp.concatenate([kp_ref[...], kc_ref[...], kn_ref[...], kctx_ref[...]], axis=0)
    vcat = jnp.concatenate([vp_ref[...], vc_ref[...], vx_ref[...], vctx_ref[...]], axis=0)
    rows = REP * WBLK
    mask = mask_ref[0]
    rgrp = lax.broadcasted_iota(jnp.int32, (rows, 1), 0) // WBLK

    for kvh in range(N_KV):
        ksl = kcat[:, kvh * HEAD_DIM:(kvh + 1) * HEAD_DIM]
        vsl = vcat[:, kvh * HEAD_DIM:(kvh + 1) * HEAD_DIM]
        qs = jnp.concatenate(
            [q_ref[:, (kvh * REP + r) * HEAD_DIM:(kvh * REP + r + 1) * HEAD_DIM] for r in range(REP)],
            axis=0)
        sink = jnp.zeros((rows, 1), F32)
        for r in range(REP):
            sink = jnp.where(rgrp == r, sink_ref[kvh * REP + r], sink)
        s = lax.dot_general(qs, ksl, (((1,), (1,)), ((), ())), preferred_element_type=F32)
        s = s + mask
        m = jnp.maximum(jnp.max(s, axis=-1, keepdims=True), sink)
        p = jnp.exp(s - m)
        den = jnp.sum(p, axis=-1, keepdims=True) + jnp.exp(sink - m)
        o = _dot(p.astype(BF16), vsl) / den
        for r in range(0, REP, 2):
            pair = jnp.concatenate([o[r * WBLK:(r + 1) * WBLK], o[(r + 1) * WBLK:(r + 2) * WBLK]], axis=1)
            c0 = (kvh * REP + r) * HEAD_DIM
            o_ref[:, c0:c0 + 2 * HEAD_DIM] = pair.astype(BF16)


def _attn_masks(nctx):
    nwin = 3 * WBLK
    qi = (jnp.arange(REP * WBLK) & (WBLK - 1))[:, None]
    kj = jnp.arange(nwin + nctx)[None, :]
    band = (kj >= qi) & (kj <= qi + 2 * WBLK)
    masks = []
    for first in (False, True):
        for last in (False, True):
            lo = WBLK if first else 0
            hi = 2 * WBLK if last else nwin
            valid = (kj >= nwin) | (band & (kj >= lo) & (kj < hi))
            masks.append(jnp.where(valid, 0.0, NEG_BIG))
    return jnp.stack(masks).astype(F32)


def _mix(sink, gu, vn, q, k, v, kctx, vctx, ws, bs_full, batch, seq):
    t = gu.shape[0]
    nblk = seq // WBLK
    nctx = kctx.shape[0] // batch
    masks = _attn_masks(nctx)
    mspec = pl.BlockSpec((1,) + masks.shape[1:],
                         lambda n: (jnp.where(n == 0, 2, 0) + jnp.where(n == nblk - 1, 1, 0), 0, 0))

    cur = lambda w: pl.BlockSpec((batch, WBLK, w), lambda n: (0, n, 0))
    prev = pl.BlockSpec((batch, WBLK, D_KV), lambda n: (0, jnp.maximum(n - 1, 0), 0))
    nxt = pl.BlockSpec((batch, WBLK, D_KV), lambda n: (0, jnp.minimum(n + 1, nblk - 1), 0))
    cblk = pl.BlockSpec((batch, nctx, D_KV), lambda n: (0, 0, 0))
    per_seq = lambda x: x.reshape(batch, -1, x.shape[-1])
    gu, vn, q, k, v, kctx, vctx = map(per_seq, (gu, vn, q, k, v, kctx, vctx))
    a, o = pl.pallas_call(
        _mix_kernel,
        grid=(nblk,),
        in_specs=[pl.BlockSpec(memory_space=pltpu.SMEM),
                  cur(D_A), cur(D_A), cur(D_Q),
                  prev, cur(D_KV), nxt, prev, cur(D_KV), nxt,
                  cblk, cblk,
                  pl.BlockSpec((G_A, CHUNK, CHUNK), lambda n: (0, 0, 0)),
                  pl.BlockSpec((CHUNK, D_A), lambda n: (0, 0)),
                  mspec],
        out_specs=[cur(D_A), cur(D_Q)],
        out_shape=[jax.ShapeDtypeStruct((batch, seq, D_A), BF16), jax.ShapeDtypeStruct((batch, seq, D_Q), BF16)],
        compiler_params=_params("arbitrary"),
        name="mix",
    )(sink, gu, vn, q, k, k, k, v, v, v, kctx, vctx, ws, bs_full, masks)
    return a.reshape(t, D_A), o.reshape(t, D_Q)


def _store_row_tiles(ref, val, wref):
    rows = val.shape[0]
    for s in range(ROW_WORDS):
        lo = val[:, (2 * s) * LANES:(2 * s + 1) * LANES]
        hi = val[:, (2 * s + 1) * LANES:(2 * s + 2) * LANES]
        wref[pl.ds(s, rows, stride=ROW_WORDS), :] = pltpu.pack_elementwise([lo, hi], packed_dtype=BF16)
    ref[...] = pltpu.bitcast(wref[...], ROW_DT)


def _load_row_tiles(wref, s):
    words = wref[pl.ds(s, wref.shape[0] // ROW_WORDS, stride=ROW_WORDS), :]
    return tuple(pltpu.unpack_elementwise(words, index=i, packed_dtype=BF16, unpacked_dtype=F32) for i in (0, 1))


def _merge_kernel(a_ref, o_ref, hg_ref, wgate_ref, x_ref, mod_ref, g2_ref, wa_ref, wb_ref, wo_ref,
                  rwh_ref, rwl_ref, rb_ref,
                  x1_ref, h2t_ref, h2b_ref, idx_ref, gate_ref, rank_ref, cnt_ref, base_ref, words_ref):
    i = pl.program_id(0)

    @pl.when(i == 0)
    def _():
        base_ref[...] = jnp.zeros_like(base_ref)

    mod = mod_ref[0]
    ya = _dot(a_ref[...], wa_ref[...])
    yb = _dot(o_ref[...], wb_ref[...])
    hg = hg_ref[...]
    ga = jax.nn.sigmoid(_dot(hg, wgate_ref[:, :D_MODEL]))
    gb = jax.nn.sigmoid(_dot(hg, wgate_ref[:, D_MODEL:]))
    y = ga * ya + gb * yb
    x1 = x_ref[...] + mod[2:3] * _dot(y.astype(BF16), wo_ref[...])
    x1_ref[...] = x1
    h2 = _rms_mod(x1, g2_ref[...], mod[3:4], mod[4:5])
    h2b_ref[...] = h2.astype(BF16)
    _store_row_tiles(h2t_ref, h2, words_ref)

    hh = h2.astype(BF16)
    hl = (h2 - hh.astype(F32)).astype(BF16)
    dn = (((1,), (1,)), ((), ()))
    logits = (lax.dot_general(rwh_ref[...], hh, dn, preferred_element_type=F32)
              + lax.dot_general(rwl_ref[...], hh, dn, preferred_element_type=F32)
              + lax.dot_general(rwh_ref[...], hl, dn, preferred_element_type=F32))
    scores = jax.nn.sigmoid(logits)
    tm = scores.shape[1]
    eio = lax.broadcasted_iota(jnp.int32, scores.shape, 0).astype(F32)
    work = scores + rb_ref[...]
    picked = jnp.zeros(scores.shape, F32)
    idxs, vals = [], []
    for _ in range(TOP_K):
        m = jnp.max(work, axis=0, keepdims=True)
        ik = jnp.min(jnp.where(work == m, eio, float(N_EXPERTS)), axis=0, keepdims=True)
        oh = eio == ik
        vals.append(jnp.sum(jnp.where(oh, scores, 0.0), axis=0, keepdims=True))
        idxs.append(ik)
        work = jnp.where(oh, -jnp.inf, work)
        picked = picked + oh.astype(F32)
    total = vals[0]
    for vk in vals[1:]:
        total = total + vk

    tr = lax.broadcasted_iota(jnp.int32, (tm, tm), 0)
    tc = lax.broadcasted_iota(jnp.int32, (tm, tm), 1)
    before = (tr < tc).astype(BF16)
    base = base_ref[...]
    rank_e = _dot(picked.astype(BF16), before) + base
    ranks = [jnp.sum(jnp.where(eio == ik, rank_e, 0.0), axis=0, keepdims=True) for ik in idxs]
    base = base + jnp.sum(picked, axis=1, keepdims=True)
    base_ref[...] = base
    cnt_ref[...] = base

    idx_ref[...] = jnp.concatenate(idxs, axis=0).astype(jnp.int32)
    rank_ref[...] = jnp.concatenate(ranks, axis=0).astype(jnp.int32)
    gate_ref[...] = jnp.concatenate(vals, axis=0) * (ROUTE_SCALE / total)


def _merge(a, o, hg, wgate, x2d, mods, g2, wa, wb, wo, rwh, rwl, rb, seq):
    t = x2d.shape[0]
    tm = min(TM_MERGE, seq)
    spb = seq // tm
    row = lambda w: pl.BlockSpec((tm, w), lambda i: (i, 0))
    col = lambda: pl.BlockSpec((TOP_K, tm), lambda i: (0, i))
    const = lambda s: pl.BlockSpec(s, lambda i: (0,) * len(s))
    return pl.pallas_call(
        _merge_kernel,
        grid=(t // tm,),
        in_specs=[row(D_A), row(D_Q), row(D_MODEL), const((D_MODEL, 2 * D_MODEL)), row(D_MODEL),
                  pl.BlockSpec((1, 6, D_MODEL), lambda i: (i // spb, 0, 0)),
                  const((1, D_MODEL)),
                  const((D_A, D_MODEL)), const((D_Q, D_MODEL)), const((D_MODEL, D_MODEL)),
                  const((N_EXPERTS, D_MODEL)), const((N_EXPERTS, D_MODEL)), const((N_EXPERTS, 1))],
        out_specs=[row(D_MODEL), pl.BlockSpec((tm * ROW_TILES, LANES), lambda i: (i, 0)), row(D_MODEL),
                   col(), col(), col(), const((N_EXPERTS, 1))],
        out_shape=[jax.ShapeDtypeStruct((t, D_MODEL), F32),
                   jax.ShapeDtypeStruct((t * ROW_TILES, LANES), ROW_DT),
                   jax.ShapeDtypeStruct((t, D_MODEL), BF16),
                   jax.ShapeDtypeStruct((TOP_K, t), jnp.int32),
                   jax.ShapeDtypeStruct((TOP_K, t), F32),
                   jax.ShapeDtypeStruct((TOP_K, t), jnp.int32),
                   jax.ShapeDtypeStruct((N_EXPERTS, 1), F32)],
        scratch_shapes=[pltpu.VMEM((N_EXPERTS, 1), F32), pltpu.VMEM((tm * ROW_WORDS, LANES), WORD_DT)],
        compiler_params=_params("arbitrary"),
        name="merge",
    )(a, o, hg, wgate, x2d, mods, g2, wa, wb, wo, rwh, rwl, rb)


def _row_copy(src, s_off, dst, d_off, sem):
    return pltpu.make_async_copy(src.at[pl.ds(pl.multiple_of(s_off, ROW_TILES), ROW_TILES), :],
                                 dst.at[pl.ds(pl.multiple_of(d_off, ROW_TILES), ROW_TILES), :], sem)


def _rows_wait(ref, nrows, sem):
    n = nrows * ROW_TILES
    pltpu.make_async_copy(ref.at[pl.ds(0, n), :], ref.at[pl.ds(0, n), :], sem).wait()


def _dispatch_kernel(nsteps, pad0_ref, padn_ref, nval_ref, dest_ref, h_ref, xs_ref, zero_ref, sem, zsem):
    i = pl.program_id(0)
    tm = h_ref.shape[0] // ROW_TILES
    nblocks = xs_ref.shape[0] // (EXP_BLK * ROW_TILES)
    experts_per_step = -(-N_EXPERTS // nsteps)
    tail_per_step = -(-nblocks // nsteps)

    @pl.when(i == 0)
    def _():
        zero_ref[...] = jnp.zeros_like(zero_ref)

    def body(j, carry):
        for k in range(TOP_K):
            _row_copy(h_ref, j * ROW_TILES, xs_ref, dest_ref[0, 0, j * TOP_K + k], sem).start(priority=k % 2)
        return carry

    lax.fori_loop(0, tm, body, 0)

    def zero_fill(act):
        def pad_body(r, carry):
            e = i * experts_per_step + r

            @pl.when(e < N_EXPERTS)
            def _():
                first = pad0_ref[e]
                n = padn_ref[e]
                bit = EXP_BLK // 2
                while bit:
                    off = first + (n & ~(2 * bit - 1))

                    @pl.when((n & bit) != 0)
                    def _(bit=bit, off=off):
                        act(pltpu.make_async_copy(
                            zero_ref.at[pl.ds(0, bit * ROW_TILES), :],
                            xs_ref.at[pl.ds(pl.multiple_of(off * ROW_TILES, ROW_TILES), bit * ROW_TILES), :], zsem))
                    bit //= 2
            return carry

        lax.fori_loop(0, experts_per_step, pad_body, 0)

        def tail_body(r, carry):
            blk = nval_ref[0] + i * tail_per_step + r

            @pl.when(blk < nblocks)
            def _():
                rows = EXP_BLK * ROW_TILES
                act(pltpu.make_async_copy(zero_ref, xs_ref.at[pl.ds(pl.multiple_of(blk * rows, rows), rows), :],
                                          zsem))
            return carry

        lax.fori_loop(0, tail_per_step, tail_body, 0)

    zero_fill(lambda cp: cp.start())
    _rows_wait(xs_ref, tm * TOP_K, sem)
    zero_fill(lambda cp: cp.wait())


def _dispatch(pad0, padn, nval, dest, h2t, nrows):
    t = h2t.shape[0] // ROW_TILES
    tm = dest.shape[2] // TOP_K
    nsteps = t // tm
    grid_spec = pltpu.PrefetchScalarGridSpec(
        num_scalar_prefetch=3,
        grid=(nsteps,),
        in_specs=[pl.BlockSpec((1, 1, tm * TOP_K), lambda i, *_: (i, 0, 0), memory_space=pltpu.SMEM),
                  pl.BlockSpec((tm * ROW_TILES, LANES), lambda i, *_: (i, 0))],
        out_specs=pl.BlockSpec(memory_space=pl.ANY),
        scratch_shapes=[pltpu.VMEM((EXP_BLK * ROW_TILES, LANES), ROW_DT),
                        pltpu.SemaphoreType.DMA(()), pltpu.SemaphoreType.DMA(())],
    )
    return pl.pallas_call(
        functools.partial(_dispatch_kernel, nsteps),
        grid_spec=grid_spec,
        out_shape=jax.ShapeDtypeStruct((nrows * ROW_TILES, LANES), ROW_DT),
        compiler_params=_params("arbitrary"),
        name="dispatch",
    )(pad0, padn, nval, dest, h2t)


def _expert_kernel(bsrc_ref, nval_ref, first_ref, run_ref, rexp_ref, nruns_ref,
                   xs_ref, wg_hbm, wu_hbm, wd_hbm, ys_ref,
                   xb_ref, xw_ref, wg_f, wu_f, wd_f, wg_b, wu_b, wd_b, wsem):
    b = pl.program_id(0)
    nruns = nruns_ref[0]

    def weights(j, act):
        e = rexp_ref[j]
        s = j % W_SLOTS
        for hbm, buf in ((wg_hbm, wg_f), (wu_hbm, wu_f), (wd_hbm, wd_f)):
            act(pltpu.make_async_copy(hbm.at[e], buf.at[s], wsem.at[s]))

    @pl.when(b == 0)
    def _():
        weights(0, lambda cp: cp.start())

        @pl.when(nruns > 1)
        def _():
            weights(1, lambda cp: cp.start())

    @pl.when(b < nval_ref[0])
    def _():
        j = run_ref[b]

        @pl.when(first_ref[b] == 1)
        def _():
            weights(j, lambda cp: cp.wait())

            @pl.when(j + 2 < nruns)
            def _():
                weights(j + 2, lambda cp: cp.start())

            s = j % W_SLOTS
            wg_b[...] = wg_f[s].astype(BF16)
            wu_b[...] = wu_f[s].astype(BF16)
            wd_b[...] = wd_f[s].astype(BF16)

        xw_ref[...] = pltpu.bitcast(xs_ref[...], WORD_DT)
        for s in range(ROW_WORDS):
            for h, part in enumerate(_load_row_tiles(xw_ref, s)):
                c = 2 * s + h
                xb_ref[:, c * LANES:(c + 1) * LANES] = part.astype(BF16)
        x = xb_ref[...]
        g = _dot(x, wg_b[...])
        u = _dot(x, wu_b[...])
        act = (_silu(g) * u).astype(BF16)
        _store_row_tiles(ys_ref, _dot(act, wd_b[...]), xw_ref)

    @pl.when(b >= nval_ref[0])
    def _():
        ys_ref[...] = jnp.zeros_like(ys_ref)


def _experts(bexp, bsrc, nval, xs, wg, wu, wd):
    nrows = xs.shape[0] // ROW_TILES
    nb = nrows // EXP_BLK
    blk = (EXP_BLK * ROW_TILES, LANES)
    blocks = jnp.arange(nb, dtype=jnp.int32)
    first = ((blocks == 0) | (bexp != jnp.roll(bexp, 1))).astype(jnp.int32)
    run = jnp.cumsum(first) - 1
    runs = jnp.arange(N_EXPERTS, dtype=jnp.int32)
    rexp = jnp.sum(jnp.where((first[None, :] == 1) & (run[None, :] == runs[:, None]), bexp[None, :], 0), axis=1)
    nruns = (run[nb - 1] + 1).reshape(1)
    grid_spec = pltpu.PrefetchScalarGridSpec(
        num_scalar_prefetch=6,
        grid=(nb,),
        in_specs=[pl.BlockSpec(blk, lambda b, bs, *_: (bs[b], 0)),
                  pl.BlockSpec(memory_space=pl.ANY), pl.BlockSpec(memory_space=pl.ANY),
                  pl.BlockSpec(memory_space=pl.ANY)],
        out_specs=pl.BlockSpec(blk, lambda b, *_: (b, 0)),
        scratch_shapes=[pltpu.VMEM((EXP_BLK, D_MODEL), BF16),
                        pltpu.VMEM((EXP_BLK * ROW_WORDS, LANES), WORD_DT),
                        pltpu.VMEM((W_SLOTS, D_MODEL, D_EXPERT), F32),
                        pltpu.VMEM((W_SLOTS, D_MODEL, D_EXPERT), F32),
                        pltpu.VMEM((W_SLOTS, D_EXPERT, D_MODEL), F32),
                        pltpu.VMEM((D_MODEL, D_EXPERT), BF16),
                        pltpu.VMEM((D_MODEL, D_EXPERT), BF16),
                        pltpu.VMEM((D_EXPERT, D_MODEL), BF16),
                        pltpu.SemaphoreType.DMA((W_SLOTS,))],
    )
    return pl.pallas_call(
        _expert_kernel,
        grid_spec=grid_spec,
        out_shape=jax.ShapeDtypeStruct((nrows * ROW_TILES, LANES), ROW_DT),
        compiler_params=_params("arbitrary"),
        name="experts",
    )(bsrc, nval, first, run, rexp, nruns, xs, wg, wu, wd)


def _combine_kernel(dest_ref, next_ref, ys_ref, gate_ref, x1_ref, h2_ref, mod_ref, sg_ref, su_ref, sd_ref, fg_ref,
                    out_ref, buf_ref, bufw_ref, x2_ref, gk_ref, sem):
    i = pl.program_id(0)
    tm = x1_ref.shape[0]
    slot = i % 2

    def gather(ids_ref, s):
        def body(j, carry):
            for k in range(TOP_K):
                _row_copy(ys_ref, ids_ref[0, 0, j * TOP_K + k], buf_ref.at[s, k], j * ROW_TILES,
                          sem.at[s]).start(priority=k % 2)
            return carry
        lax.fori_loop(0, tm, body, 0)

    @pl.when(i == 0)
    def _():
        gather(dest_ref, 0)

    nxt = 1 - slot

    def fetch_next(j):
        for k in range(TOP_K):
            _row_copy(ys_ref, next_ref[0, 0, j * TOP_K + k], buf_ref.at[nxt, k], j * ROW_TILES,
                      sem.at[nxt]).start(priority=k % 2)

    for j in range(COMBINE_PRE):
        fetch_next(j)

    mod = mod_ref[0]
    h = h2_ref[...]
    act = (_silu(_dot(h, sg_ref[...])) * _dot(h, su_ref[...])).astype(BF16)
    moe = _dot(act, sd_ref[...])

    g = gate_ref[...]
    g0 = g.astype(BF16)
    r1 = g - g0.astype(F32)
    g1 = r1.astype(BF16)
    g2 = (r1 - g1.astype(F32)).astype(BF16)
    eye = (lax.broadcasted_iota(jnp.int32, (tm, tm), 0)
           == lax.broadcasted_iota(jnp.int32, (tm, tm), 1)).astype(BF16)
    dn = (((1,), (1,)), ((), ()))
    gcol = (lax.dot_general(eye, g0, dn, preferred_element_type=F32)
            + lax.dot_general(eye, g1, dn, preferred_element_type=F32)
            + lax.dot_general(eye, g2, dn, preferred_element_type=F32))

    for k in range(TOP_K):
        gk_ref[k] = jnp.broadcast_to(gcol[:, k:k + 1], (tm, LANES))
    x2_ref[...] = moe
    for k in range(TOP_K):
        _rows_wait(buf_ref.at[slot, k], tm, sem.at[slot])
    for k in range(TOP_K):
        bufw_ref[k] = pltpu.bitcast(buf_ref[slot, k], WORD_DT)
    nchunks = tm // COMBINE_ROWS
    per_chunk = (tm - COMBINE_PRE) // nchunks

    def chunk(rc, carry):
        r0 = pl.multiple_of(rc * COMBINE_ROWS, COMBINE_ROWS)
        rs = pl.ds(r0, COMBINE_ROWS)
        for jj in range(per_chunk):
            fetch_next(COMBINE_PRE + rc * per_chunk + jj)
        ssq = jnp.zeros((COMBINE_ROWS, 1), F32)
        for s in range(ROW_WORDS):
            acc = [x2_ref[rs, (2 * s + h) * LANES:(2 * s + h + 1) * LANES] for h in range(2)]
            for k in range(TOP_K):
                words = bufw_ref[k, pl.ds(r0 * ROW_WORDS + s, COMBINE_ROWS, stride=ROW_WORDS), :]
                g = gk_ref[k, rs, :]
                for h in range(2):
                    acc[h] = acc[h] + g * pltpu.unpack_elementwise(words, index=h, packed_dtype=BF16,
                                                                   unpacked_dtype=F32)
            for h in range(2):
                sl = slice((2 * s + h) * LANES, (2 * s + h + 1) * LANES)
                x2 = x1_ref[rs, sl] + mod[5:6, sl] * acc[h]
                x2_ref[rs, sl] = x2
                ssq = ssq + jnp.sum(x2 * x2, axis=-1, keepdims=True)
        out_ref[rs, :] = x2_ref[rs, :] * lax.rsqrt(ssq * (1.0 / D_MODEL) + EPS) * fg_ref[...]
        return carry

    assert COMBINE_PRE + nchunks * per_chunk == tm
    lax.fori_loop(0, nchunks, chunk, 0)

    @pl.when(i == pl.num_programs(0) - 1)
    def _():
        for k in range(TOP_K):
            _rows_wait(buf_ref.at[nxt, k], tm, sem.at[nxt])


def _combine(dest, ys, gate, x1, h2, mods, sg, su, sd, fg, seq):
    t = x1.shape[0]
    tm = dest.shape[2] // TOP_K
    spb = seq // tm
    nt = t // tm
    row = lambda: pl.BlockSpec((tm, D_MODEL), lambda i: (i, 0))
    const = lambda s: pl.BlockSpec(s, lambda i: (0,) * len(s))
    ids = lambda f: pl.BlockSpec((1, 1, tm * TOP_K), f, memory_space=pltpu.SMEM)
    return pl.pallas_call(
        _combine_kernel,
        grid=(nt,),
        in_specs=[ids(lambda i: (i, 0, 0)), ids(lambda i: (jnp.minimum(i + 1, nt - 1), 0, 0)),
                  pl.BlockSpec(memory_space=pl.ANY),
                  pl.BlockSpec((TOP_K, tm), lambda i: (0, i)),
                  row(), row(),
                  pl.BlockSpec((1, 6, D_MODEL), lambda i: (i // spb, 0, 0)),
                  const((D_MODEL, D_SHARED)), const((D_MODEL, D_SHARED)), const((D_SHARED, D_MODEL)),
                  const((1, D_MODEL))],
        out_specs=row(),
        out_shape=jax.ShapeDtypeStruct((t, D_MODEL), F32),
        scratch_shapes=[pltpu.VMEM((2, TOP_K, tm * ROW_TILES, LANES), ROW_DT),
                        pltpu.VMEM((TOP_K, tm * ROW_WORDS, LANES), WORD_DT), pltpu.VMEM((tm, D_MODEL), F32),
                        pltpu.VMEM((TOP_K, tm, LANES), F32),
                        pltpu.SemaphoreType.DMA((2,))],
        compiler_params=_params("arbitrary"),
        name="combine",
    )(dest, dest, ys, gate, x1, h2, mods, sg, su, sd, fg)


def _rope_tables(seq):
    pos = jnp.arange(seq)
    n_freq = HEAD_DIM // 4
    inv = ROPE_THETA ** (-jnp.arange(n_freq, dtype=F32) / n_freq)
    ang_r = (pos // GRID_W)[:, None].astype(F32) * inv
    ang_c = (pos % GRID_W)[:, None].astype(F32) * inv
    cr, sr, cc, sc = jnp.cos(ang_r), jnp.sin(ang_r), jnp.cos(ang_c), jnp.sin(ang_c)
    cos = jnp.concatenate([cr, cr, cc, cc], axis=1)
    sin = jnp.concatenate([-sr, sr, -sc, sc], axis=1)
    reps = LANES // HEAD_DIM
    return jnp.tile(cos, (1, reps)), jnp.tile(sin, (1, reps))


def _layer(x2d, ctx2d, mods, batch, seq, norm1_g, norm2_g, w_in, ln_g, ln_b, gmlp_ws, gmlp_bs, sink,
           w_a, w_b, w_o, router_w, router_b, e_gate, e_up, e_down, s_gate, s_up, s_down, final_g):
    t = x2d.shape[0]
    g1 = norm1_g.reshape(1, D_MODEL)
    w_in_b = w_in.astype(BF16)
    cos, sin = _rope_tables(seq)

    kctx, vctx = _ctx_kv(ctx2d, mods, g1, w_in_b[:, C_K:C_GA])
    gu, vn, q, k, v, hg = _inproj(x2d, mods, g1, w_in_b[:, :C_GA], ln_g.reshape(1, D_A), ln_b.reshape(1, D_A),
                                        cos, sin, seq)

    bs_full = jnp.repeat(gmlp_bs.T, D_A // G_A, axis=1)
    a, o = _mix(sink, gu, vn, q, k, v, kctx, vctx, gmlp_ws.astype(BF16), bs_full, batch, seq)

    rwt = router_w.T
    rwh = rwt.astype(BF16)
    rwl = (rwt - rwh.astype(F32)).astype(BF16)
    x1, h2t, h2b, idx, gate, rank, counts = _merge(
        a, o, hg, w_in_b[:, C_GA:], x2d, mods, norm2_g.reshape(1, D_MODEL),
        w_a.astype(BF16), w_b.astype(BF16), w_o.astype(BF16), rwh, rwl, router_b.reshape(N_EXPERTS, 1), seq)

    cnt = counts[:, 0].astype(jnp.int32)
    pcnt = (cnt + EXP_BLK - 1) // EXP_BLK * EXP_BLK
    pend = jnp.cumsum(pcnt)
    pstart = pend - pcnt
    onehot = idx[:, :, None] == jnp.arange(N_EXPERTS, dtype=jnp.int32)
    dest = jnp.sum(jnp.where(onehot, pstart, 0), axis=-1) + rank
    nblocks = (t * TOP_K) // EXP_BLK + N_EXPERTS
    nval = (pend[-1] // EXP_BLK).astype(jnp.int32)
    bsrc = jnp.minimum(jnp.arange(nblocks, dtype=jnp.int32), nval - 1)
    bexp = jnp.sum((pend[None, :] <= (bsrc * EXP_BLK)[:, None]).astype(jnp.int32), axis=1)
    bexp = jnp.minimum(bexp, N_EXPERTS - 1)

    nval = nval.reshape(1)
    tm = min(TM_DMA, seq)
    dest = (dest * ROW_TILES).T.reshape(t // tm, 1, tm * TOP_K)
    xs = _dispatch(pstart + cnt, pcnt - cnt, nval, dest, h2t, nblocks * EXP_BLK)
    ys = _experts(bexp, bsrc, nval, xs, e_gate, e_up, e_down)
    return _combine(dest, ys, gate, x1, h2b, mods, s_gate.astype(BF16), s_up.astype(BF16),
                    s_down.astype(BF16), final_g.reshape(1, D_MODEL), seq)


def kernel(x, c, ctx, c_ctx, ada_w, ada_b, norm1_g, norm2_g, w_in, gmlp_ln_g, gmlp_ln_b, gmlp_ws, gmlp_bs,
           attn_sink, w_branch_a, w_branch_b, w_out, router_w, router_b, exp_w_gate, exp_w_up, exp_w_down,
           sh_w_gate, sh_w_up, sh_w_down, final_g):
    batch, seq, _ = x.shape
    depth = ada_w.shape[0]
    assert depth == 1, "the context stream is only carried as keys/values of a single layer"
    assert batch + 1 <= 8 and seq % WBLK == 0
    cond = jnp.concatenate([c, c_ctx[None], jnp.zeros((8 - batch - 1, D_MODEL), F32)], axis=0)
    mods = _ada(cond, ada_w[0], ada_b[0])[:batch + 1].reshape(batch + 1, 6, D_MODEL)
    out = _layer(x.reshape(batch * seq, D_MODEL), ctx.reshape(-1, D_MODEL), mods, batch, seq,
                 norm1_g[0], norm2_g[0], w_in[0], gmlp_ln_g[0], gmlp_ln_b[0], gmlp_ws[0], gmlp_bs[0],
                 attn_sink[0], w_branch_a[0], w_branch_b[0], w_out[0], router_w[0], router_b[0],
                 exp_w_gate[0], exp_w_up[0], exp_w_down[0], sh_w_gate[0], sh_w_up[0], sh_w_down[0], final_g)
    return out.reshape(batch, seq, D_MODEL)
```
